```python
import jax, jax.numpy as jnp
from jax import lax
import numpy as np

D_MODEL = 1024
BATCH = 16
SEQ = 4096
DEPTH = 1
DEC_BATCH = 32
DEC_SEQ = 32
PAST_LEN = 4096

CHUNK = 64
Q_BLOCK = 128
FOX_HEADS = 8
FOX_HEAD_DIM = 64
FOX_DIM = FOX_HEADS * FOX_HEAD_DIM
CONV_DIM = 512
CONV_WIDTH = 3
PLE_DIM = 256
N_EXPERTS = 32
TOP_K = 4
D_FF = 1024
SWIGLU_ALPHA = 1.702
SWIGLU_LIMIT = 7.0
MOE_BLOCK = 128
RMS_EPS = 1e-6
IN_SIZES = (FOX_DIM, FOX_DIM, FOX_DIM, FOX_HEADS, CONV_DIM, CONV_DIM, CONV_DIM, D_MODEL, D_MODEL)
IN_DIM = sum(IN_SIZES)

kernel_name = "fox_shortconv_moe_streaming_step"


def rmsnorm(x, g):
    xf = x.astype(jnp.float32)
    y = xf * lax.rsqrt(jnp.mean(xf * xf, axis=-1, keepdims=True) + RMS_EPS)
    return (y * g.astype(jnp.float32)).astype(x.dtype)


def split_cols(u):
    out, off = [], 0
    for s in IN_SIZES:
        out.append(u[..., off:off + s])
        off += s
    return out


def fox_attend(q, k, v, cq, ck, q_pos, k_pos):
    s = jnp.einsum('bqhd,bkhd->bhqk', q, k, preferred_element_type=jnp.float32) * (FOX_HEAD_DIM ** -0.5)
    s = s + (jnp.transpose(cq, (0, 2, 1))[:, :, :, None] - jnp.transpose(ck, (0, 2, 1))[:, :, None, :])
    mask = k_pos[None, :] <= q_pos[:, None]
    s = jnp.where(mask[None, None], s, -jnp.inf)
    p = jax.nn.softmax(s, axis=-1)
    return jnp.einsum('bhqk,bkhd->bqhd', p.astype(v.dtype), v)


def fox_prompt(q, k, v, logf):
    b, s = q.shape[:2]
    c = jnp.cumsum(logf, axis=1)
    nb = s // Q_BLOCK
    qb = q.reshape(b, nb, Q_BLOCK, FOX_HEADS, FOX_HEAD_DIM).transpose(1, 0, 2, 3, 4)
    cb = c.reshape(b, nb, Q_BLOCK, FOX_HEADS).transpose(1, 0, 2, 3)
    k_pos = jnp.arange(s)

    def one_block(args):
        q_blk, c_blk, i = args
        q_pos = i * Q_BLOCK + jnp.arange(Q_BLOCK)
        return fox_attend(q_blk, k, v, c_blk, c, q_pos, k_pos)

    o = lax.map(one_block, (qb, cb, jnp.arange(nb)))
    return o.transpose(1, 0, 2, 3, 4).reshape(b, s, FOX_HEADS, FOX_HEAD_DIM)


def fox_sample(q, k, v, logf, past_k, past_v, past_logf):
    p_len = past_k.shape[1]
    t = q.shape[1]
    k_all = jnp.concatenate([past_k.astype(k.dtype), k], axis=1)
    v_all = jnp.concatenate([past_v.astype(v.dtype), v], axis=1)
    c = jnp.cumsum(jnp.concatenate([past_logf.astype(jnp.float32), logf], axis=1), axis=1)
    q_pos = p_len + jnp.arange(t)
    k_pos = jnp.arange(p_len + t)
    return fox_attend(q, k_all, v_all, c[:, p_len:], c, q_pos, k_pos)


def token_mixers(xn, past, w_in, b_f, w_conv, w_pa, w_pb, w_o):
    b, t = xn.shape[:2]
    u = xn @ w_in
    q, k, v, f_logit, gate_b, gate_c, h_conv, gl_a, gl_b = split_cols(u)
    q = q.reshape(b, t, FOX_HEADS, FOX_HEAD_DIM)
    k = k.reshape(b, t, FOX_HEADS, FOX_HEAD_DIM)
    v = v.reshape(b, t, FOX_HEADS, FOX_HEAD_DIM)
    logf = jax.nn.log_sigmoid((f_logit + b_f).astype(jnp.float32))
    if past is None:
        o = fox_prompt(q, k, v, logf)
        conv_prev = jnp.zeros((b, CONV_WIDTH - 1, CONV_DIM), xn.dtype)
    else:
        past_k, past_v, past_logf, conv_prev = past
        o = fox_sample(q, k, v, logf, past_k, past_v, past_logf)
    z = gate_c * h_conv
    zpad = jnp.concatenate([conv_prev.astype(z.dtype), z], axis=1)
    zc = sum(w_conv[i] * zpad[:, i:i + t] for i in range(CONV_WIDTH))
    yb = (gate_b * zc) @ w_pb
    ya = o.reshape(b, t, FOX_DIM) @ w_pa
    merged = jax.nn.sigmoid(gl_a) * ya + jax.nn.sigmoid(gl_b) * yb
    new_state = (k, v, logf, zpad[:, -(CONV_WIDTH - 1):])
    return merged @ w_o, new_state


def moe(xn, w_router, b_router, w_gu, b_gu, w_dn, b_dn):
    xt = xn.reshape(-1, D_MODEL)
    n_tok = xt.shape[0]
    logits = (xt @ w_router).astype(jnp.float32) + b_router.astype(jnp.float32)
    top_val, top_idx = lax.top_k(logits, TOP_K)
    gate = jax.nn.softmax(top_val, axis=-1)
    n_assign = n_tok * TOP_K
    flat_e = top_idx.reshape(-1)
    flat_tok = jnp.repeat(jnp.arange(n_tok, dtype=jnp.int32), TOP_K)
    flat_g = gate.reshape(-1)
    order = jnp.argsort(flat_e)
    e_sorted = flat_e[order]
    counts = jnp.bincount(flat_e, length=N_EXPERTS)
    starts = jnp.cumsum(counts) - counts
    padded = (counts + MOE_BLOCK - 1) // MOE_BLOCK * MOE_BLOCK
    pends = jnp.cumsum(padded)
    pstarts = pends - padded
    dest = pstarts[e_sorted] + jnp.arange(n_assign) - starts[e_sorted]
    n_blocks = -(-n_assign // MOE_BLOCK) + N_EXPERTS
    n_slots = n_blocks * MOE_BLOCK
    slot_tok = jnp.zeros((n_slots,), jnp.int32).at[dest].set(flat_tok[order])
    slot_g = jnp.zeros((n_slots,), jnp.float32).at[dest].set(flat_g[order])
    block_e = jnp.minimum(jnp.searchsorted(pends, jnp.arange(n_blocks) * MOE_BLOCK, side='right'), N_EXPERTS - 1)

    def expert_block(args):
        tok, g, e = args
        xb = xt[tok]
        gu = xb @ w_gu[e] + b_gu[e]
        gt = jnp.minimum(gu[:, :D_FF], SWIGLU_LIMIT)
        up = jnp.clip(gu[:, D_FF:], -SWIGLU_LIMIT, SWIGLU_LIMIT)
        act = (up + 1) * gt * jax.nn.sigmoid(SWIGLU_ALPHA * gt)
        out = act @ w_dn[e] + b_dn[e]
        return out * g[:, None].astype(out.dtype)

    outs = lax.map(expert_block, (slot_tok.reshape(n_blocks, MOE_BLOCK), slot_g.reshape(n_blocks, MOE_BLOCK), block_e))
    y = jnp.zeros_like(xt).at[slot_tok].add(outs.reshape(n_slots, D_MODEL).astype(xt.dtype))
    return y.reshape(xn.shape)


def layer(h, p, past, g_mix, w_in, b_f, w_conv, w_pa, w_pb, w_o, g_ffn, w_router, b_router,
          w_gu, b_gu, w_dn, b_dn, g_ple, w_ple_gate, w_ple_proj):
    mix, new_state = token_mixers(rmsnorm(h, g_mix), past, w_in, b_f, w_conv, w_pa, w_pb, w_o)
    h = h + mix
    h = h + moe(rmsnorm(h, g_ffn), w_router, b_router, w_gu, b_gu, w_dn, b_dn)
    h = h + jax.nn.sigmoid(rmsnorm(h, g_ple) @ w_ple_gate) * (p @ w_ple_proj)
    return h, new_state


def setup_inputs(seed: int = 0) -> dict:
    key = jax.random.key(seed)
    ks = jax.random.split(key, 32)
    nrm = jax.random.normal
    f32 = jnp.float32
    return {
        "x_prompt": nrm(ks[0], (BATCH, SEQ, D_MODEL), f32),
        "x_sample": nrm(ks[1], (DEC_BATCH, DEC_SEQ, D_MODEL), f32),
        "p_prompt": nrm(ks[2], (DEPTH, BATCH, SEQ, PLE_DIM), f32),
        "p_sample": nrm(ks[3], (DEPTH, DEC_BATCH, DEC_SEQ, PLE_DIM), f32),
        "cache_k": nrm(ks[4], (DEPTH, DEC_BATCH, PAST_LEN, FOX_HEADS, FOX_HEAD_DIM), f32),
        "cache_v": nrm(ks[5], (DEPTH, DEC_BATCH, PAST_LEN, FOX_HEADS, FOX_HEAD_DIM), f32),
        "cache_logf": jax.nn.log_sigmoid(3.0 + nrm(ks[6], (DEPTH, DEC_BATCH, PAST_LEN, FOX_HEADS), f32)),
        "state_conv": nrm(ks[7], (DEPTH, DEC_BATCH, CONV_WIDTH - 1, CONV_DIM), f32),
        "g_mix": 1.0 + 0.05 * nrm(ks[8], (DEPTH, D_MODEL), f32),
        "w_in": nrm(ks[9], (DEPTH, D_MODEL, IN_DIM), f32) * D_MODEL ** -0.5,
        "b_f": jax.random.uniform(ks[10], (DEPTH, FOX_HEADS), f32, 1.0, 5.0),
        "w_conv": nrm(ks[11], (DEPTH, CONV_WIDTH, CONV_DIM), f32) * CONV_WIDTH ** -0.5,
        "w_pa": nrm(ks[12], (DEPTH, FOX_DIM, D_MODEL), f32) * FOX_DIM ** -0.5,
        "w_pb": nrm(ks[13], (DEPTH, CONV_DIM, D_MODEL), f32) * CONV_DIM ** -0.5,
        "w_o": nrm(ks[14], (DEPTH, D_MODEL, D_MODEL), f32) * D_MODEL ** -0.5,
        "g_ffn": 1.0 + 0.05 * nrm(ks[15], (DEPTH, D_MODEL), f32),
        "w_router": nrm(ks[16], (DEPTH, D_MODEL, N_EXPERTS), f32) * D_MODEL ** -0.5,
        "b_router": 0.01 * nrm(ks[17], (DEPTH, N_EXPERTS), f32),
        "w_gu": nrm(ks[18], (DEPTH, N_EXPERTS, D_MODEL, 2 * D_FF), f32) * D_MODEL ** -0.5,
        "b_gu": 0.02 * nrm(ks[19], (DEPTH, N_EXPERTS, 2 * D_FF), f32),
        "w_dn": nrm(ks[20], (DEPTH, N_EXPERTS, D_FF, D_MODEL), f32) * D_FF ** -0.5,
        "b_dn": 0.02 * nrm(ks[21], (DEPTH, N_EXPERTS, D_MODEL), f32),
        "g_ple": 1.0 + 0.05 * nrm(ks[22], (DEPTH, D_MODEL), f32),
        "w_ple_gate": nrm(ks[23], (DEPTH, D_MODEL, D_MODEL), f32) * D_MODEL ** -0.5,
        "w_ple_proj": nrm(ks[24], (DEPTH, PLE_DIM, D_MODEL), f32) * PLE_DIM ** -0.5,
        "g_final": 1.0 + 0.05 * nrm(ks[25], (D_MODEL,), f32),
    }


def reference(x_prompt, x_sample, p_prompt, p_sample, cache_k, cache_v, cache_logf, state_conv,
              g_mix, w_in, b_f, w_conv, w_pa, w_pb, w_o, g_ffn, w_router, b_router,
              w_gu, b_gu, w_dn, b_dn, g_ple, w_ple_gate, w_ple_proj, g_final):
    h_p, h_s = x_prompt, x_sample
    st_p, st_s = [], []
    for i in range(DEPTH):
        lw = (g_mix[i], w_in[i], b_f[i], w_conv[i], w_pa[i], w_pb[i], w_o[i], g_ffn[i], w_router[i],
              b_router[i], w_gu[i], b_gu[i], w_dn[i], b_dn[i], g_ple[i], w_ple_gate[i], w_ple_proj[i])
        h_p, sp = layer(h_p, p_prompt[i], None, *lw)
        h_s, ss = layer(h_s, p_sample[i], (cache_k[i], cache_v[i], cache_logf[i], state_conv[i]), *lw)
        st_p.append(sp)
        st_s.append(ss)
    y_prompt = rmsnorm(h_p, g_final)
    y_sample = rmsnorm(h_s, g_final)
    new_k_prompt = jnp.stack([s[0] for s in st_p])
    new_v_prompt = jnp.stack([s[1] for s in st_p])
    new_logf_prompt = jnp.stack([s[2] for s in st_p])
    new_conv_prompt = jnp.stack([s[3] for s in st_p])
    new_k_sample = jnp.stack([s[0] for s in st_s])
    new_v_sample = jnp.stack([s[1] for s in st_s])
    new_logf_sample = jnp.stack([s[2] for s in st_s])
    new_conv_sample = jnp.stack([s[3] for s in st_s])
    return (y_prompt, y_sample, new_k_prompt, new_v_prompt, new_logf_prompt, new_conv_prompt,
            new_k_sample, new_v_sample, new_logf_sample, new_conv_sample)
```

```python
import functools

import jax
import jax.numpy as jnp
from jax import lax
from jax.experimental import pallas as pl
from jax.experimental.pallas import tpu as pltpu

F32, BF16, I32 = jnp.float32, jnp.bfloat16, jnp.int32

HEADS = 8
HEAD_DIM = 64
FOX_DIM = HEADS * HEAD_DIM
CONV_DIM = 512
CONV_WIDTH = 3
N_EXPERTS = 32
TOP_K = 4
D_FF = 1024
SWIGLU_ALPHA = 1.702
SWIGLU_LIMIT = 7.0
RMS_EPS = 1e-6

LANES = 128
SUBLANES = 8
TOKEN_TILE = 512
DISPATCH_TILE = 256
ROW_ALIGN = SUBLANES
LOCAL_ROWS = DISPATCH_TILE * TOP_K + N_EXPERTS * ROW_ALIGN
FFN_ROWS = 512
RUN_CHUNKS = (256, 128, 64, 32, 16, 8)
VMEM_LIMIT_BYTES = 56 * 1024 * 1024


def _cparams(*sem):
    return pltpu.CompilerParams(dimension_semantics=sem, vmem_limit_bytes=VMEM_LIMIT_BYTES)


def _rms(x, g):
    return x * lax.rsqrt(jnp.mean(x * x, axis=-1, keepdims=True) + RMS_EPS) * g


def _sigmoid(x):
    return 1.0 / (1.0 + jnp.exp(-x))


def _log_sigmoid(x):
    return jnp.minimum(x, 0.0) - jnp.log1p(jnp.exp(-jnp.abs(x)))


def _dot(a, b):
    return jnp.dot(a, b, preferred_element_type=F32)


def _dot_nt(a, b):
    return lax.dot_general(a, b, (((1,), (1,)), ((), ())), preferred_element_type=F32)


def _mixer_inputs(x_ref, g_ref, wqkv_ref, wf_ref, bf_ref, wc_ref, wgl_ref, q_ref, k_ref, v_ref, logf_ref, sa_ref):
    xn = _rms(x_ref[...], g_ref[...]).astype(BF16)
    qkv = _dot(xn, wqkv_ref[...])
    q_ref[...] = (qkv[:, :FOX_DIM] * (HEAD_DIM ** -0.5)).astype(BF16)
    k_ref[...] = qkv[:, FOX_DIM:2 * FOX_DIM]
    v_ref[...] = qkv[:, 2 * FOX_DIM:]
    f = _dot(xn, wf_ref[...]) + bf_ref[...]
    logf_ref[...] = _log_sigmoid(f)[:, :HEADS]
    c3 = _dot(xn, wc_ref[...])
    gate_b = c3[:, :CONV_DIM]
    z = c3[:, CONV_DIM:2 * CONV_DIM] * c3[:, 2 * CONV_DIM:]
    gl = _dot(xn, wgl_ref[...])
    d = gl.shape[1] // 2
    sa_ref[...] = _sigmoid(gl[:, :d]).astype(BF16)
    return gate_b, z, _sigmoid(gl[:, d:])


def _conv_out(gate_b, z, z1, z2, sig_b, wconv_ref, wpb_ref, mb_ref):
    zc = wconv_ref[0:1, :] * z2 + wconv_ref[1:2, :] * z1 + wconv_ref[2:3, :] * z
    yb = _dot((gate_b * zc).astype(BF16), wpb_ref[...])
    mb_ref[...] = (sig_b * yb).astype(BF16)


def _in_proj_seq_kernel(x_ref, prev_ref, g_ref, wqkv_ref, wf_ref, bf_ref, wc_ref, wconv_ref, wgl_ref, wpb_ref,
                        q_ref, k_ref, v_ref, logf_ref, sa_ref, mb_ref, tail_ref, zbuf_ref, *, tm):
    gate_b, z, sig_b = _mixer_inputs(x_ref, g_ref, wqkv_ref, wf_ref, bf_ref, wc_ref, wgl_ref,
                                     q_ref, k_ref, v_ref, logf_ref, sa_ref)
    zbuf_ref[pl.ds(SUBLANES, tm), :] = z

    @pl.when(pl.program_id(1) == 0)
    def _():
        zbuf_ref[pl.ds(SUBLANES - 2, 2), :] = prev_ref[0]

    z1 = zbuf_ref[pl.ds(SUBLANES - 1, tm), :]
    z2 = zbuf_ref[pl.ds(SUBLANES - 2, tm), :]
    _conv_out(gate_b, z, z1, z2, sig_b, wconv_ref, wpb_ref, mb_ref)
    tail = zbuf_ref[pl.ds(tm + SUBLANES - 2, 2), :]
    zbuf_ref[pl.ds(SUBLANES - 2, 2), :] = tail
    tail_ref[0] = tail


def _in_proj_multi_kernel(x_ref, ov1_ref, ov2_ref, g_ref, wqkv_ref, wf_ref, bf_ref, wc_ref, wconv_ref, wgl_ref,
                          wpb_ref, q_ref, k_ref, v_ref, logf_ref, sa_ref, mb_ref, z_ref, zbuf_ref, *, tm, seq_len):
    gate_b, z, sig_b = _mixer_inputs(x_ref, g_ref, wqkv_ref, wf_ref, bf_ref, wc_ref, wgl_ref,
                                     q_ref, k_ref, v_ref, logf_ref, sa_ref)
    z_ref[...] = z
    zbuf_ref[pl.ds(0, SUBLANES), :] = jnp.zeros((SUBLANES, CONV_DIM), F32)
    zbuf_ref[pl.ds(SUBLANES, tm), :] = z
    t = lax.broadcasted_iota(I32, (tm, CONV_DIM), 0) & (seq_len - 1)
    z1 = jnp.where(t == 0, ov1_ref[...], zbuf_ref[pl.ds(SUBLANES - 1, tm), :])
    z2 = jnp.where(t < 2, ov2_ref[...], zbuf_ref[pl.ds(SUBLANES - 2, tm), :])
    _conv_out(gate_b, z, z1, z2, sig_b, wconv_ref, wpb_ref, mb_ref)


def _full(shape):
    n = len(shape)
    return pl.BlockSpec(shape, lambda *_: (0,) * n)


def _in_proj(x, conv_prev, weights, *, seq_len):
    g, wqkv, wf, bf, wc, wconv, wgl, wpb = weights
    n, d = x.shape
    w_specs = [_full(w.shape) for w in (g, wqkv, wf, bf, wc, wconv, wgl, wpb)]
    out_shape = [jax.ShapeDtypeStruct((n, FOX_DIM), BF16), jax.ShapeDtypeStruct((n, FOX_DIM), F32),
                 jax.ShapeDtypeStruct((n, FOX_DIM), F32), jax.ShapeDtypeStruct((n, HEADS), F32),
                 jax.ShapeDtypeStruct((n, d), BF16), jax.ShapeDtypeStruct((n, d), BF16)]
    if seq_len % TOKEN_TILE == 0:
        tm, nb, nj = TOKEN_TILE, n // seq_len, seq_len // TOKEN_TILE
        row = lambda b, j: (b * nj + j, 0)
        outs = pl.pallas_call(
            functools.partial(_in_proj_seq_kernel, tm=tm),
            out_shape=out_shape + [jax.ShapeDtypeStruct((nb, CONV_WIDTH - 1, CONV_DIM), F32)],
            grid=(nb, nj),
            in_specs=[pl.BlockSpec((tm, d), row), pl.BlockSpec((1, CONV_WIDTH - 1, CONV_DIM), lambda b, j: (b, 0, 0))] + w_specs,
            out_specs=[pl.BlockSpec((tm, FOX_DIM), row), pl.BlockSpec((tm, FOX_DIM), row), pl.BlockSpec((tm, FOX_DIM), row),
                       pl.BlockSpec((tm, HEADS), row), pl.BlockSpec((tm, d), row), pl.BlockSpec((tm, d), row),
                       pl.BlockSpec((1, CONV_WIDTH - 1, CONV_DIM), lambda b, j: (b, 0, 0))],
            scratch_shapes=[pltpu.VMEM((tm + SUBLANES, CONV_DIM), F32)],
            compiler_params=_cparams("arbitrary", "arbitrary"),
            name="in_proj_seq",
        )(x, conv_prev, g, wqkv, wf, bf, wc, wconv, wgl, wpb)
        return outs[:6], outs[6]
    assert seq_len & (seq_len - 1) == 0 and seq_len >= CONV_WIDTH - 1
    tm = TOKEN_TILE
    assert n % tm == 0 and tm % seq_len == 0
    first = jnp.zeros((n // seq_len, seq_len, CONV_DIM), F32)
    ov1 = first.at[:, 0].set(conv_prev[:, 1]).reshape(n, CONV_DIM)
    ov2 = first.at[:, 0].set(conv_prev[:, 0]).at[:, 1].set(conv_prev[:, 1]).reshape(n, CONV_DIM)
    row = lambda i: (i, 0)
    outs = pl.pallas_call(
        functools.partial(_in_proj_multi_kernel, tm=tm, seq_len=seq_len),
        out_shape=out_shape + [jax.ShapeDtypeStruct((n, CONV_DIM), F32)],
        grid=(n // tm,),
        in_specs=[pl.BlockSpec((tm, d), row), pl.BlockSpec((tm, CONV_DIM), row), pl.BlockSpec((tm, CONV_DIM), row)] + w_specs,
        out_specs=[pl.BlockSpec((tm, FOX_DIM), row), pl.BlockSpec((tm, FOX_DIM), row), pl.BlockSpec((tm, FOX_DIM), row),
                   pl.BlockSpec((tm, HEADS), row), pl.BlockSpec((tm, d), row), pl.BlockSpec((tm, d), row),
                   pl.BlockSpec((tm, CONV_DIM), row)],
        scratch_shapes=[pltpu.VMEM((tm + SUBLANES, CONV_DIM), F32)],
        compiler_params=_cparams("arbitrary"),
        name="in_proj_multi",
    )(x, ov1, ov2, g, wqkv, wf, bf, wc, wconv, wgl, wpb)
    tail = outs[6].reshape(n // seq_len, seq_len, CONV_DIM)[:, seq_len - (CONV_WIDTH - 1):]
    return outs[:6], tail


def _cumsum_kernel(lf_ref, c_ref, *, chunk, nchunk):
    r = lax.broadcasted_iota(I32, (chunk, chunk), 0)
    c = lax.broadcasted_iota(I32, (chunk, chunk), 1)
    upper = jnp.where(r <= c, 1.0, 0.0).astype(BF16)
    carry = jnp.zeros((HEADS, 1), F32)
    for n in range(nchunk):
        a = lf_ref[0, n]
        hi = a.astype(BF16)
        r1 = a - hi.astype(F32)
        mid = r1.astype(BF16)
        lo = (r1 - mid.astype(F32)).astype(BF16)
        cs = _dot(hi, upper) + _dot(mid, upper) + _dot(lo, upper) + carry
        c_ref[0, n] = cs
        carry = cs[:, chunk - 1:chunk]


def _cumsum_chunks(lf):
    b, nchunk, _, chunk = lf.shape
    spec = pl.BlockSpec((1, nchunk, HEADS, chunk), lambda i: (i, 0, 0, 0))
    return pl.pallas_call(
        functools.partial(_cumsum_kernel, chunk=chunk, nchunk=nchunk),
        out_shape=jax.ShapeDtypeStruct(lf.shape, F32),
        grid=(b,), in_specs=[spec], out_specs=spec,
        compiler_params=_cparams("arbitrary"),
        name="logf_cumsum",
    )(lf)


def _attn_prompt_kernel(q_ref, k_ref, v_ref, c_ref, o_ref, kb_ref, vb_ref, m_ref, l_ref, acc_ref, *, t):
    hp, i = pl.program_id(1), pl.program_id(2)

    @pl.when(i == 0)
    def _():
        kb_ref[...] = k_ref[...].astype(BF16)
        vb_ref[...] = v_ref[...].astype(BF16)

    lane = lax.broadcasted_iota(I32, (1, LANES), 1)
    low = lane < HEAD_DIM
    q = q_ref[...]
    zq = jnp.zeros_like(q)
    qh = (jnp.where(low, q, zq), jnp.where(low, zq, q))
    m_ref[...] = jnp.full(m_ref.shape, -jnp.inf, F32)
    l_ref[...] = jnp.zeros(l_ref.shape, F32)
    acc_ref[...] = jnp.zeros(acc_ref.shape, F32)

    def step(j, masked):
        start = pl.multiple_of(j * t, t)
        kc = kb_ref[pl.ds(start, t), :]
        vc = vb_ref[pl.ds(start, t), :]
        zv = jnp.zeros_like(vc)
        vh = (jnp.where(low, vc, zv), jnp.where(low, zv, vc))
        pv, alpha = [], []
        for h in range(2):
            s = _dot_nt(qh[h], kc) - c_ref[0, j, pl.ds(2 * hp + h, 1), :]
            if masked:
                qi = lax.broadcasted_iota(I32, (t, t), 0)
                ki = lax.broadcasted_iota(I32, (t, t), 1)
                s = jnp.where(ki <= qi, s, -jnp.inf)
            m_prev = m_ref[h]
            m_new = jnp.maximum(m_prev, jnp.max(s, axis=-1, keepdims=True))
            a = jnp.exp(m_prev - m_new)
            p = jnp.exp(s - m_new)
            l_ref[h] = a * l_ref[h] + jnp.sum(p, axis=-1, keepdims=True)
            m_ref[h] = m_new
            pv.append(_dot(p.astype(BF16), vh[h]))
            alpha.append(a)
        acc_ref[...] = acc_ref[...] * jnp.where(low, alpha[0], alpha[1]) + pv[0] + pv[1]

    def body(j, carry):
        step(j, False)
        return carry

    lax.fori_loop(0, i, body, 0)
    step(i, True)
    o_ref[...] = (acc_ref[...] * jnp.where(low, 1.0 / l_ref[0], 1.0 / l_ref[1])).astype(BF16)


def _attn_prompt(q, k, v, c, *, nb, seq):
    t = c.shape[-1]
    nq = seq // t
    return pl.pallas_call(
        functools.partial(_attn_prompt_kernel, t=t),
        out_shape=jax.ShapeDtypeStruct(q.shape, BF16),
        grid=(nb, HEADS // 2, nq),
        in_specs=[pl.BlockSpec((t, LANES), lambda b, hp, i: (b * nq + i, hp)),
                  pl.BlockSpec((seq, LANES), lambda b, hp, i: (b, hp)),
                  pl.BlockSpec((seq, LANES), lambda b, hp, i: (b, hp)),
                  pl.BlockSpec((1, nq, HEADS, t), lambda b, hp, i: (b, 0, 0, 0))],
        out_specs=pl.BlockSpec((t, LANES), lambda b, hp, i: (b * nq + i, hp)),
        scratch_shapes=[pltpu.VMEM((seq, LANES), BF16), pltpu.VMEM((seq, LANES), BF16),
                        pltpu.VMEM((2, t, 1), F32), pltpu.VMEM((2, t, 1), F32), pltpu.VMEM((t, LANES), F32)],
        compiler_params=_cparams("arbitrary", "arbitrary", "arbitrary"),
        name="attn_prompt",
    )(q, k, v, c)


def _attn_sample_kernel(q_ref, kc_ref, vc_ref, kn_ref, vn_ref, c_ref, o_ref,
                        qbd_ref, m_ref, l_ref, acc_ref, kpad_ref, vpad_ref, *, nq, nchunk):
    j = pl.program_id(1)
    rows = HEADS * nq
    shift = nq.bit_length() - 1
    row_head = lax.broadcasted_iota(I32, (rows, FOX_DIM), 0) >> shift
    col_head = lax.broadcasted_iota(I32, (rows, FOX_DIM), 1) >> (HEAD_DIM.bit_length() - 1)
    own = row_head == col_head

    @pl.when(j == 0)
    def _():
        qt = jnp.concatenate([q_ref[...]] * HEADS, axis=0)
        qbd_ref[...] = jnp.where(own, qt, jnp.zeros_like(qt))
        m_ref[...] = jnp.full(m_ref.shape, -jnp.inf, F32)
        l_ref[...] = jnp.zeros(l_ref.shape, F32)
        acc_ref[...] = jnp.zeros(acc_ref.shape, F32)

    def update(kb, vb, cvals, visible):
        width = kb.shape[0]
        bias = jnp.concatenate([jnp.broadcast_to(cvals[h:h + 1, :], (nq, width)) for h in range(HEADS)], axis=0)
        s = _dot_nt(qbd_ref[...], kb) - bias
        if visible is not None:
            s = jnp.where(visible, s, -jnp.inf)
        m_prev = m_ref[...]
        m_new = jnp.maximum(m_prev, jnp.max(s, axis=-1, keepdims=True))
        a = jnp.exp(m_prev - m_new)
        p = jnp.exp(s - m_new)
        l_ref[...] = a * l_ref[...] + jnp.sum(p, axis=-1, keepdims=True)
        m_ref[...] = m_new
        acc_ref[...] = acc_ref[...] * a + _dot(p.astype(BF16), vb)

    @pl.when(j < nchunk)
    def _():
        update(kc_ref[0].astype(BF16), vc_ref[0].astype(BF16), c_ref[0, j], None)

    @pl.when(j == nchunk)
    def _():
        kpad_ref[...] = jnp.zeros(kpad_ref.shape, BF16)
        vpad_ref[...] = jnp.zeros(vpad_ref.shape, BF16)
        kpad_ref[pl.ds(0, nq), :] = kn_ref[...].astype(BF16)
        vpad_ref[pl.ds(0, nq), :] = vn_ref[...].astype(BF16)
        ki = lax.broadcasted_iota(I32, (rows, LANES), 1)
        qi = lax.broadcasted_iota(I32, (rows, LANES), 0) & (nq - 1)
        update(kpad_ref[...], vpad_ref[...], c_ref[0, nchunk][:, :LANES], ki <= qi)
        out = jnp.where(own, acc_ref[...] * (1.0 / l_ref[...]), 0.0)
        o = out[0:nq]
        for h in range(1, HEADS):
            o = o + out[h * nq:(h + 1) * nq]
        o_ref[...] = o.astype(BF16)


def _attn_sample(q, k_cache, v_cache, k_new, v_new, c, *, nq):
    nb, past, _ = k_cache.shape
    chunk = c.shape[-1]
    nchunk = past // chunk
    assert nq & (nq - 1) == 0 and nq <= LANES and past % chunk == 0 and c.shape[1] == nchunk + 1
    rows = HEADS * nq
    cache_spec = pl.BlockSpec((1, chunk, FOX_DIM), lambda b, j: (b, jnp.minimum(j, nchunk - 1), 0))
    new_spec = pl.BlockSpec((nq, FOX_DIM), lambda b, j: (b, 0))
    return pl.pallas_call(
        functools.partial(_attn_sample_kernel, nq=nq, nchunk=nchunk),
        out_shape=jax.ShapeDtypeStruct(q.shape, BF16),
        grid=(nb, nchunk + 1),
        in_specs=[new_spec, cache_spec, cache_spec, new_spec, new_spec,
                  pl.BlockSpec((1, nchunk + 1, HEADS, chunk), lambda b, j: (b, 0, 0, 0))],
        out_specs=new_spec,
        scratch_shapes=[pltpu.VMEM((rows, FOX_DIM), BF16), pltpu.VMEM((rows, 1), F32), pltpu.VMEM((rows, 1), F32),
                        pltpu.VMEM((rows, FOX_DIM), F32), pltpu.VMEM((LANES, FOX_DIM), BF16),
                        pltpu.VMEM((LANES, FOX_DIM), BF16)],
        compiler_params=_cparams("arbitrary", "arbitrary"),
        name="attn_sample",
    )(q, k_cache, v_cache, k_new, v_new, c)


def _pick(is_first, a_ref, b_ref):
    return jnp.where(is_first, a_ref[...], b_ref[...])


def _post_attn_kernel(xp_ref, xs_ref, op_ref, os_ref, sap_ref, sas_ref, mbp_ref, mbs_ref,
                      wpa_ref, wo_ref, g_ref, wrt_ref, br_ref,
                      h1_ref, xn_ref, pos_ref, gate_ref, cnt_ref, *, n_first, tm):
    is_p = pl.program_id(0) < n_first
    ya = _dot(_pick(is_p, op_ref, os_ref), wpa_ref[...])
    merged = _pick(is_p, sap_ref, sas_ref).astype(F32) * ya + _pick(is_p, mbp_ref, mbs_ref).astype(F32)
    h1 = _pick(is_p, xp_ref, xs_ref) + _dot(merged.astype(BF16), wo_ref[...])
    h1_ref[...] = h1
    xn = _rms(h1, g_ref[...]).astype(BF16)
    xn_ref[...] = xn

    lt = _dot_nt(wrt_ref[...], xn) + br_ref[...]
    eio = lax.broadcasted_iota(I32, (N_EXPERTS, tm), 0).astype(F32)
    vals, hots = [], []
    for _ in range(TOP_K):
        m = jnp.max(lt, axis=0, keepdims=True)
        idx = jnp.min(jnp.where(lt == m, eio, float(N_EXPERTS)), axis=0, keepdims=True)
        hot = eio == idx
        vals.append(m)
        hots.append(hot)
        lt = jnp.where(hot, -jnp.inf, lt)
    ex = [jnp.exp(v - vals[0]) for v in vals]
    den = ex[0] + ex[1] + ex[2] + ex[3]
    gate_ref[...] = jnp.concatenate([e / den for e in ex], axis=0)

    chosen = jnp.zeros((N_EXPERTS, tm), F32)
    for hot in hots:
        chosen = jnp.where(hot, 1.0, chosen)
    td = DISPATCH_TILE
    r = lax.broadcasted_iota(I32, (td, td), 0)
    c = lax.broadcasted_iota(I32, (td, td), 1)
    before = jnp.where(r < c, 1.0, 0.0).astype(BF16)
    er = lax.broadcasted_iota(I32, (N_EXPERTS, N_EXPERTS), 0)
    ec = lax.broadcasted_iota(I32, (N_EXPERTS, N_EXPERTS), 1)
    lower = jnp.where(ec < er, 1.0, 0.0).astype(BF16)
    for sub in range(tm // td):
        sl = slice(sub * td, (sub + 1) * td)
        ch = chosen[:, sl]
        rank = _dot(ch.astype(BF16), before)
        cnt = rank[:, td - 1:td] + ch[:, td - 1:td]
        units = jnp.floor((cnt + (ROW_ALIGN - 1)) * (1.0 / ROW_ALIGN))
        start = ROW_ALIGN * _dot(lower, jnp.broadcast_to(units, (N_EXPERTS, td)).astype(BF16))
        base = start + rank
        pos = [jnp.sum(jnp.where(hot[:, sl], base, 0.0), axis=0, keepdims=True) for hot in hots]
        pos_ref[:, sl] = jnp.concatenate(pos, axis=0).astype(I32)
        cnt_ref[sub] = jnp.broadcast_to(cnt, (N_EXPERTS, LANES)).astype(I32)


def _post_attn(xp, xs, op, os_, sap, sas, mbp, mbs, wpa, wo, g, wrt, br):
    n_p, d = xp.shape
    n_s = xs.shape[0]
    tm = TOKEN_TILE
    assert n_p % tm == 0 and n_s % tm == 0
    n_first, n_tiles = n_p // tm, (n_p + n_s) // tm
    n = n_p + n_s
    sub = tm // DISPATCH_TILE
    first = lambda i: (jnp.minimum(i, n_first - 1), 0)
    second = lambda i: (jnp.maximum(i - n_first, 0), 0)
    row = lambda i: (i, 0)
    col = lambda i: (0, i)

    def pair(width):
        return [pl.BlockSpec((tm, width), first), pl.BlockSpec((tm, width), second)]

    return pl.pallas_call(
        functools.partial(_post_attn_kernel, n_first=n_first, tm=tm),
        out_shape=[jax.ShapeDtypeStruct((n, d), F32), jax.ShapeDtypeStruct((n, d), BF16),
                   jax.ShapeDtypeStruct((TOP_K, n), I32), jax.ShapeDtypeStruct((TOP_K, n), F32),
                   jax.ShapeDtypeStruct((n // DISPATCH_TILE, N_EXPERTS, LANES), I32)],
        grid=(n_tiles,),
        in_specs=pair(d) + pair(FOX_DIM) + pair(d) + pair(d) + [_full(w.shape) for w in (wpa, wo, g, wrt, br)],
        out_specs=[pl.BlockSpec((tm, d), row), pl.BlockSpec((tm, d), row),
                   pl.BlockSpec((TOP_K, tm), col), pl.BlockSpec((TOP_K, tm), col),
                   pl.BlockSpec((sub, N_EXPERTS, LANES), lambda i: (i, 0, 0))],
        compiler_params=_cparams("arbitrary"),
        name="post_attn_router",
    )(xp, xs, op, os_, sap, sas, mbp, mbs, wpa, wo, g, wrt, br)


def _for_each_run(tile, len_ref, loc_ref, dst_ref, fn):
    def body(e, carry):
        idx = tile * N_EXPERTS + e
        n, loc, dst = len_ref[idx], loc_ref[idx], dst_ref[idx]
        for c in RUN_CHUNKS:
            off = n & ~(2 * c - 1)

            @pl.when((n & c) != 0)
            def _():
                fn(pl.multiple_of(loc + off, ROW_ALIGN), pl.multiple_of(dst + off, ROW_ALIGN), c)
        return carry

    lax.fori_loop(0, N_EXPERTS, body, 0)


def _dispatch_kernel(len_ref, loc_ref, dst_ref, xn_ref, pos_ref, xs_hbm, xloc_ref, sem_ref, *, n_tiles):
    i = pl.program_id(0)
    slot = lax.rem(i, 2)

    def copy(s, loc, dst, rows):
        return pltpu.make_async_copy(xloc_ref.at[s, pl.ds(loc, rows), :], xs_hbm.at[pl.ds(dst, rows), :], sem_ref.at[s])

    def start_tile(tile, s):
        _for_each_run(tile, len_ref, loc_ref, dst_ref, lambda loc, dst, rows: copy(s, loc, dst, rows).start())

    def wait_tile(tile, s):
        _for_each_run(tile, len_ref, loc_ref, dst_ref, lambda loc, dst, rows: copy(s, loc, dst, rows).wait())

    pos = pos_ref[...]
    aio = lax.broadcasted_iota(I32, (LOCAL_ROWS, DISPATCH_TILE), 0)
    hit = aio == pos[0:1, :]
    for k in range(1, TOP_K):
        hit = jnp.logical_or(hit, aio == pos[k:k + 1, :])
    sel = jnp.where(hit, 1.0, 0.0).astype(BF16)
    rows_sorted = _dot(sel, xn_ref[...])

    @pl.when(i >= 2)
    def _():
        wait_tile(i - 2, slot)

    xloc_ref[slot] = rows_sorted
    start_tile(i, slot)

    @pl.when(i == n_tiles - 1)
    def _():
        if n_tiles >= 2:
            wait_tile(i - 1, 1 - slot)
        wait_tile(i, slot)


def _dispatch(run_len, run_loc, run_dst, xn, pos, *, total_rows):
    n, d = xn.shape
    td = DISPATCH_TILE
    n_tiles = n // td
    return pl.pallas_call(
        functools.partial(_dispatch_kernel, n_tiles=n_tiles),
        out_shape=jax.ShapeDtypeStruct((total_rows, d), F32),
        grid_spec=pltpu.PrefetchScalarGridSpec(
            num_scalar_prefetch=3, grid=(n_tiles,),
            in_specs=[pl.BlockSpec((td, d), lambda i, *_: (i, 0)), pl.BlockSpec((TOP_K, td), lambda i, *_: (0, i))],
            out_specs=pl.BlockSpec(memory_space=pl.ANY),
            scratch_shapes=[pltpu.VMEM((2, LOCAL_ROWS, d), F32), pltpu.SemaphoreType.DMA((2,))]),
        compiler_params=_cparams("arbitrary"),
        name="moe_dispatch",
    )(run_len, run_loc, run_dst, xn, pos)


def _ffn_kernel(bexp_ref, nval_ref, last_ref, xs_ref, wgu_ref, bgu_ref, wdn_ref, bdn_ref, o_ref):
    nv = nval_ref[pl.program_id(0)]

    @pl.when(nv > 0)
    def _():
        rows = lax.broadcasted_iota(I32, xs_ref.shape, 0)
        x = jnp.where(rows < nv, xs_ref[...], 0.0).astype(BF16)
        gu = _dot(x, wgu_ref[0]) + bgu_ref[0]
        gt = jnp.minimum(gu[:, :D_FF], SWIGLU_LIMIT)
        up = jnp.clip(gu[:, D_FF:], -SWIGLU_LIMIT, SWIGLU_LIMIT)
        act = (up + 1.0) * gt * _sigmoid(SWIGLU_ALPHA * gt)
        o_ref[...] = _dot(act.astype(BF16), wdn_ref[0]) + bdn_ref[0]


def _ffn(bexp, nval, last, xs, wgu, bgu, wdn, bdn):
    rows, d = xs.shape
    nblk = rows // FFN_ROWS
    blk = lambda i, bexp, nval, last: (jnp.minimum(i, last[0]), 0)
    exp3 = lambda i, bexp, nval, last: (bexp[i], 0, 0)
    return pl.pallas_call(
        _ffn_kernel,
        out_shape=jax.ShapeDtypeStruct((rows, d), F32),
        grid_spec=pltpu.PrefetchScalarGridSpec(
            num_scalar_prefetch=3, grid=(nblk,),
            in_specs=[pl.BlockSpec((FFN_ROWS, d), blk),
                      pl.BlockSpec((1, d, 2 * D_FF), exp3), pl.BlockSpec((1, 1, 2 * D_FF), exp3),
                      pl.BlockSpec((1, D_FF, d), exp3), pl.BlockSpec((1, 1, d), exp3)],
            out_specs=pl.BlockSpec((FFN_ROWS, d), blk)),
        compiler_params=_cparams("arbitrary"),
        name="moe_ffn",
    )(bexp, nval, last, xs, wgu, bgu, wdn, bdn)


def _combine_kernel(len_ref, loc_ref, dst_ref, os_hbm, pos_ref, gate_ref, h1_ref, pp_ref, ps_ref,
                    gple_ref, wpg_ref, wpp_ref, gfin_ref, yp_ref, ys_ref, oloc_ref, sem_ref, *, n_tiles, n_first):
    i = pl.program_id(0)
    slot = lax.rem(i, 2)

    def copy(s, loc, dst, rows):
        return pltpu.make_async_copy(os_hbm.at[pl.ds(dst, rows), :], oloc_ref.at[s, pl.ds(loc, rows), :], sem_ref.at[s])

    def start_tile(tile, s):
        _for_each_run(tile, len_ref, loc_ref, dst_ref, lambda loc, dst, rows: copy(s, loc, dst, rows).start())

    def wait_tile(tile, s):
        _for_each_run(tile, len_ref, loc_ref, dst_ref, lambda loc, dst, rows: copy(s, loc, dst, rows).wait())

    @pl.when(i == 0)
    def _():
        oloc_ref[...] = jnp.zeros(oloc_ref.shape, F32)
        start_tile(0, 0)

    @pl.when(i + 1 < n_tiles)
    def _():
        start_tile(i + 1, 1 - slot)

    wait_tile(i, slot)

    pos, gate = pos_ref[...], gate_ref[...]
    lio = lax.broadcasted_iota(I32, (DISPATCH_TILE, LOCAL_ROWS), 1)
    w = jnp.zeros((DISPATCH_TILE, LOCAL_ROWS), F32)
    for k in range(TOP_K):
        w = jnp.where(lio == pos[:, k:k + 1], gate[:, k:k + 1], w)
    h2 = h1_ref[...] + _dot(w.astype(BF16), oloc_ref[slot].astype(BF16))

    is_p = i < n_first
    xn = _rms(h2, gple_ref[...]).astype(BF16)
    ple = _dot(_pick(is_p, pp_ref, ps_ref).astype(BF16), wpp_ref[...])
    h3 = h2 + _sigmoid(_dot(xn, wpg_ref[...])) * ple
    y = _rms(h3, gfin_ref[...])

    @pl.when(is_p)
    def _():
        yp_ref[...] = y

    @pl.when(jnp.logical_not(is_p))
    def _():
        ys_ref[...] = y


def _combine(run_len, run_loc, run_dst, os_, pos, gate, h1, pp, ps, gple, wpg, wpp, gfin):
    n, d = h1.shape
    td = DISPATCH_TILE
    n_p, n_s = pp.shape[0], ps.shape[0]
    assert n_p % td == 0 and n_s % td == 0
    n_tiles, n_first = n // td, n_p // td
    pw = pp.shape[1]
    first = lambda i, *_: (jnp.minimum(i, n_first - 1), 0)
    second = lambda i, *_: (jnp.maximum(i - n_first, 0), 0)
    row = lambda i, *_: (i, 0)
    full2 = lambda i, *_: (0, 0)
    return pl.pallas_call(
        functools.partial(_combine_kernel, n_tiles=n_tiles, n_first=n_first),
        out_shape=[jax.ShapeDtypeStruct((n_p, d), F32), jax.ShapeDtypeStruct((n_s, d), F32)],
        grid_spec=pltpu.PrefetchScalarGridSpec(
            num_scalar_prefetch=3, grid=(n_tiles,),
            in_specs=[pl.BlockSpec(memory_space=pl.ANY),
                      pl.BlockSpec((td, TOP_K), row), pl.BlockSpec((td, TOP_K), row), pl.BlockSpec((td, d), row),
                      pl.BlockSpec((td, pw), first), pl.BlockSpec((td, pw), second),
                      pl.BlockSpec(gple.shape, full2), pl.BlockSpec(wpg.shape, full2),
                      pl.BlockSpec(wpp.shape, full2), pl.BlockSpec(gfin.shape, full2)],
            out_specs=[pl.BlockSpec((td, d), first), pl.BlockSpec((td, d), second)],
            scratch_shapes=[pltpu.VMEM((2, LOCAL_ROWS, d), F32), pltpu.SemaphoreType.DMA((2,))]),
        compiler_params=_cparams("arbitrary"),
        name="moe_combine_ple",
    )(run_len, run_loc, run_dst, os_, pos, gate, h1, pp, ps, gple, wpg, wpp, gfin)


def _routing_tables(cnt, total_rows):
    n_tiles = cnt.shape[0]
    run_len = (cnt + (ROW_ALIGN - 1)) // ROW_ALIGN * ROW_ALIGN
    run_loc = jnp.cumsum(run_len, axis=1) - run_len
    per_expert = jnp.sum(run_len, axis=0)
    region = (per_expert + (FFN_ROWS - 1)) // FFN_ROWS * FFN_ROWS
    region_end = jnp.cumsum(region)
    region_start = region_end - region
    run_dst = region_start[None, :] + jnp.cumsum(run_len, axis=0) - run_len
    nblk = total_rows // FFN_ROWS
    blk_row = jnp.arange(nblk, dtype=I32) * FFN_ROWS
    used = region_end[-1] // FFN_ROWS
    last = jnp.maximum(used - 1, 0)
    bexp = jnp.minimum(jnp.searchsorted(region_end, blk_row, side="right"), N_EXPERTS - 1).astype(I32)
    nval = jnp.clip(per_expert[bexp] - (blk_row - region_start[bexp]), 0, FFN_ROWS)
    nval = jnp.where(blk_row < region_end[-1], nval, 0).astype(I32)
    bexp = jnp.where(blk_row < region_end[-1], bexp, bexp[last])
    flat = lambda a: a.reshape(n_tiles * N_EXPERTS).astype(I32)
    return flat(run_len), flat(run_loc), flat(run_dst), bexp, nval, last.reshape(1).astype(I32)


def _chunked_t(lf, chunk):
    b, length, h = lf.shape
    return lf.reshape(b, length // chunk, chunk, h).transpose(0, 1, 3, 2)


def kernel(x_prompt, x_sample, p_prompt, p_sample, cache_k, cache_v, cache_logf, state_conv, g_mix, w_in, b_f, w_conv, w_pa, w_pb, w_o, g_ffn, w_router, b_router, w_gu, b_gu, w_dn, b_dn, g_ple, w_ple_gate, w_ple_proj, g_final):
    depth = g_mix.shape[0]
    assert depth == 1
    nb, seq, d = x_prompt.shape
    db, dseq, _ = x_sample.shape
    past = cache_k.shape[2]
    n_p, n_s = nb * seq, db * dseq

    w = w_in[0].astype(BF16)
    o_f = 3 * FOX_DIM
    o_c = o_f + HEADS
    o_g = o_c + 3 * CONV_DIM
    wf = jnp.pad(w[:, o_f:o_c], ((0, 0), (0, LANES - HEADS)))
    bf = jnp.pad(b_f[0], (0, LANES - HEADS)).reshape(1, LANES)
    mixer_w = (g_mix[0].reshape(1, d), w[:, :o_f], wf, bf, w[:, o_c:o_g], w_conv[0], w[:, o_g:], w_pb[0].astype(BF16))

    (q_p, k_p, v_p, logf_p, sa_p, mb_p), tail_p = _in_proj(
        x_prompt.reshape(n_p, d), jnp.zeros((nb, CONV_WIDTH - 1, CONV_DIM), F32), mixer_w, seq_len=seq)
    c_p = _cumsum_chunks(_chunked_t(logf_p.reshape(nb, seq, HEADS), TOKEN_TILE))
    o_p = _attn_prompt(q_p, k_p, v_p, c_p, nb=nb, seq=seq)

    (q_s, k_s, v_s, logf_s, sa_s, mb_s), tail_s = _in_proj(x_sample.reshape(n_s, d), state_conv[0], mixer_w, seq_len=dseq)
    chunk = min(past, 2 * TOKEN_TILE)
    lf_all = jnp.concatenate([cache_logf[0].astype(F32), logf_s.reshape(db, dseq, HEADS),
                              jnp.zeros((db, chunk - dseq, HEADS), F32)], axis=1)
    c_s = _cumsum_chunks(_chunked_t(lf_all, chunk))
    o_s = _attn_sample(q_s, cache_k[0].reshape(db, past, FOX_DIM), cache_v[0].reshape(db, past, FOX_DIM),
                       k_s, v_s, c_s, nq=dseq)

    h1, xn2, pos_t, gate_t, cnt = _post_attn(
        x_prompt.reshape(n_p, d), x_sample.reshape(n_s, d), o_p, o_s, sa_p, sa_s, mb_p, mb_s,
        w_pa[0].astype(BF16), w_o[0].astype(BF16), g_ffn[0].reshape(1, d),
        w_router[0].T.astype(BF16), b_router[0].reshape(N_EXPERTS, 1))

    n = n_p + n_s
    n_tiles = n // DISPATCH_TILE
    max_rows = n * TOP_K + n_tiles * N_EXPERTS * (ROW_ALIGN - 1) + N_EXPERTS * (FFN_ROWS - 1)
    total_rows = -(-max_rows // FFN_ROWS) * FFN_ROWS
    run_len, run_loc, run_dst, bexp, nval, last = _routing_tables(cnt[:, :, 0], total_rows)
    xs = _dispatch(run_len, run_loc, run_dst, xn2, pos_t, total_rows=total_rows)
    os_ = _ffn(bexp, nval, last, xs, w_gu[0].astype(BF16), b_gu[0].reshape(N_EXPERTS, 1, 2 * D_FF),
               w_dn[0].astype(BF16), b_dn[0].reshape(N_EXPERTS, 1, d))

    y_p, y_s = _combine(run_len, run_loc, run_dst, os_, pos_t.T, gate_t.T, h1,
                        p_prompt[0].reshape(n_p, -1), p_sample[0].reshape(n_s, -1),
                        g_ple[0].reshape(1, d), w_ple_gate[0].astype(BF16), w_ple_proj[0].astype(BF16),
                        g_final.reshape(1, d))

    return (y_p.reshape(nb, seq, d), y_s.reshape(db, dseq, d),
            k_p.reshape(1, nb, seq, HEADS, HEAD_DIM), v_p.reshape(1, nb, seq, HEADS, HEAD_DIM),
            logf_p.reshape(1, nb, seq, HEADS), tail_p[None],
            k_s.reshape(1, db, dseq, HEADS, HEAD_DIM), v_s.reshape(1, db, dseq, HEADS, HEAD_DIM),
            logf_s.reshape(1, db, dseq, HEADS), tail_s[None])
```

```python
import functools

import jax
import jax.numpy as jnp
from jax import lax
from jax.experimental import pallas as pl
from jax.experimental.pallas import tpu as pltpu

F32, BF16, I32 = jnp.float32, jnp.bfloat16, jnp.int32

HEADS = 8
HEAD_DIM = 64
FOX_DIM = HEADS * HEAD_DIM
CONV_DIM = 512
CONV_WIDTH = 3
N_EXPERTS = 32
TOP_K = 4
D_FF = 1024
SWIGLU_ALPHA = 1.702
SWIGLU_LIMIT = 7.0
RMS_EPS = 1e-6

LANES = 128
SUBLANES = 8
TOKEN_TILE = 512
DISPATCH_TILE = 256
ROW_ALIGN = SUBLANES
LOCAL_ROWS = DISPATCH_TILE * TOP_K + N_EXPERTS * ROW_ALIGN
FFN_ROWS = 512
RUN_CHUNKS = (256, 128, 64, 32, 16, 8)
VMEM_LIMIT_BYTES = 56 * 1024 * 1024


def _cparams(*sem):
    return pltpu.CompilerParams(dimension_semantics=sem, vmem_limit_bytes=VMEM_LIMIT_BYTES)


def _rms(x, g):
    return x * lax.rsqrt(jnp.mean(x * x, axis=-1, keepdims=True) + RMS_EPS) * g


def _sigmoid(x):
    return 1.0 / (1.0 + jnp.exp(-x))


def _log_sigmoid(x):
    return jnp.minimum(x, 0.0) - jnp.log1p(jnp.exp(-jnp.abs(x)))


def _dot(a, b):
    return jnp.dot(a, b, preferred_element_type=F32)


def _dot_nt(a, b):
    return lax.dot_general(a, b, (((1,), (1,)), ((), ())), preferred_element_type=F32)


def _mixer_inputs(x_ref, g_ref, wqkv_ref, wf_ref, bf_ref, wc_ref, wgl_ref,
                  q_ref, k_ref, v_ref, ko_ref, vo_ref, logf_ref, lfp_ref, sa_ref):
    xn = _rms(x_ref[...], g_ref[...]).astype(BF16)
    qkv = _dot(xn, wqkv_ref[...])
    q_ref[...] = (qkv[:, :FOX_DIM] * (HEAD_DIM ** -0.5)).astype(BF16)
    k = qkv[:, FOX_DIM:2 * FOX_DIM]
    v = qkv[:, 2 * FOX_DIM:]
    k_ref[...] = k.astype(BF16)
    v_ref[...] = v.astype(BF16)
    for h in range(HEADS):
        ko_ref[:, h, :] = k[:, h * HEAD_DIM:(h + 1) * HEAD_DIM]
        vo_ref[:, h, :] = v[:, h * HEAD_DIM:(h + 1) * HEAD_DIM]
    logf = _log_sigmoid(_dot(xn, wf_ref[...]) + bf_ref[...])
    logf_ref[...] = logf[:, :HEADS]
    lane = lax.broadcasted_iota(I32, logf.shape, 1)
    lfp_ref[...] = jnp.where(lane < HEADS, logf, 0.0)
    c3 = _dot(xn, wc_ref[...])
    gate_b = c3[:, :CONV_DIM]
    z = c3[:, CONV_DIM:2 * CONV_DIM] * c3[:, 2 * CONV_DIM:]
    gl = _dot(xn, wgl_ref[...])
    d = gl.shape[1] // 2
    sa_ref[...] = _sigmoid(gl[:, :d]).astype(BF16)
    return gate_b, z, _sigmoid(gl[:, d:])


def _conv_out(gate_b, z, z1, z2, sig_b, wconv_ref, wpb_ref, mb_ref):
    zc = wconv_ref[0:1, :] * z2 + wconv_ref[1:2, :] * z1 + wconv_ref[2:3, :] * z
    yb = _dot((gate_b * zc).astype(BF16), wpb_ref[...])
    mb_ref[...] = (sig_b * yb).astype(BF16)


N_MIXER_W = 8
N_MIXER_OUT = 9


def _mixer_front(x_ref, w_refs, out_refs):
    g_ref, wqkv_ref, wf_ref, bf_ref, wc_ref, wconv_ref, wgl_ref, wpb_ref = w_refs
    gate_b, z, sig_b = _mixer_inputs(x_ref, g_ref, wqkv_ref, wf_ref, bf_ref, wc_ref, wgl_ref, *out_refs[:-1])
    return gate_b, z, sig_b, wconv_ref, wpb_ref, out_refs[-1]


def _in_proj_seq_kernel(x_ref, prev_ref, *refs, tm):
    w_refs, out_refs = refs[:N_MIXER_W], refs[N_MIXER_W:N_MIXER_W + N_MIXER_OUT]
    tail_ref, zbuf_ref = refs[N_MIXER_W + N_MIXER_OUT:]
    gate_b, z, sig_b, wconv_ref, wpb_ref, mb_ref = _mixer_front(x_ref, w_refs, out_refs)
    zbuf_ref[pl.ds(SUBLANES, tm), :] = z

    @pl.when(pl.program_id(1) == 0)
    def _():
        zbuf_ref[pl.ds(SUBLANES - 2, 2), :] = prev_ref[0]

    z1 = zbuf_ref[pl.ds(SUBLANES - 1, tm), :]
    z2 = zbuf_ref[pl.ds(SUBLANES - 2, tm), :]
    _conv_out(gate_b, z, z1, z2, sig_b, wconv_ref, wpb_ref, mb_ref)
    tail = zbuf_ref[pl.ds(tm + SUBLANES - 2, 2), :]
    zbuf_ref[pl.ds(SUBLANES - 2, 2), :] = tail
    tail_ref[0] = tail


def _in_proj_multi_kernel(x_ref, ov1_ref, ov2_ref, *refs, tm, seq_len):
    w_refs, out_refs = refs[:N_MIXER_W], refs[N_MIXER_W:N_MIXER_W + N_MIXER_OUT]
    z_ref, zbuf_ref = refs[N_MIXER_W + N_MIXER_OUT:]
    gate_b, z, sig_b, wconv_ref, wpb_ref, mb_ref = _mixer_front(x_ref, w_refs, out_refs)
    z_ref[...] = z
    zbuf_ref[pl.ds(0, SUBLANES), :] = jnp.zeros((SUBLANES, CONV_DIM), F32)
    zbuf_ref[pl.ds(SUBLANES, tm), :] = z
    t = lax.broadcasted_iota(I32, (tm, CONV_DIM), 0) & (seq_len - 1)
    z1 = jnp.where(t == 0, ov1_ref[...], zbuf_ref[pl.ds(SUBLANES - 1, tm), :])
    z2 = jnp.where(t < 2, ov2_ref[...], zbuf_ref[pl.ds(SUBLANES - 2, tm), :])
    _conv_out(gate_b, z, z1, z2, sig_b, wconv_ref, wpb_ref, mb_ref)


def _full(shape):
    n = len(shape)
    return pl.BlockSpec(shape, lambda *_: (0,) * n)


def _in_proj(x, conv_prev, weights, *, seq_len):
    g, wqkv, wf, bf, wc, wconv, wgl, wpb = weights
    n, d = x.shape
    w_specs = [_full(w.shape) for w in (g, wqkv, wf, bf, wc, wconv, wgl, wpb)]
    shapes = (((FOX_DIM,), BF16), ((FOX_DIM,), BF16), ((FOX_DIM,), BF16), ((HEADS, HEAD_DIM), F32), ((HEADS, HEAD_DIM), F32),
              ((HEADS,), F32), ((LANES,), F32), ((d,), BF16), ((d,), BF16))
    assert len(weights) == N_MIXER_W and len(shapes) == N_MIXER_OUT
    out_shape = [jax.ShapeDtypeStruct((n,) + s, t) for s, t in shapes]
    n_common = N_MIXER_OUT
    tm = TOKEN_TILE

    def out_specs(tile_index):
        return [pl.BlockSpec((tm,) + s, lambda *a, k=len(s): (tile_index(*a),) + (0,) * k) for s, _ in shapes]

    if seq_len % tm == 0:
        nb, nj = n // seq_len, seq_len // tm
        row = lambda b, j: (b * nj + j, 0)
        outs = pl.pallas_call(
            functools.partial(_in_proj_seq_kernel, tm=tm),
            out_shape=out_shape + [jax.ShapeDtypeStruct((nb, CONV_WIDTH - 1, CONV_DIM), F32)],
            grid=(nb, nj),
            in_specs=[pl.BlockSpec((tm, d), row), pl.BlockSpec((1, CONV_WIDTH - 1, CONV_DIM), lambda b, j: (b, 0, 0))] + w_specs,
            out_specs=out_specs(lambda b, j: b * nj + j)
            + [pl.BlockSpec((1, CONV_WIDTH - 1, CONV_DIM), lambda b, j: (b, 0, 0))],
            scratch_shapes=[pltpu.VMEM((tm + SUBLANES, CONV_DIM), F32)],
            compiler_params=_cparams("arbitrary", "arbitrary"),
            name="in_proj_seq",
        )(x, conv_prev, g, wqkv, wf, bf, wc, wconv, wgl, wpb)
        return outs[:n_common], outs[n_common]
    assert seq_len & (seq_len - 1) == 0 and seq_len >= CONV_WIDTH - 1
    assert n % tm == 0 and tm % seq_len == 0
    first = jnp.zeros((n // seq_len, seq_len, CONV_DIM), F32)
    ov1 = first.at[:, 0].set(conv_prev[:, 1]).reshape(n, CONV_DIM)
    ov2 = first.at[:, 0].set(conv_prev[:, 0]).at[:, 1].set(conv_prev[:, 1]).reshape(n, CONV_DIM)
    row = lambda i: (i, 0)
    outs = pl.pallas_call(
        functools.partial(_in_proj_multi_kernel, tm=tm, seq_len=seq_len),
        out_shape=out_shape + [jax.ShapeDtypeStruct((n, CONV_DIM), F32)],
        grid=(n // tm,),
        in_specs=[pl.BlockSpec((tm, d), row), pl.BlockSpec((tm, CONV_DIM), row), pl.BlockSpec((tm, CONV_DIM), row)] + w_specs,
        out_specs=out_specs(lambda i: i) + [pl.BlockSpec((tm, CONV_DIM), row)],
        scratch_shapes=[pltpu.VMEM((tm + SUBLANES, CONV_DIM), F32)],
        compiler_params=_cparams("arbitrary"),
        name="in_proj_multi",
    )(x, ov1, ov2, g, wqkv, wf, bf, wc, wconv, wgl, wpb)
    tail = outs[n_common].reshape(n // seq_len, seq_len, CONV_DIM)[:, seq_len - (CONV_WIDTH - 1):]
    return outs[:n_common], tail


def _cumsum_kernel(lf_ref, c_ref, *, chunk, nchunk):
    r = lax.broadcasted_iota(I32, (chunk, chunk), 0)
    c = lax.broadcasted_iota(I32, (chunk, chunk), 1)
    upper = jnp.where(r <= c, 1.0, 0.0).astype(BF16)
    carry = jnp.zeros((HEADS, 1), F32)
    for n in range(nchunk):
        a = lf_ref[0, n]
        hi = a.astype(BF16)
        r1 = a - hi.astype(F32)
        mid = r1.astype(BF16)
        lo = (r1 - mid.astype(F32)).astype(BF16)
        cs = _dot(hi, upper) + _dot(mid, upper) + _dot(lo, upper) + carry
        c_ref[0, n] = cs
        carry = cs[:, chunk - 1:chunk]


def _cumsum_chunks(lf):
    b, nchunk, _, chunk = lf.shape
    spec = pl.BlockSpec((1, nchunk, HEADS, chunk), lambda i: (i, 0, 0, 0))
    return pl.pallas_call(
        functools.partial(_cumsum_kernel, chunk=chunk, nchunk=nchunk),
        out_shape=jax.ShapeDtypeStruct(lf.shape, F32),
        grid=(b,), in_specs=[spec], out_specs=spec,
        compiler_params=_cparams("arbitrary"),
        name="logf_cumsum",
    )(lf)


C_TERMS = 3


def _cumsum_rows_kernel(lf_ref, cs_ref, *, chunk, nchunk):
    r = lax.broadcasted_iota(I32, (chunk, chunk), 0)
    c = lax.broadcasted_iota(I32, (chunk, chunk), 1)
    lower = jnp.where(c <= r, 1.0, 0.0).astype(BF16)
    carry = jnp.zeros((1, LANES), F32)
    for n in range(nchunk):
        rest = lf_ref[pl.ds(n * chunk, chunk), :]
        cs = carry
        for _ in range(C_TERMS):
            term = rest.astype(BF16)
            cs = cs + _dot(lower, term)
            rest = rest - term.astype(F32)
        carry = cs[chunk - 1:chunk, :]
        packed = jnp.zeros((chunk, LANES), F32)
        rest = cs
        for j in range(C_TERMS):
            term = rest.astype(BF16).astype(F32)
            packed = packed + (term if j == 0 else pltpu.roll(term, j * HEADS, 1))
            rest = rest - term
        cs_ref[pl.ds(n * chunk, chunk), :] = packed.astype(BF16)


def _cumsum_rows(lfp, *, seq):
    n = lfp.shape[0]
    spec = pl.BlockSpec((seq, LANES), lambda b: (b, 0))
    return pl.pallas_call(
        functools.partial(_cumsum_rows_kernel, chunk=TOKEN_TILE, nchunk=seq // TOKEN_TILE),
        out_shape=jax.ShapeDtypeStruct((n, LANES), BF16),
        grid=(n // seq,), in_specs=[spec], out_specs=spec,
        compiler_params=_cparams("arbitrary"),
        name="logf_cumsum_rows",
    )(lfp)


def _attn_prompt_kernel(q_ref, k_ref, v_ref, cs_ref, o_ref, ka_ref, vt_ref, m_ref, acc_ref, *, t, nq):
    hp, i = pl.program_id(1), pl.program_id(2)
    rr = lax.broadcasted_iota(I32, (LANES, LANES), 0)
    cc = lax.broadcasted_iota(I32, (LANES, LANES), 1)
    sel = [jnp.where(jnp.where(cc < HEAD_DIM, rr - cc, -1) == HEAD_DIM * h, 1.0, 0.0).astype(BF16) for h in range(2)]

    @pl.when(i == 0)
    def _():
        kb = k_ref[...]
        cs = cs_ref[...]
        vt = v_ref[...].astype(F32).T
        row = lax.broadcasted_iota(I32, (SUBLANES, t), 0)
        ones = jnp.where(row == 0, 1.0, 0.0)
        pad = jnp.zeros((LANES - HEAD_DIM - SUBLANES, t), F32)
        for h in range(2):
            head = 2 * hp + h
            src = jnp.where(cc >= HEAD_DIM, (cc - HEAD_DIM) * HEADS + head, -1)
            src = jnp.where(cc < HEAD_DIM + C_TERMS, src, -1)
            place = jnp.where(rr == src, -1.0, 0.0).astype(BF16)
            ka_ref[h] = (_dot(kb, sel[h]) + _dot(cs, place)).astype(BF16)
            for n in range(nq):
                vh = vt[h * HEAD_DIM:(h + 1) * HEAD_DIM, n * t:(n + 1) * t]
                vt_ref[h, n] = jnp.concatenate([vh, ones, pad], axis=0).astype(BF16)

    lane = lax.broadcasted_iota(I32, (1, LANES), 1)
    one_lanes = jnp.where(jnp.logical_and(lane >= HEAD_DIM, lane < HEAD_DIM + C_TERMS), 1.0, 0.0)
    q = q_ref[...]
    qa = [(_dot(q, sel[h]) + one_lanes).astype(BF16) for h in range(2)]
    m_ref[...] = jnp.full(m_ref.shape, -jnp.inf, F32)
    acc_ref[...] = jnp.zeros(acc_ref.shape, F32)

    def step(j, masked):
        start = pl.multiple_of(j * t, t)
        for h in range(2):
            st = _dot_nt(ka_ref[h, pl.ds(start, t), :], qa[h])
            if masked:
                ki = lax.broadcasted_iota(I32, (t, t), 0)
                qi = lax.broadcasted_iota(I32, (t, t), 1)
                st = jnp.where(ki <= qi, st, -jnp.inf)
            m_prev = m_ref[h]
            m_new = jnp.maximum(m_prev, jnp.max(st, axis=0, keepdims=True))
            p = jnp.exp(st - m_new).astype(BF16)
            m_ref[h] = m_new
            acc_ref[h] = acc_ref[h] * jnp.exp(m_prev - m_new) + _dot(vt_ref[h, j], p)

    def body(j, carry):
        step(j, False)
        return carry

    lax.fori_loop(0, i, body, 0)
    step(i, True)
    halves = []
    for h in range(2):
        acc = acc_ref[h]
        halves.append(acc[:HEAD_DIM] * (1.0 / acc[HEAD_DIM:HEAD_DIM + 1]))
    o_ref[...] = jnp.concatenate(halves, axis=0).T.astype(BF16)


def _attn_prompt(q, k, v, cs, *, nb, seq):
    t = TOKEN_TILE
    nq = seq // t
    return pl.pallas_call(
        functools.partial(_attn_prompt_kernel, t=t, nq=nq),
        out_shape=jax.ShapeDtypeStruct(q.shape, BF16),
        grid=(nb, HEADS // 2, nq),
        in_specs=[pl.BlockSpec((t, LANES), lambda b, hp, i: (b * nq + i, hp)),
                  pl.BlockSpec((seq, LANES), lambda b, hp, i: (b, hp)),
                  pl.BlockSpec((seq, LANES), lambda b, hp, i: (b, hp)),
                  pl.BlockSpec((seq, LANES), lambda b, hp, i: (b, 0))],
        out_specs=pl.BlockSpec((t, LANES), lambda b, hp, i: (b * nq + i, hp)),
        scratch_shapes=[pltpu.VMEM((2, seq, LANES), BF16), pltpu.VMEM((2, nq, LANES, t), BF16),
                        pltpu.VMEM((2, 1, t), F32), pltpu.VMEM((2, LANES, t), F32)],
        compiler_params=_cparams("arbitrary", "arbitrary", "arbitrary"),
        name="attn_prompt",
    )(q, k, v, cs)


def _attn_sample_kernel(q_ref, kc_ref, vc_ref, kn_ref, vn_ref, c_ref, o_ref,
                        m_ref, l_ref, acc_ref, kpad_ref, vpad_ref, *, nq, nchunk):
    j = pl.program_id(1)

    @pl.when(j == 0)
    def _():
        m_ref[...] = jnp.full(m_ref.shape, -jnp.inf, F32)
        l_ref[...] = jnp.zeros(l_ref.shape, F32)
        acc_ref[...] = jnp.zeros(acc_ref.shape, F32)

    def update(h, kh, vh, crow, visible):
        s = _dot_nt(q_ref[:, h * HEAD_DIM:(h + 1) * HEAD_DIM], kh) - crow
        if visible is not None:
            s = jnp.where(visible, s, -jnp.inf)
        m_prev = m_ref[h]
        m_new = jnp.maximum(m_prev, jnp.max(s, axis=-1, keepdims=True))
        a = jnp.exp(m_prev - m_new)
        p = jnp.exp(s - m_new)
        l_ref[h] = a * l_ref[h] + jnp.sum(p, axis=-1, keepdims=True)
        m_ref[h] = m_new
        acc_ref[h] = acc_ref[h] * a + _dot(p.astype(BF16), vh)

    @pl.when(j < nchunk)
    def _():
        for h in range(HEADS):
            update(h, kc_ref[0, :, h, :].astype(BF16), vc_ref[0, :, h, :].astype(BF16), c_ref[0, j, h:h + 1, :], None)

    @pl.when(j == nchunk)
    def _():
        kpad_ref[...] = jnp.zeros(kpad_ref.shape, BF16)
        vpad_ref[...] = jnp.zeros(vpad_ref.shape, BF16)
        kpad_ref[pl.ds(0, nq), :] = kn_ref[...]
        vpad_ref[pl.ds(0, nq), :] = vn_ref[...]
        ki = lax.broadcasted_iota(I32, (nq, LANES), 1)
        qi = lax.broadcasted_iota(I32, (nq, LANES), 0)
        outs = []
        for h in range(HEADS):
            sl = slice(h * HEAD_DIM, (h + 1) * HEAD_DIM)
            update(h, kpad_ref[:, sl], vpad_ref[:, sl], c_ref[0, nchunk, h:h + 1, :LANES], ki <= qi)
            outs.append(acc_ref[h] * (1.0 / l_ref[h]))
        o_ref[...] = jnp.concatenate(outs, axis=1).astype(BF16)


def _attn_sample(q, k_cache, v_cache, k_new, v_new, c, *, nq):
    nb, past = k_cache.shape[:2]
    chunk = c.shape[-1]
    nchunk = past // chunk
    assert nq <= LANES and past % chunk == 0 and c.shape[1] == nchunk + 1
    cache_spec = pl.BlockSpec((1, chunk, HEADS, HEAD_DIM), lambda b, j: (b, jnp.minimum(j, nchunk - 1), 0, 0))
    new_spec = pl.BlockSpec((nq, FOX_DIM), lambda b, j: (b, 0))
    return pl.pallas_call(
        functools.partial(_attn_sample_kernel, nq=nq, nchunk=nchunk),
        out_shape=jax.ShapeDtypeStruct(q.shape, BF16),
        grid=(nb, nchunk + 1),
        in_specs=[new_spec, cache_spec, cache_spec, new_spec, new_spec,
                  pl.BlockSpec((1, nchunk + 1, HEADS, chunk), lambda b, j: (b, 0, 0, 0))],
        out_specs=new_spec,
        scratch_shapes=[pltpu.VMEM((HEADS, nq, 1), F32), pltpu.VMEM((HEADS, nq, 1), F32),
                        pltpu.VMEM((HEADS, nq, HEAD_DIM), F32), pltpu.VMEM((LANES, FOX_DIM), BF16),
                        pltpu.VMEM((LANES, FOX_DIM), BF16)],
        compiler_params=_cparams("arbitrary", "arbitrary"),
        name="attn_sample",
    )(q, k_cache, v_cache, k_new, v_new, c)


def _pick(is_first, a_ref, b_ref):
    return jnp.where(is_first, a_ref[...], b_ref[...])


def _post_attn_kernel(xp_ref, xs_ref, op_ref, os_ref, sap_ref, sas_ref, mbp_ref, mbs_ref,
                      wpa_ref, wo_ref, g_ref, wrt_ref, br_ref,
                      h1_ref, xn_ref, pos_ref, gate_ref, cnt_ref, *, n_first, tm):
    is_p = pl.program_id(0) < n_first
    ya = _dot(_pick(is_p, op_ref, os_ref), wpa_ref[...])
    merged = _pick(is_p, sap_ref, sas_ref).astype(F32) * ya + _pick(is_p, mbp_ref, mbs_ref).astype(F32)
    h1 = _pick(is_p, xp_ref, xs_ref) + _dot(merged.astype(BF16), wo_ref[...])
    h1_ref[...] = h1
    xn = _rms(h1, g_ref[...]).astype(BF16)
    xn_ref[...] = xn

    lt = _dot_nt(wrt_ref[...], xn) + br_ref[...]
    eio = lax.broadcasted_iota(I32, (N_EXPERTS, tm), 0).astype(F32)
    vals, hots = [], []
    for _ in range(TOP_K):
        m = jnp.max(lt, axis=0, keepdims=True)
        idx = jnp.min(jnp.where(lt == m, eio, float(N_EXPERTS)), axis=0, keepdims=True)
        hot = eio == idx
        vals.append(m)
        hots.append(hot)
        lt = jnp.where(hot, -jnp.inf, lt)
    ex = [jnp.exp(v - vals[0]) for v in vals]
    den = ex[0] + ex[1] + ex[2] + ex[3]
    gate_ref[...] = jnp.concatenate([e / den for e in ex], axis=0)

    chosen = jnp.zeros((N_EXPERTS, tm), F32)
    for hot in hots:
        chosen = jnp.where(hot, 1.0, chosen)
    td = DISPATCH_TILE
    r = lax.broadcasted_iota(I32, (td, td), 0)
    c = lax.broadcasted_iota(I32, (td, td), 1)
    before = jnp.where(r < c, 1.0, 0.0).astype(BF16)
    er = lax.broadcasted_iota(I32, (N_EXPERTS, N_EXPERTS), 0)
    ec = lax.broadcasted_iota(I32, (N_EXPERTS, N_EXPERTS), 1)
    lower = jnp.where(ec < er, 1.0, 0.0).astype(BF16)
    for sub in range(tm // td):
        sl = slice(sub * td, (sub + 1) * td)
        ch = chosen[:, sl]
        rank = _dot(ch.astype(BF16), before)
        cnt = rank[:, td - 1:td] + ch[:, td - 1:td]
        units = jnp.floor((cnt + (ROW_ALIGN - 1)) * (1.0 / ROW_ALIGN))
        start = ROW_ALIGN * _dot(lower, jnp.broadcast_to(units, (N_EXPERTS, td)).astype(BF16))
        base = start + rank
        pos = [jnp.sum(jnp.where(hot[:, sl], base, 0.0), axis=0, keepdims=True) for hot in hots]
        pos_ref[:, sl] = jnp.concatenate(pos, axis=0).astype(I32)
        cnt_ref[sub] = jnp.broadcast_to(cnt, (N_EXPERTS, LANES)).astype(I32)


def _post_attn(xp, xs, op, os_, sap, sas, mbp, mbs, wpa, wo, g, wrt, br):
    n_p, d = xp.shape
    n_s = xs.shape[0]
    tm = TOKEN_TILE
    assert n_p % tm == 0 and n_s % tm == 0
    n_first, n_tiles = n_p // tm, (n_p + n_s) // tm
    n = n_p + n_s
    sub = tm // DISPATCH_TILE
    first = lambda i: (jnp.minimum(i, n_first - 1), 0)
    second = lambda i: (jnp.maximum(i - n_first, 0), 0)
    row = lambda i: (i, 0)
    col = lambda i: (0, i)

    def pair(width):
        return [pl.BlockSpec((tm, width), first), pl.BlockSpec((tm, width), second)]

    return pl.pallas_call(
        functools.partial(_post_attn_kernel, n_first=n_first, tm=tm),
        out_shape=[jax.ShapeDtypeStruct((n, d), F32), jax.ShapeDtypeStruct((n, d), BF16),
                   jax.ShapeDtypeStruct((TOP_K, n), I32), jax.ShapeDtypeStruct((TOP_K, n), F32),
                   jax.ShapeDtypeStruct((n // DISPATCH_TILE, N_EXPERTS, LANES), I32)],
        grid=(n_tiles,),
        in_specs=pair(d) + pair(FOX_DIM) + pair(d) + pair(d) + [_full(w.shape) for w in (wpa, wo, g, wrt, br)],
        out_specs=[pl.BlockSpec((tm, d), row), pl.BlockSpec((tm, d), row),
                   pl.BlockSpec((TOP_K, tm), col), pl.BlockSpec((TOP_K, tm), col),
                   pl.BlockSpec((sub, N_EXPERTS, LANES), lambda i: (i, 0, 0))],
        compiler_params=_cparams("arbitrary"),
        name="post_attn_router",
    )(xp, xs, op, os_, sap, sas, mbp, mbs, wpa, wo, g, wrt, br)


def _for_each_run(tile, len_ref, loc_ref, dst_ref, fn):
    def body(e, carry):
        idx = tile * N_EXPERTS + e
        n, loc, dst = len_ref[idx], loc_ref[idx], dst_ref[idx]
        for c in RUN_CHUNKS:
            off = n & ~(2 * c - 1)

            @pl.when((n & c) != 0)
            def _():
                fn(pl.multiple_of(loc + off, ROW_ALIGN), pl.multiple_of(dst + off, ROW_ALIGN), c)
        return carry

    lax.fori_loop(0, N_EXPERTS, body, 0)


def _dispatch_kernel(len_ref, loc_ref, dst_ref, xn_ref, pos_ref, xs_hbm, xloc_ref, sem_ref, *, n_tiles):
    i = pl.program_id(0)
    slot = lax.rem(i, 2)

    def copy(s, loc, dst, rows):
        return pltpu.make_async_copy(xloc_ref.at[s, pl.ds(loc, rows), :], xs_hbm.at[pl.ds(dst, rows), :], sem_ref.at[s])

    def start_tile(tile, s):
        _for_each_run(tile, len_ref, loc_ref, dst_ref, lambda loc, dst, rows: copy(s, loc, dst, rows).start())

    def wait_tile(tile, s):
        _for_each_run(tile, len_ref, loc_ref, dst_ref, lambda loc, dst, rows: copy(s, loc, dst, rows).wait())

    pos = pos_ref[...]
    aio = lax.broadcasted_iota(I32, (LOCAL_ROWS, DISPATCH_TILE), 0)
    hit = aio == pos[0:1, :]
    for k in range(1, TOP_K):
        hit = jnp.logical_or(hit, aio == pos[k:k + 1, :])
    sel = jnp.where(hit, 1.0, 0.0).astype(BF16)
    rows_sorted = _dot(sel, xn_ref[...])

    @pl.when(i >= 2)
    def _():
        wait_tile(i - 2, slot)

    xloc_ref[slot] = rows_sorted
    start_tile(i, slot)

    @pl.when(i == n_tiles - 1)
    def _():
        if n_tiles >= 2:
            wait_tile(i - 1, 1 - slot)
        wait_tile(i, slot)


def _dispatch(run_len, run_loc, run_dst, xn, pos, *, total_rows):
    n, d = xn.shape
    td = DISPATCH_TILE
    n_tiles = n // td
    return pl.pallas_call(
        functools.partial(_dispatch_kernel, n_tiles=n_tiles),
        out_shape=jax.ShapeDtypeStruct((total_rows, d), F32),
        grid_spec=pltpu.PrefetchScalarGridSpec(
            num_scalar_prefetch=3, grid=(n_tiles,),
            in_specs=[pl.BlockSpec((td, d), lambda i, *_: (i, 0)), pl.BlockSpec((TOP_K, td), lambda i, *_: (0, i))],
            out_specs=pl.BlockSpec(memory_space=pl.ANY),
            scratch_shapes=[pltpu.VMEM((2, LOCAL_ROWS, d), F32), pltpu.SemaphoreType.DMA((2,))]),
        compiler_params=_cparams("arbitrary"),
        name="moe_dispatch",
    )(run_len, run_loc, run_dst, xn, pos)


def _ffn_kernel(bexp_ref, nval_ref, last_ref, xs_ref, wgu_ref, bgu_ref, wdn_ref, bdn_ref, o_ref):
    nv = nval_ref[pl.program_id(0)]

    @pl.when(nv > 0)
    def _():
        rows = lax.broadcasted_iota(I32, xs_ref.shape, 0)
        x = jnp.where(rows < nv, xs_ref[...], 0.0).astype(BF16)
        gu = _dot(x, wgu_ref[0]) + bgu_ref[0]
        gt = jnp.minimum(gu[:, :D_FF], SWIGLU_LIMIT)
        up = jnp.clip(gu[:, D_FF:], -SWIGLU_LIMIT, SWIGLU_LIMIT)
        act = (up + 1.0) * gt * _sigmoid(SWIGLU_ALPHA * gt)
        o_ref[...] = _dot(act.astype(BF16), wdn_ref[0]) + bdn_ref[0]


def _ffn(bexp, nval, last, xs, wgu, bgu, wdn, bdn):
    rows, d = xs.shape
    nblk = rows // FFN_ROWS
    blk = lambda i, bexp, nval, last: (jnp.minimum(i, last[0]), 0)
    exp3 = lambda i, bexp, nval, last: (bexp[i], 0, 0)
    return pl.pallas_call(
        _ffn_kernel,
        out_shape=jax.ShapeDtypeStruct((rows, d), F32),
        grid_spec=pltpu.PrefetchScalarGridSpec(
            num_scalar_prefetch=3, grid=(nblk,),
            in_specs=[pl.BlockSpec((FFN_ROWS, d), blk),
                      pl.BlockSpec((1, d, 2 * D_FF), exp3), pl.BlockSpec((1, 1, 2 * D_FF), exp3),
                      pl.BlockSpec((1, D_FF, d), exp3), pl.BlockSpec((1, 1, d), exp3)],
            out_specs=pl.BlockSpec((FFN_ROWS, d), blk)),
        compiler_params=_cparams("arbitrary"),
        name="moe_ffn",
    )(bexp, nval, last, xs, wgu, bgu, wdn, bdn)


def _combine_kernel(len_ref, loc_ref, dst_ref, os_hbm, pos_ref, gate_ref, h1_ref, pp_ref, ps_ref,
                    gple_ref, wpg_ref, wpp_ref, gfin_ref, yp_ref, ys_ref, oloc_ref, sem_ref, *, n_tiles, n_first):
    i = pl.program_id(0)
    slot = lax.rem(i, 2)

    def copy(s, loc, dst, rows):
        return pltpu.make_async_copy(os_hbm.at[pl.ds(dst, rows), :], oloc_ref.at[s, pl.ds(loc, rows), :], sem_ref.at[s])

    def start_tile(tile, s):
        _for_each_run(tile, len_ref, loc_ref, dst_ref, lambda loc, dst, rows: copy(s, loc, dst, rows).start())

    def wait_tile(tile, s):
        _for_each_run(tile, len_ref, loc_ref, dst_ref, lambda loc, dst, rows: copy(s, loc, dst, rows).wait())

    @pl.when(i == 0)
    def _():
        oloc_ref[...] = jnp.zeros(oloc_ref.shape, F32)
        start_tile(0, 0)

    @pl.when(i + 1 < n_tiles)
    def _():
        start_tile(i + 1, 1 - slot)

    wait_tile(i, slot)

    pos, gate = pos_ref[...], gate_ref[...]
    lio = lax.broadcasted_iota(I32, (DISPATCH_TILE, LOCAL_ROWS), 1)
    w = jnp.zeros((DISPATCH_TILE, LOCAL_ROWS), F32)
    for k in range(TOP_K):
        w = jnp.where(lio == pos[:, k:k + 1], gate[:, k:k + 1], w)
    h2 = h1_ref[...] + _dot(w.astype(BF16), oloc_ref[slot].astype(BF16))

    is_p = i < n_first
    xn = _rms(h2, gple_ref[...]).astype(BF16)
    ple = _dot(_pick(is_p, pp_ref, ps_ref).astype(BF16), wpp_ref[...])
    h3 = h2 + _sigmoid(_dot(xn, wpg_ref[...])) * ple
    y = _rms(h3, gfin_ref[...])

    @pl.when(is_p)
    def _():
        yp_ref[...] = y

    @pl.when(jnp.logical_not(is_p))
    def _():
        ys_ref[...] = y


def _combine(run_len, run_loc, run_dst, os_, pos, gate, h1, pp, ps, gple, wpg, wpp, gfin):
    n, d = h1.shape
    td = DISPATCH_TILE
    n_p, n_s = pp.shape[0], ps.shape[0]
    assert n_p % td == 0 and n_s % td == 0
    n_tiles, n_first = n // td, n_p // td
    pw = pp.shape[1]
    first = lambda i, *_: (jnp.minimum(i, n_first - 1), 0)
    second = lambda i, *_: (jnp.maximum(i - n_first, 0), 0)
    row = lambda i, *_: (i, 0)
    full2 = lambda i, *_: (0, 0)
    return pl.pallas_call(
        functools.partial(_combine_kernel, n_tiles=n_tiles, n_first=n_first),
        out_shape=[jax.ShapeDtypeStruct((n_p, d), F32), jax.ShapeDtypeStruct((n_s, d), F32)],
        grid_spec=pltpu.PrefetchScalarGridSpec(
            num_scalar_prefetch=3, grid=(n_tiles,),
            in_specs=[pl.BlockSpec(memory_space=pl.ANY),
                      pl.BlockSpec((td, TOP_K), row), pl.BlockSpec((td, TOP_K), row), pl.BlockSpec((td, d), row),
                      pl.BlockSpec((td, pw), first), pl.BlockSpec((td, pw), second),
                      pl.BlockSpec(gple.shape, full2), pl.BlockSpec(wpg.shape, full2),
                      pl.BlockSpec(wpp.shape, full2), pl.BlockSpec(gfin.shape, full2)],
            out_specs=[pl.BlockSpec((td, d), first), pl.BlockSpec((td, d), second)],
            scratch_shapes=[pltpu.VMEM((2, LOCAL_ROWS, d), F32), pltpu.SemaphoreType.DMA((2,))]),
        compiler_params=_cparams("arbitrary"),
        name="moe_combine_ple",
    )(run_len, run_loc, run_dst, os_, pos, gate, h1, pp, ps, gple, wpg, wpp, gfin)


def _routing_tables(cnt, total_rows):
    n_tiles = cnt.shape[0]
    run_len = (cnt + (ROW_ALIGN - 1)) // ROW_ALIGN * ROW_ALIGN
    run_loc = jnp.cumsum(run_len, axis=1) - run_len
    per_expert = jnp.sum(run_len, axis=0)
    region = (per_expert + (FFN_ROWS - 1)) // FFN_ROWS * FFN_ROWS
    region_end = jnp.cumsum(region)
    region_start = region_end - region
    run_dst = region_start[None, :] + jnp.cumsum(run_len, axis=0) - run_len
    nblk = total_rows // FFN_ROWS
    blk_row = jnp.arange(nblk, dtype=I32) * FFN_ROWS
    used = region_end[-1] // FFN_ROWS
    last = jnp.maximum(used - 1, 0)
    bexp = jnp.minimum(jnp.sum(blk_row[:, None] >= region_end[None, :], axis=1), N_EXPERTS - 1).astype(I32)
    nval = jnp.clip(per_expert[bexp] - (blk_row - region_start[bexp]), 0, FFN_ROWS)
    nval = jnp.where(blk_row < region_end[-1], nval, 0).astype(I32)
    bexp = jnp.where(blk_row < region_end[-1], bexp, bexp[last])
    flat = lambda a: a.reshape(n_tiles * N_EXPERTS).astype(I32)
    return flat(run_len), flat(run_loc), flat(run_dst), bexp, nval, last.reshape(1).astype(I32)


def _chunked_t(lf, chunk):
    b, length, h = lf.shape
    return lf.reshape(b, length // chunk, chunk, h).transpose(0, 1, 3, 2)


def kernel(x_prompt, x_sample, p_prompt, p_sample, cache_k, cache_v, cache_logf, state_conv, g_mix, w_in, b_f, w_conv, w_pa, w_pb, w_o, g_ffn, w_router, b_router, w_gu, b_gu, w_dn, b_dn, g_ple, w_ple_gate, w_ple_proj, g_final):
    depth = g_mix.shape[0]
    assert depth == 1
    nb, seq, d = x_prompt.shape
    db, dseq, _ = x_sample.shape
    past = cache_k.shape[2]
    n_p, n_s = nb * seq, db * dseq

    w = w_in[0].astype(BF16)
    o_f = 3 * FOX_DIM
    o_c = o_f + HEADS
    o_g = o_c + 3 * CONV_DIM
    wf = jnp.pad(w[:, o_f:o_c], ((0, 0), (0, LANES - HEADS)))
    bf = jnp.pad(b_f[0], (0, LANES - HEADS)).reshape(1, LANES)
    mixer_w = (g_mix[0].reshape(1, d), w[:, :o_f], wf, bf, w[:, o_c:o_g], w_conv[0], w[:, o_g:], w_pb[0].astype(BF16))

    (q_p, k_p, v_p, ko_p, vo_p, logf_p, lfp_p, sa_p, mb_p), tail_p = _in_proj(
        x_prompt.reshape(n_p, d), jnp.zeros((nb, CONV_WIDTH - 1, CONV_DIM), F32), mixer_w, seq_len=seq)
    o_p = _attn_prompt(q_p, k_p, v_p, _cumsum_rows(lfp_p, seq=seq), nb=nb, seq=seq)

    (q_s, k_s, v_s, ko_s, vo_s, logf_s, _, sa_s, mb_s), tail_s = _in_proj(
        x_sample.reshape(n_s, d), state_conv[0], mixer_w, seq_len=dseq)
    chunk = min(past, 2 * TOKEN_TILE)
    lf_all = jnp.concatenate([cache_logf[0].astype(F32), logf_s.reshape(db, dseq, HEADS),
                              jnp.zeros((db, chunk - dseq, HEADS), F32)], axis=1)
    c_s = _cumsum_chunks(_chunked_t(lf_all, chunk))
    o_s = _attn_sample(q_s, cache_k[0], cache_v[0], k_s, v_s, c_s, nq=dseq)

    h1, xn2, pos_t, gate_t, cnt = _post_attn(
        x_prompt.reshape(n_p, d), x_sample.reshape(n_s, d), o_p, o_s, sa_p, sa_s, mb_p, mb_s,
        w_pa[0].astype(BF16), w_o[0].astype(BF16), g_ffn[0].reshape(1, d),
        w_router[0].T.astype(BF16), b_router[0].reshape(N_EXPERTS, 1))

    n = n_p + n_s
    n_tiles = n // DISPATCH_TILE
    max_rows = n * TOP_K + n_tiles * N_EXPERTS * (ROW_ALIGN - 1) + N_EXPERTS * (FFN_ROWS - 1)
    total_rows = -(-max_rows // FFN_ROWS) * FFN_ROWS
    run_len, run_loc, run_dst, bexp, nval, last = _routing_tables(cnt[:, :, 0], total_rows)
    xs = _dispatch(run_len, run_loc, run_dst, xn2, pos_t, total_rows=total_rows)
    os_ = _ffn(bexp, nval, last, xs, w_gu[0].astype(BF16), b_gu[0].reshape(N_EXPERTS, 1, 2 * D_FF),
               w_dn[0].astype(BF16), b_dn[0].reshape(N_EXPERTS, 1, d))

    y_p, y_s = _combine(run_len, run_loc, run_dst, os_, pos_t.T, gate_t.T, h1,
                        p_prompt[0].reshape(n_p, -1), p_sample[0].reshape(n_s, -1),
                        g_ple[0].reshape(1, d), w_ple_gate[0].astype(BF16), w_ple_proj[0].astype(BF16),
                        g_final.reshape(1, d))

    return (y_p.reshape(nb, seq, d), y_s.reshape(db, dseq, d),
            ko_p.reshape(1, nb, seq, HEADS, HEAD_DIM), vo_p.reshape(1, nb, seq, HEADS, HEAD_DIM),
            logf_p.reshape(1, nb, seq, HEADS), tail_p[None],
            ko_s.reshape(1, db, dseq, HEADS, HEAD_DIM), vo_s.reshape(1, db, dseq, HEADS, HEAD_DIM),
            logf_s.reshape(1, db, dseq, HEADS), tail_s[None])
```

```python
import functools

import jax
import jax.numpy as jnp
from jax import lax
from jax.experimental import pallas as pl
from jax.experimental.pallas import tpu as pltpu

F32, BF16, I32 = jnp.float32, jnp.bfloat16, jnp.int32

HEADS = 8
HEAD_DIM = 64
FOX_DIM = HEADS * HEAD_DIM
CONV_DIM = 512
CONV_WIDTH = 3
N_EXPERTS = 32
TOP_K = 4
D_FF = 1024
SWIGLU_ALPHA = 1.702
SWIGLU_LIMIT = 7.0
RMS_EPS = 1e-6

LANES = 128
SUBLANES = 8
TOKEN_TILE = 512
DISPATCH_TILE = 256
ROW_ALIGN = SUBLANES
LOCAL_ROWS = DISPATCH_TILE * TOP_K + N_EXPERTS * ROW_ALIGN
FFN_ROWS = 512
RUN_CHUNKS = (256, 128, 64, 32, 16, 8)
VMEM_LIMIT_BYTES = 56 * 1024 * 1024


def _cparams(*sem):
    return pltpu.CompilerParams(dimension_semantics=sem, vmem_limit_bytes=VMEM_LIMIT_BYTES)


def _rms(x, g):
    return x * lax.rsqrt(jnp.mean(x * x, axis=-1, keepdims=True) + RMS_EPS) * g


def _sigmoid(x):
    return 1.0 / (1.0 + jnp.exp(-x))


def _log_sigmoid(x):
    return jnp.minimum(x, 0.0) - jnp.log1p(jnp.exp(-jnp.abs(x)))


def _dot(a, b):
    return jnp.dot(a, b, preferred_element_type=F32)


def _dot_nt(a, b):
    return lax.dot_general(a, b, (((1,), (1,)), ((), ())), preferred_element_type=F32)


def _mixer_inputs(x_ref, g_ref, wqkv_ref, wf_ref, bf_ref, wc_ref, wgl_ref,
                  q_ref, k_ref, v_ref, ko_ref, vo_ref, logf_ref, lfp_ref, sa_ref, *, keys_minor):
    xn = _rms(x_ref[...], g_ref[...]).astype(BF16)
    qkv = _dot(xn, wqkv_ref[...])
    q_ref[...] = (qkv[:, :FOX_DIM] * (HEAD_DIM ** -0.5)).astype(BF16)
    k = qkv[:, FOX_DIM:2 * FOX_DIM]
    v = qkv[:, 2 * FOX_DIM:]
    k_ref[...] = k.astype(BF16)
    v_ref[...] = v.astype(BF16)
    if keys_minor:
        ko_ref[0] = k.T.reshape(HEADS, HEAD_DIM, k.shape[0])
        vo_ref[0] = v.T.reshape(HEADS, HEAD_DIM, v.shape[0])
    else:
        for h in range(HEADS):
            ko_ref[:, h, :] = k[:, h * HEAD_DIM:(h + 1) * HEAD_DIM]
            vo_ref[:, h, :] = v[:, h * HEAD_DIM:(h + 1) * HEAD_DIM]
    logf = _log_sigmoid(_dot(xn, wf_ref[...]) + bf_ref[...])
    logf_ref[...] = logf[:, :HEADS]
    lane = lax.broadcasted_iota(I32, logf.shape, 1)
    lfp_ref[...] = jnp.where(lane < HEADS, logf, 0.0)
    c3 = _dot(xn, wc_ref[...])
    gate_b = c3[:, :CONV_DIM]
    z = c3[:, CONV_DIM:2 * CONV_DIM] * c3[:, 2 * CONV_DIM:]
    gl = _dot(xn, wgl_ref[...])
    d = gl.shape[1] // 2
    sa_ref[...] = _sigmoid(gl[:, :d]).astype(BF16)
    return gate_b, z, _sigmoid(gl[:, d:])


def _conv_out(gate_b, z, z1, z2, sig_b, wconv_ref, wpb_ref, mb_ref):
    zc = wconv_ref[0:1, :] * z2 + wconv_ref[1:2, :] * z1 + wconv_ref[2:3, :] * z
    yb = _dot((gate_b * zc).astype(BF16), wpb_ref[...])
    mb_ref[...] = (sig_b * yb).astype(BF16)


N_MIXER_W = 8
N_MIXER_OUT = 9


def _mixer_front(x_ref, w_refs, out_refs, keys_minor):
    g_ref, wqkv_ref, wf_ref, bf_ref, wc_ref, wconv_ref, wgl_ref, wpb_ref = w_refs
    gate_b, z, sig_b = _mixer_inputs(x_ref, g_ref, wqkv_ref, wf_ref, bf_ref, wc_ref, wgl_ref, *out_refs[:-1],
                                     keys_minor=keys_minor)
    return gate_b, z, sig_b, wconv_ref, wpb_ref, out_refs[-1]


def _in_proj_seq_kernel(x_ref, prev_ref, *refs, tm):
    w_refs, out_refs = refs[:N_MIXER_W], refs[N_MIXER_W:N_MIXER_W + N_MIXER_OUT]
    tail_ref, zbuf_ref = refs[N_MIXER_W + N_MIXER_OUT:]
    gate_b, z, sig_b, wconv_ref, wpb_ref, mb_ref = _mixer_front(x_ref, w_refs, out_refs, True)
    zbuf_ref[pl.ds(SUBLANES, tm), :] = z

    @pl.when(pl.program_id(1) == 0)
    def _():
        zbuf_ref[pl.ds(SUBLANES - 2, 2), :] = prev_ref[0]

    z1 = zbuf_ref[pl.ds(SUBLANES - 1, tm), :]
    z2 = zbuf_ref[pl.ds(SUBLANES - 2, tm), :]
    _conv_out(gate_b, z, z1, z2, sig_b, wconv_ref, wpb_ref, mb_ref)
    tail = zbuf_ref[pl.ds(tm + SUBLANES - 2, 2), :]
    zbuf_ref[pl.ds(SUBLANES - 2, 2), :] = tail
    tail_ref[0] = tail


def _in_proj_multi_kernel(x_ref, ov1_ref, ov2_ref, *refs, tm, seq_len):
    w_refs, out_refs = refs[:N_MIXER_W], refs[N_MIXER_W:N_MIXER_W + N_MIXER_OUT]
    z_ref, zbuf_ref = refs[N_MIXER_W + N_MIXER_OUT:]
    gate_b, z, sig_b, wconv_ref, wpb_ref, mb_ref = _mixer_front(x_ref, w_refs, out_refs, False)
    z_ref[...] = z
    zbuf_ref[pl.ds(0, SUBLANES), :] = jnp.zeros((SUBLANES, CONV_DIM), F32)
    zbuf_ref[pl.ds(SUBLANES, tm), :] = z
    t = lax.broadcasted_iota(I32, (tm, CONV_DIM), 0) & (seq_len - 1)
    z1 = jnp.where(t == 0, ov1_ref[...], zbuf_ref[pl.ds(SUBLANES - 1, tm), :])
    z2 = jnp.where(t < 2, ov2_ref[...], zbuf_ref[pl.ds(SUBLANES - 2, tm), :])
    _conv_out(gate_b, z, z1, z2, sig_b, wconv_ref, wpb_ref, mb_ref)


def _full(shape):
    n = len(shape)
    return pl.BlockSpec(shape, lambda *_: (0,) * n)


def _in_proj(x, conv_prev, weights, *, seq_len):
    g, wqkv, wf, bf, wc, wconv, wgl, wpb = weights
    n, d = x.shape
    w_specs = [_full(w.shape) for w in (g, wqkv, wf, bf, wc, wconv, wgl, wpb)]
    shapes = (((FOX_DIM,), BF16), ((FOX_DIM,), BF16), ((FOX_DIM,), BF16), ((HEADS, HEAD_DIM), F32), ((HEADS, HEAD_DIM), F32),
              ((HEADS,), F32), ((LANES,), F32), ((d,), BF16), ((d,), BF16))
    assert len(weights) == N_MIXER_W and len(shapes) == N_MIXER_OUT
    out_shape = [jax.ShapeDtypeStruct((n,) + s, t) for s, t in shapes]
    n_common = N_MIXER_OUT
    tm = TOKEN_TILE

    def out_specs(tile_index):
        return [pl.BlockSpec((tm,) + s, lambda *a, k=len(s): (tile_index(*a),) + (0,) * k) for s, _ in shapes]

    if seq_len % tm == 0:
        nb, nj = n // seq_len, seq_len // tm
        row = lambda b, j: (b * nj + j, 0)
        specs = out_specs(lambda b, j: b * nj + j)
        for i in (3, 4):
            out_shape[i] = jax.ShapeDtypeStruct((nb, HEADS, HEAD_DIM, seq_len), F32)
            specs[i] = pl.BlockSpec((1, HEADS, HEAD_DIM, tm), lambda b, j: (b, 0, 0, j))
        outs = pl.pallas_call(
            functools.partial(_in_proj_seq_kernel, tm=tm),
            out_shape=out_shape + [jax.ShapeDtypeStruct((nb, CONV_WIDTH - 1, CONV_DIM), F32)],
            grid=(nb, nj),
            in_specs=[pl.BlockSpec((tm, d), row), pl.BlockSpec((1, CONV_WIDTH - 1, CONV_DIM), lambda b, j: (b, 0, 0))] + w_specs,
            out_specs=specs + [pl.BlockSpec((1, CONV_WIDTH - 1, CONV_DIM), lambda b, j: (b, 0, 0))],
            scratch_shapes=[pltpu.VMEM((tm + SUBLANES, CONV_DIM), F32)],
            compiler_params=_cparams("arbitrary", "arbitrary"),
            name="in_proj_seq",
        )(x, conv_prev, g, wqkv, wf, bf, wc, wconv, wgl, wpb)
        return outs[:n_common], outs[n_common]
    assert seq_len & (seq_len - 1) == 0 and seq_len >= CONV_WIDTH - 1
    assert n % tm == 0 and tm % seq_len == 0
    first = jnp.zeros((n // seq_len, seq_len, CONV_DIM), F32)
    ov1 = first.at[:, 0].set(conv_prev[:, 1]).reshape(n, CONV_DIM)
    ov2 = first.at[:, 0].set(conv_prev[:, 0]).at[:, 1].set(conv_prev[:, 1]).reshape(n, CONV_DIM)
    row = lambda i: (i, 0)
    outs = pl.pallas_call(
        functools.partial(_in_proj_multi_kernel, tm=tm, seq_len=seq_len),
        out_shape=out_shape + [jax.ShapeDtypeStruct((n, CONV_DIM), F32)],
        grid=(n // tm,),
        in_specs=[pl.BlockSpec((tm, d), row), pl.BlockSpec((tm, CONV_DIM), row), pl.BlockSpec((tm, CONV_DIM), row)] + w_specs,
        out_specs=out_specs(lambda i: i) + [pl.BlockSpec((tm, CONV_DIM), row)],
        scratch_shapes=[pltpu.VMEM((tm + SUBLANES, CONV_DIM), F32)],
        compiler_params=_cparams("arbitrary"),
        name="in_proj_multi",
    )(x, ov1, ov2, g, wqkv, wf, bf, wc, wconv, wgl, wpb)
    tail = outs[n_common].reshape(n // seq_len, seq_len, CONV_DIM)[:, seq_len - (CONV_WIDTH - 1):]
    return outs[:n_common], tail


def _cumsum_kernel(lf_ref, c_ref, *, chunk, nchunk):
    r = lax.broadcasted_iota(I32, (chunk, chunk), 0)
    c = lax.broadcasted_iota(I32, (chunk, chunk), 1)
    upper = jnp.where(r <= c, 1.0, 0.0).astype(BF16)
    carry = jnp.zeros((HEADS, 1), F32)
    for n in range(nchunk):
        a = lf_ref[0, n]
        hi = a.astype(BF16)
        r1 = a - hi.astype(F32)
        mid = r1.astype(BF16)
        lo = (r1 - mid.astype(F32)).astype(BF16)
        cs = _dot(hi, upper) + _dot(mid, upper) + _dot(lo, upper) + carry
        c_ref[0, n] = cs
        carry = cs[:, chunk - 1:chunk]


def _cumsum_chunks(lf):
    b, nchunk, _, chunk = lf.shape
    spec = pl.BlockSpec((1, nchunk, HEADS, chunk), lambda i: (i, 0, 0, 0))
    return pl.pallas_call(
        functools.partial(_cumsum_kernel, chunk=chunk, nchunk=nchunk),
        out_shape=jax.ShapeDtypeStruct(lf.shape, F32),
        grid=(b,), in_specs=[spec], out_specs=spec,
        compiler_params=_cparams("arbitrary"),
        name="logf_cumsum",
    )(lf)


C_TERMS = 3


def _cumsum_rows_kernel(lf_ref, cs_ref, *, chunk, nchunk):
    r = lax.broadcasted_iota(I32, (chunk, chunk), 0)
    c = lax.broadcasted_iota(I32, (chunk, chunk), 1)
    lower = jnp.where(c <= r, 1.0, 0.0).astype(BF16)
    carry = jnp.zeros((1, LANES), F32)
    for n in range(nchunk):
        rest = lf_ref[pl.ds(n * chunk, chunk), :]
        cs = carry
        for _ in range(C_TERMS):
            term = rest.astype(BF16)
            cs = cs + _dot(lower, term)
            rest = rest - term.astype(F32)
        carry = cs[chunk - 1:chunk, :]
        packed = jnp.zeros((chunk, LANES), F32)
        rest = cs
        for j in range(C_TERMS):
            term = rest.astype(BF16).astype(F32)
            packed = packed + (term if j == 0 else pltpu.roll(term, j * HEADS, 1))
            rest = rest - term
        cs_ref[pl.ds(n * chunk, chunk), :] = packed.astype(BF16)


def _cumsum_rows(lfp, *, seq):
    n = lfp.shape[0]
    spec = pl.BlockSpec((seq, LANES), lambda b: (b, 0))
    return pl.pallas_call(
        functools.partial(_cumsum_rows_kernel, chunk=TOKEN_TILE, nchunk=seq // TOKEN_TILE),
        out_shape=jax.ShapeDtypeStruct((n, LANES), BF16),
        grid=(n // seq,), in_specs=[spec], out_specs=spec,
        compiler_params=_cparams("arbitrary"),
        name="logf_cumsum_rows",
    )(lfp)


def _attn_prompt_kernel(q_ref, k_ref, v_ref, cs_ref, o_ref, ka_ref, vt_ref, m_ref, acc_ref, *, t, nq):
    hp, i = pl.program_id(1), pl.program_id(2)
    rr = lax.broadcasted_iota(I32, (LANES, LANES), 0)
    cc = lax.broadcasted_iota(I32, (LANES, LANES), 1)
    sel = [jnp.where(jnp.where(cc < HEAD_DIM, rr - cc, -1) == HEAD_DIM * h, 1.0, 0.0).astype(BF16) for h in range(2)]

    @pl.when(i == 0)
    def _():
        kb = k_ref[...]
        cs = cs_ref[...]
        vt = v_ref[...].astype(F32).T
        row = lax.broadcasted_iota(I32, (SUBLANES, t), 0)
        ones = jnp.where(row == 0, 1.0, 0.0)
        pad = jnp.zeros((LANES - HEAD_DIM - SUBLANES, t), F32)
        for h in range(2):
            head = 2 * hp + h
            src = jnp.where(cc >= HEAD_DIM, (cc - HEAD_DIM) * HEADS + head, -1)
            src = jnp.where(cc < HEAD_DIM + C_TERMS, src, -1)
            place = jnp.where(rr == src, -1.0, 0.0).astype(BF16)
            ka_ref[h] = (_dot(kb, sel[h]) + _dot(cs, place)).astype(BF16)
            for n in range(nq):
                vh = vt[h * HEAD_DIM:(h + 1) * HEAD_DIM, n * t:(n + 1) * t]
                vt_ref[h, n] = jnp.concatenate([vh, ones, pad], axis=0).astype(BF16)

    lane = lax.broadcasted_iota(I32, (1, LANES), 1)
    one_lanes = jnp.where(jnp.logical_and(lane >= HEAD_DIM, lane < HEAD_DIM + C_TERMS), 1.0, 0.0)
    q = q_ref[...]
    qa = [(_dot(q, sel[h]) + one_lanes).astype(BF16) for h in range(2)]
    m_ref[...] = jnp.full(m_ref.shape, -jnp.inf, F32)
    acc_ref[...] = jnp.zeros(acc_ref.shape, F32)

    def step(j, masked):
        start = pl.multiple_of(j * t, t)
        for h in range(2):
            st = _dot_nt(ka_ref[h, pl.ds(start, t), :], qa[h])
            if masked:
                ki = lax.broadcasted_iota(I32, (t, t), 0)
                qi = lax.broadcasted_iota(I32, (t, t), 1)
                st = jnp.where(ki <= qi, st, -jnp.inf)
            m_prev = m_ref[h]
            m_new = jnp.maximum(m_prev, jnp.max(st, axis=0, keepdims=True))
            p = jnp.exp(st - m_new).astype(BF16)
            m_ref[h] = m_new
            acc_ref[h] = acc_ref[h] * jnp.exp(m_prev - m_new) + _dot(vt_ref[h, j], p)

    def body(j, carry):
        step(j, False)
        return carry

    lax.fori_loop(0, i, body, 0)
    step(i, True)
    halves = []
    for h in range(2):
        acc = acc_ref[h]
        halves.append(acc[:HEAD_DIM] * (1.0 / acc[HEAD_DIM:HEAD_DIM + 1]))
    o_ref[...] = jnp.concatenate(halves, axis=0).T.astype(BF16)


def _attn_prompt(q, k, v, cs, *, nb, seq):
    t = TOKEN_TILE
    nq = seq // t
    return pl.pallas_call(
        functools.partial(_attn_prompt_kernel, t=t, nq=nq),
        out_shape=jax.ShapeDtypeStruct(q.shape, BF16),
        grid=(nb, HEADS // 2, nq),
        in_specs=[pl.BlockSpec((t, LANES), lambda b, hp, i: (b * nq + i, hp)),
                  pl.BlockSpec((seq, LANES), lambda b, hp, i: (b, hp)),
                  pl.BlockSpec((seq, LANES), lambda b, hp, i: (b, hp)),
                  pl.BlockSpec((seq, LANES), lambda b, hp, i: (b, 0))],
        out_specs=pl.BlockSpec((t, LANES), lambda b, hp, i: (b * nq + i, hp)),
        scratch_shapes=[pltpu.VMEM((2, seq, LANES), BF16), pltpu.VMEM((2, nq, LANES, t), BF16),
                        pltpu.VMEM((2, 1, t), F32), pltpu.VMEM((2, LANES, t), F32)],
        compiler_params=_cparams("arbitrary", "arbitrary", "arbitrary"),
        name="attn_prompt",
    )(q, k, v, cs)


def _attn_sample_kernel(q_ref, kc_ref, vc_ref, kn_ref, vn_ref, c_ref, o_ref,
                        m_ref, l_ref, acc_ref, kpad_ref, vpad_ref, *, nq, nchunk):
    j = pl.program_id(1)

    @pl.when(j == 0)
    def _():
        m_ref[...] = jnp.full(m_ref.shape, -jnp.inf, F32)
        l_ref[...] = jnp.zeros(l_ref.shape, F32)
        acc_ref[...] = jnp.zeros(acc_ref.shape, F32)

    def update(h, s, weighted_values):
        m_prev = m_ref[h]
        m_new = jnp.maximum(m_prev, jnp.max(s, axis=-1, keepdims=True))
        a = jnp.exp(m_prev - m_new)
        p = jnp.exp(s - m_new)
        l_ref[h] = a * l_ref[h] + jnp.sum(p, axis=-1, keepdims=True)
        m_ref[h] = m_new
        acc_ref[h] = acc_ref[h] * a + weighted_values(p.astype(BF16))

    @pl.when(j < nchunk)
    def _():
        for h in range(HEADS):
            qh = q_ref[:, h * HEAD_DIM:(h + 1) * HEAD_DIM]
            s = _dot(qh, kc_ref[0, h].astype(BF16)) - c_ref[0, j, h:h + 1, :]
            update(h, s, lambda p, h=h: _dot_nt(p, vc_ref[0, h].astype(BF16)))

    @pl.when(j == nchunk)
    def _():
        kpad_ref[...] = jnp.zeros(kpad_ref.shape, BF16)
        vpad_ref[...] = jnp.zeros(vpad_ref.shape, BF16)
        kpad_ref[pl.ds(0, nq), :] = kn_ref[...]
        vpad_ref[pl.ds(0, nq), :] = vn_ref[...]
        ki = lax.broadcasted_iota(I32, (nq, LANES), 1)
        qi = lax.broadcasted_iota(I32, (nq, LANES), 0)
        outs = []
        for h in range(HEADS):
            sl = slice(h * HEAD_DIM, (h + 1) * HEAD_DIM)
            s = _dot_nt(q_ref[:, sl], kpad_ref[:, sl]) - c_ref[0, nchunk, h:h + 1, :LANES]
            update(h, jnp.where(ki <= qi, s, -jnp.inf), lambda p, sl=sl: _dot(p, vpad_ref[:, sl]))
            outs.append(acc_ref[h] * (1.0 / l_ref[h]))
        o_ref[...] = jnp.concatenate(outs, axis=1).astype(BF16)


def _attn_sample(q, k_cache, v_cache, k_new, v_new, c, *, nq):
    nb, past = k_cache.shape[0], k_cache.shape[3]
    chunk = c.shape[-1]
    nchunk = past // chunk
    assert nq <= LANES and past % chunk == 0 and c.shape[1] == nchunk + 1
    cache_spec = pl.BlockSpec((1, HEADS, HEAD_DIM, chunk), lambda b, j: (b, 0, 0, jnp.minimum(j, nchunk - 1)))
    new_spec = pl.BlockSpec((nq, FOX_DIM), lambda b, j: (b, 0))
    return pl.pallas_call(
        functools.partial(_attn_sample_kernel, nq=nq, nchunk=nchunk),
        out_shape=jax.ShapeDtypeStruct(q.shape, BF16),
        grid=(nb, nchunk + 1),
        in_specs=[new_spec, cache_spec, cache_spec, new_spec, new_spec,
                  pl.BlockSpec((1, nchunk + 1, HEADS, chunk), lambda b, j: (b, 0, 0, 0))],
        out_specs=new_spec,
        scratch_shapes=[pltpu.VMEM((HEADS, nq, 1), F32), pltpu.VMEM((HEADS, nq, 1), F32),
                        pltpu.VMEM((HEADS, nq, HEAD_DIM), F32), pltpu.VMEM((LANES, FOX_DIM), BF16),
                        pltpu.VMEM((LANES, FOX_DIM), BF16)],
        compiler_params=_cparams("arbitrary", "arbitrary"),
        name="attn_sample",
    )(q, k_cache, v_cache, k_new, v_new, c)


def _pick(is_first, a_ref, b_ref):
    return jnp.where(is_first, a_ref[...], b_ref[...])


def _post_attn_kernel(xp_ref, xs_ref, op_ref, os_ref, sap_ref, sas_ref, mbp_ref, mbs_ref,
                      wpa_ref, wo_ref, g_ref, wrt_ref, br_ref,
                      h1_ref, xn_ref, pos_ref, gate_ref, cnt_ref, *, n_first, tm):
    is_p = pl.program_id(0) < n_first
    ya = _dot(_pick(is_p, op_ref, os_ref), wpa_ref[...])
    merged = _pick(is_p, sap_ref, sas_ref).astype(F32) * ya + _pick(is_p, mbp_ref, mbs_ref).astype(F32)
    h1 = _pick(is_p, xp_ref, xs_ref) + _dot(merged.astype(BF16), wo_ref[...])
    h1_ref[...] = h1
    xn = _rms(h1, g_ref[...]).astype(BF16)
    xn_ref[...] = xn

    lt = _dot_nt(wrt_ref[...], xn) + br_ref[...]
    eio = lax.broadcasted_iota(I32, (N_EXPERTS, tm), 0).astype(F32)
    vals, hots = [], []
    for _ in range(TOP_K):
        m = jnp.max(lt, axis=0, keepdims=True)
        idx = jnp.min(jnp.where(lt == m, eio, float(N_EXPERTS)), axis=0, keepdims=True)
        hot = eio == idx
        vals.append(m)
        hots.append(hot)
        lt = jnp.where(hot, -jnp.inf, lt)
    ex = [jnp.exp(v - vals[0]) for v in vals]
    den = ex[0] + ex[1] + ex[2] + ex[3]
    gate_ref[...] = jnp.concatenate([e / den for e in ex], axis=0)

    chosen = jnp.zeros((N_EXPERTS, tm), F32)
    for hot in hots:
        chosen = jnp.where(hot, 1.0, chosen)
    td = DISPATCH_TILE
    r = lax.broadcasted_iota(I32, (td, td), 0)
    c = lax.broadcasted_iota(I32, (td, td), 1)
    before = jnp.where(r < c, 1.0, 0.0).astype(BF16)
    er = lax.broadcasted_iota(I32, (N_EXPERTS, N_EXPERTS), 0)
    ec = lax.broadcasted_iota(I32, (N_EXPERTS, N_EXPERTS), 1)
    lower = jnp.where(ec < er, 1.0, 0.0).astype(BF16)
    for sub in range(tm // td):
        sl = slice(sub * td, (sub + 1) * td)
        ch = chosen[:, sl]
        rank = _dot(ch.astype(BF16), before)
        cnt = rank[:, td - 1:td] + ch[:, td - 1:td]
        units = jnp.floor((cnt + (ROW_ALIGN - 1)) * (1.0 / ROW_ALIGN))
        start = ROW_ALIGN * _dot(lower, jnp.broadcast_to(units, (N_EXPERTS, td)).astype(BF16))
        base = start + rank
        pos = [jnp.sum(jnp.where(hot[:, sl], base, 0.0), axis=0, keepdims=True) for hot in hots]
        pos_ref[:, sl] = jnp.concatenate(pos, axis=0).astype(I32)
        cnt_ref[sub] = jnp.broadcast_to(cnt, (N_EXPERTS, LANES)).astype(I32)


def _post_attn(xp, xs, op, os_, sap, sas, mbp, mbs, wpa, wo, g, wrt, br):
    n_p, d = xp.shape
    n_s = xs.shape[0]
    tm = TOKEN_TILE
    assert n_p % tm == 0 and n_s % tm == 0
    n_first, n_tiles = n_p // tm, (n_p + n_s) // tm
    n = n_p + n_s
    sub = tm // DISPATCH_TILE
    first = lambda i: (jnp.minimum(i, n_first - 1), 0)
    second = lambda i: (jnp.maximum(i - n_first, 0), 0)
    row = lambda i: (i, 0)
    col = lambda i: (0, i)

    def pair(width):
        return [pl.BlockSpec((tm, width), first), pl.BlockSpec((tm, width), second)]

    return pl.pallas_call(
        functools.partial(_post_attn_kernel, n_first=n_first, tm=tm),
        out_shape=[jax.ShapeDtypeStruct((n, d), F32), jax.ShapeDtypeStruct((n, d), BF16),
                   jax.ShapeDtypeStruct((TOP_K, n), I32), jax.ShapeDtypeStruct((TOP_K, n), F32),
                   jax.ShapeDtypeStruct((n // DISPATCH_TILE, N_EXPERTS, LANES), I32)],
        grid=(n_tiles,),
        in_specs=pair(d) + pair(FOX_DIM) + pair(d) + pair(d) + [_full(w.shape) for w in (wpa, wo, g, wrt, br)],
        out_specs=[pl.BlockSpec((tm, d), row), pl.BlockSpec((tm, d), row),
                   pl.BlockSpec((TOP_K, tm), col), pl.BlockSpec((TOP_K, tm), col),
                   pl.BlockSpec((sub, N_EXPERTS, LANES), lambda i: (i, 0, 0))],
        compiler_params=_cparams("arbitrary"),
        name="post_attn_router",
    )(xp, xs, op, os_, sap, sas, mbp, mbs, wpa, wo, g, wrt, br)


def _for_each_run(tile, len_ref, loc_ref, dst_ref, fn):
    def body(e, carry):
        idx = tile * N_EXPERTS + e
        n, loc, dst = len_ref[idx], loc_ref[idx], dst_ref[idx]
        for c in RUN_CHUNKS:
            off = n & ~(2 * c - 1)

            @pl.when((n & c) != 0)
            def _():
                fn(pl.multiple_of(loc + off, ROW_ALIGN), pl.multiple_of(dst + off, ROW_ALIGN), c)
        return carry

    for e in range(N_EXPERTS):
        body(e, 0)


def _dispatch_kernel(len_ref, loc_ref, dst_ref, xn_ref, pos_ref, xs_hbm, xloc_ref, sem_ref, *, n_tiles):
    i = pl.program_id(0)
    slot = lax.rem(i, 2)

    def copy(s, loc, dst, rows):
        return pltpu.make_async_copy(xloc_ref.at[s, pl.ds(loc, rows), :], xs_hbm.at[pl.ds(dst, rows), :], sem_ref.at[s])

    def start_tile(tile, s):
        _for_each_run(tile, len_ref, loc_ref, dst_ref, lambda loc, dst, rows: copy(s, loc, dst, rows).start())

    def wait_tile(tile, s):
        _for_each_run(tile, len_ref, loc_ref, dst_ref, lambda loc, dst, rows: copy(s, loc, dst, rows).wait())

    pos = pos_ref[...]
    aio = lax.broadcasted_iota(I32, (LOCAL_ROWS, DISPATCH_TILE), 0)
    hit = aio == pos[0:1, :]
    for k in range(1, TOP_K):
        hit = jnp.logical_or(hit, aio == pos[k:k + 1, :])
    sel = jnp.where(hit, 1.0, 0.0).astype(BF16)
    rows_sorted = _dot(sel, xn_ref[...])

    @pl.when(i >= 2)
    def _():
        wait_tile(i - 2, slot)

    xloc_ref[slot] = rows_sorted
    start_tile(i, slot)

    @pl.when(i == n_tiles - 1)
    def _():
        if n_tiles >= 2:
            wait_tile(i - 1, 1 - slot)
        wait_tile(i, slot)


def _dispatch(run_len, run_loc, run_dst, xn, pos, *, total_rows):
    n, d = xn.shape
    td = DISPATCH_TILE
    n_tiles = n // td
    return pl.pallas_call(
        functools.partial(_dispatch_kernel, n_tiles=n_tiles),
        out_shape=jax.ShapeDtypeStruct((total_rows, d), F32),
        grid_spec=pltpu.PrefetchScalarGridSpec(
            num_scalar_prefetch=3, grid=(n_tiles,),
            in_specs=[pl.BlockSpec((td, d), lambda i, *_: (i, 0)), pl.BlockSpec((TOP_K, td), lambda i, *_: (0, i))],
            out_specs=pl.BlockSpec(memory_space=pl.ANY),
            scratch_shapes=[pltpu.VMEM((2, LOCAL_ROWS, d), F32), pltpu.SemaphoreType.DMA((2,))]),
        compiler_params=_cparams("arbitrary"),
        name="moe_dispatch",
    )(run_len, run_loc, run_dst, xn, pos)


def _ffn_kernel(bexp_ref, nval_ref, last_ref, xs_ref, wgu_ref, bgu_ref, wdn_ref, bdn_ref, o_ref, wgu_s, wdn_s):
    i = pl.program_id(0)
    nv = nval_ref[i]

    @pl.when(jnp.logical_or(i == 0, bexp_ref[i] != bexp_ref[jnp.maximum(i - 1, 0)]))
    def _():
        wgu_s[...] = wgu_ref[0].astype(BF16)
        wdn_s[...] = wdn_ref[0].astype(BF16)

    @pl.when(nv > 0)
    def _():
        rows = lax.broadcasted_iota(I32, xs_ref.shape, 0)
        x = jnp.where(rows < nv, xs_ref[...], 0.0).astype(BF16)
        gu = _dot(x, wgu_s[...]) + bgu_ref[0]
        gt = jnp.minimum(gu[:, :D_FF], SWIGLU_LIMIT)
        up = jnp.clip(gu[:, D_FF:], -SWIGLU_LIMIT, SWIGLU_LIMIT)
        act = (up + 1.0) * gt * _sigmoid(SWIGLU_ALPHA * gt)
        o_ref[...] = _dot(act.astype(BF16), wdn_s[...]) + bdn_ref[0]


def _ffn(bexp, nval, last, xs, wgu, bgu, wdn, bdn):
    rows, d = xs.shape
    nblk = rows // FFN_ROWS
    blk = lambda i, bexp, nval, last: (jnp.minimum(i, last[0]), 0)
    exp3 = lambda i, bexp, nval, last: (bexp[i], 0, 0)
    return pl.pallas_call(
        _ffn_kernel,
        out_shape=jax.ShapeDtypeStruct((rows, d), F32),
        grid_spec=pltpu.PrefetchScalarGridSpec(
            num_scalar_prefetch=3, grid=(nblk,),
            in_specs=[pl.BlockSpec((FFN_ROWS, d), blk),
                      pl.BlockSpec((1, d, 2 * D_FF), exp3), pl.BlockSpec((1, 1, 2 * D_FF), exp3),
                      pl.BlockSpec((1, D_FF, d), exp3), pl.BlockSpec((1, 1, d), exp3)],
            out_specs=pl.BlockSpec((FFN_ROWS, d), blk),
            scratch_shapes=[pltpu.VMEM((d, 2 * D_FF), BF16), pltpu.VMEM((D_FF, d), BF16)]),
        compiler_params=_cparams("arbitrary"),
        name="moe_ffn",
    )(bexp, nval, last, xs, wgu, bgu, wdn, bdn)


def _combine_kernel(len_ref, loc_ref, dst_ref, os_hbm, pos_ref, gate_ref, h1_ref, pp_ref, ps_ref,
                    gple_ref, wpg_ref, wpp_ref, gfin_ref, yp_ref, ys_ref, oloc_ref, sem_ref, *, n_tiles, n_first):
    i = pl.program_id(0)
    slot = lax.rem(i, 2)

    def copy(s, loc, dst, rows):
        return pltpu.make_async_copy(os_hbm.at[pl.ds(dst, rows), :], oloc_ref.at[s, pl.ds(loc, rows), :], sem_ref.at[s])

    def start_tile(tile, s):
        _for_each_run(tile, len_ref, loc_ref, dst_ref, lambda loc, dst, rows: copy(s, loc, dst, rows).start())

    def wait_tile(tile, s):
        _for_each_run(tile, len_ref, loc_ref, dst_ref, lambda loc, dst, rows: copy(s, loc, dst, rows).wait())

    @pl.when(i == 0)
    def _():
        oloc_ref[...] = jnp.zeros(oloc_ref.shape, F32)
        start_tile(0, 0)

    @pl.when(i + 1 < n_tiles)
    def _():
        start_tile(i + 1, 1 - slot)

    wait_tile(i, slot)

    pos, gate = pos_ref[...], gate_ref[...]
    lio = lax.broadcasted_iota(I32, (DISPATCH_TILE, LOCAL_ROWS), 1)
    w = jnp.zeros((DISPATCH_TILE, LOCAL_ROWS), F32)
    for k in range(TOP_K):
        w = jnp.where(lio == pos[:, k:k + 1], gate[:, k:k + 1], w)
    h2 = h1_ref[...] + _dot(w.astype(BF16), oloc_ref[slot].astype(BF16))

    is_p = i < n_first
    xn = _rms(h2, gple_ref[...]).astype(BF16)
    ple = _dot(_pick(is_p, pp_ref, ps_ref).astype(BF16), wpp_ref[...])
    h3 = h2 + _sigmoid(_dot(xn, wpg_ref[...])) * ple
    y = _rms(h3, gfin_ref[...])

    @pl.when(is_p)
    def _():
        yp_ref[...] = y

    @pl.when(jnp.logical_not(is_p))
    def _():
        ys_ref[...] = y


def _combine(run_len, run_loc, run_dst, os_, pos, gate, h1, pp, ps, gple, wpg, wpp, gfin):
    n, d = h1.shape
    td = DISPATCH_TILE
    n_p, n_s = pp.shape[0], ps.shape[0]
    assert n_p % td == 0 and n_s % td == 0
    n_tiles, n_first = n // td, n_p // td
    pw = pp.shape[1]
    first = lambda i, *_: (jnp.minimum(i, n_first - 1), 0)
    second = lambda i, *_: (jnp.maximum(i - n_first, 0), 0)
    row = lambda i, *_: (i, 0)
    full2 = lambda i, *_: (0, 0)
    return pl.pallas_call(
        functools.partial(_combine_kernel, n_tiles=n_tiles, n_first=n_first),
        out_shape=[jax.ShapeDtypeStruct((n_p, d), F32), jax.ShapeDtypeStruct((n_s, d), F32)],
        grid_spec=pltpu.PrefetchScalarGridSpec(
            num_scalar_prefetch=3, grid=(n_tiles,),
            in_specs=[pl.BlockSpec(memory_space=pl.ANY),
                      pl.BlockSpec((td, TOP_K), row), pl.BlockSpec((td, TOP_K), row), pl.BlockSpec((td, d), row),
                      pl.BlockSpec((td, pw), first), pl.BlockSpec((td, pw), second),
                      pl.BlockSpec(gple.shape, full2), pl.BlockSpec(wpg.shape, full2),
                      pl.BlockSpec(wpp.shape, full2), pl.BlockSpec(gfin.shape, full2)],
            out_specs=[pl.BlockSpec((td, d), first), pl.BlockSpec((td, d), second)],
            scratch_shapes=[pltpu.VMEM((2, LOCAL_ROWS, d), F32), pltpu.SemaphoreType.DMA((2,))]),
        compiler_params=_cparams("arbitrary"),
        name="moe_combine_ple",
    )(run_len, run_loc, run_dst, os_, pos, gate, h1, pp, ps, gple, wpg, wpp, gfin)


def _routing_tables(cnt, total_rows):
    n_tiles = cnt.shape[0]
    run_len = (cnt + (ROW_ALIGN - 1)) // ROW_ALIGN * ROW_ALIGN
    run_loc = jnp.cumsum(run_len, axis=1) - run_len
    per_expert = jnp.sum(run_len, axis=0)
    region = (per_expert + (FFN_ROWS - 1)) // FFN_ROWS * FFN_ROWS
    region_end = jnp.cumsum(region)
    region_start = region_end - region
    run_dst = region_start[None, :] + jnp.cumsum(run_len, axis=0) - run_len
    nblk = total_rows // FFN_ROWS
    blk_row = jnp.arange(nblk, dtype=I32) * FFN_ROWS
    used = region_end[-1] // FFN_ROWS
    last = jnp.maximum(used - 1, 0)
    bexp = jnp.minimum(jnp.sum(blk_row[:, None] >= region_end[None, :], axis=1), N_EXPERTS - 1).astype(I32)
    nval = jnp.clip(per_expert[bexp] - (blk_row - region_start[bexp]), 0, FFN_ROWS)
    nval = jnp.where(blk_row < region_end[-1], nval, 0).astype(I32)
    bexp = jnp.where(blk_row < region_end[-1], bexp, bexp[last])
    flat = lambda a: a.reshape(n_tiles * N_EXPERTS).astype(I32)
    return flat(run_len), flat(run_loc), flat(run_dst), bexp, nval, last.reshape(1).astype(I32)


def _chunked_t(lf, chunk):
    b, length, h = lf.shape
    return lf.reshape(b, length // chunk, chunk, h).transpose(0, 1, 3, 2)


def kernel(x_prompt, x_sample, p_prompt, p_sample, cache_k, cache_v, cache_logf, state_conv, g_mix, w_in, b_f, w_conv, w_pa, w_pb, w_o, g_ffn, w_router, b_router, w_gu, b_gu, w_dn, b_dn, g_ple, w_ple_gate, w_ple_proj, g_final):
    depth = g_mix.shape[0]
    assert depth == 1
    nb, seq, d = x_prompt.shape
    db, dseq, _ = x_sample.shape
    past = cache_k.shape[2]
    n_p, n_s = nb * seq, db * dseq

    w = w_in[0].astype(BF16)
    o_f = 3 * FOX_DIM
    o_c = o_f + HEADS
    o_g = o_c + 3 * CONV_DIM
    wf = jnp.pad(w[:, o_f:o_c], ((0, 0), (0, LANES - HEADS)))
    bf = jnp.pad(b_f[0], (0, LANES - HEADS)).reshape(1, LANES)
    mixer_w = (g_mix[0].reshape(1, d), w[:, :o_f], wf, bf, w[:, o_c:o_g], w_conv[0], w[:, o_g:], w_pb[0].astype(BF16))

    (q_p, k_p, v_p, ko_p, vo_p, logf_p, lfp_p, sa_p, mb_p), tail_p = _in_proj(
        x_prompt.reshape(n_p, d), jnp.zeros((nb, CONV_WIDTH - 1, CONV_DIM), F32), mixer_w, seq_len=seq)
    o_p = _attn_prompt(q_p, k_p, v_p, _cumsum_rows(lfp_p, seq=seq), nb=nb, seq=seq)

    (q_s, k_s, v_s, ko_s, vo_s, logf_s, _, sa_s, mb_s), tail_s = _in_proj(
        x_sample.reshape(n_s, d), state_conv[0], mixer_w, seq_len=dseq)
    chunk = min(past, 2 * TOKEN_TILE)
    lf_all = jnp.concatenate([cache_logf[0].astype(F32), logf_s.reshape(db, dseq, HEADS),
                              jnp.zeros((db, chunk - dseq, HEADS), F32)], axis=1)
    c_s = _cumsum_chunks(_chunked_t(lf_all, chunk))
    o_s = _attn_sample(q_s, cache_k[0].transpose(0, 2, 3, 1), cache_v[0].transpose(0, 2, 3, 1), k_s, v_s, c_s, nq=dseq)

    h1, xn2, pos_t, gate_t, cnt = _post_attn(
        x_prompt.reshape(n_p, d), x_sample.reshape(n_s, d), o_p, o_s, sa_p, sa_s, mb_p, mb_s,
        w_pa[0].astype(BF16), w_o[0].astype(BF16), g_ffn[0].reshape(1, d),
        w_router[0].T.astype(BF16), b_router[0].reshape(N_EXPERTS, 1))

    n = n_p + n_s
    n_tiles = n // DISPATCH_TILE
    max_rows = n * TOP_K + n_tiles * N_EXPERTS * (ROW_ALIGN - 1) + N_EXPERTS * (FFN_ROWS - 1)
    total_rows = -(-max_rows // FFN_ROWS) * FFN_ROWS
    run_len, run_loc, run_dst, bexp, nval, last = _routing_tables(cnt[:, :, 0], total_rows)
    xs = _dispatch(run_len, run_loc, run_dst, xn2, pos_t, total_rows=total_rows)
    os_ = _ffn(bexp, nval, last, xs, w_gu[0], b_gu[0].reshape(N_EXPERTS, 1, 2 * D_FF),
               w_dn[0], b_dn[0].reshape(N_EXPERTS, 1, d))

    y_p, y_s = _combine(run_len, run_loc, run_dst, os_, pos_t.T, gate_t.T, h1,
                        p_prompt[0].reshape(n_p, -1), p_sample[0].reshape(n_s, -1),
                        g_ple[0].reshape(1, d), w_ple_gate[0].astype(BF16), w_ple_proj[0].astype(BF16),
                        g_final.reshape(1, d))

    return (y_p.reshape(nb, seq, d), y_s.reshape(db, dseq, d),
            ko_p.transpose(0, 3, 1, 2)[None], vo_p.transpose(0, 3, 1, 2)[None],
            logf_p.reshape(1, nb, seq, HEADS), tail_p[None],
            ko_s.reshape(1, db, dseq, HEADS, HEAD_DIM), vo_s.reshape(1, db, dseq, HEADS, HEAD_DIM),
            logf_s.reshape(1, db, dseq, HEADS), tail_s[None])
```

```python
import functools

import jax
import jax.numpy as jnp
from jax import lax
from jax.experimental import pallas as pl
from jax.experimental.pallas import tpu as pltpu

F32, BF16, I32 = jnp.float32, jnp.bfloat16, jnp.int32

HEADS = 8
HEAD_DIM = 64
FOX_DIM = HEADS * HEAD_DIM
CONV_DIM = 512
CONV_WIDTH = 3
N_EXPERTS = 32
TOP_K = 4
D_FF = 1024
SWIGLU_ALPHA = 1.702
SWIGLU_LIMIT = 7.0
RMS_EPS = 1e-6
LOG2E = 1.4426950408889634

LANES = 128
SUBLANES = 8
TOKEN_TILE = 512
KEY_CHUNK = 512
DISPATCH_TILE = 256
ROW_ALIGN = SUBLANES
LOCAL_ROWS = DISPATCH_TILE * TOP_K + N_EXPERTS * ROW_ALIGN
FFN_ROWS = 512
RUN_CHUNKS = (256, 128, 64, 32, 16, 8)
VMEM_LIMIT_BYTES = 56 * 1024 * 1024


def _cparams(*sem):
    return pltpu.CompilerParams(dimension_semantics=sem, vmem_limit_bytes=VMEM_LIMIT_BYTES)


def _rms(x, g):
    return x * lax.rsqrt(jnp.mean(x * x, axis=-1, keepdims=True) + RMS_EPS) * g


def _sigmoid(x):
    return 1.0 / (1.0 + jnp.exp(-x))


def _log_sigmoid(x):
    return jnp.minimum(x, 0.0) - jnp.log1p(jnp.exp(-jnp.abs(x)))


def _dot(a, b):
    return jnp.dot(a, b, preferred_element_type=F32)


def _dot_nt(a, b):
    return lax.dot_general(a, b, (((1,), (1,)), ((), ())), preferred_element_type=F32)


def _mixer_inputs(x_ref, g_ref, wqkv_ref, wf_ref, bf_ref, wc_ref, wgl_ref,
                  q_ref, k_ref, v_ref, ko_ref, vo_ref, logf_ref, lfp_ref, sa_ref, *, keys_minor):
    xn = _rms(x_ref[...], g_ref[...]).astype(BF16)
    qkv = _dot(xn, wqkv_ref[...])
    q_ref[...] = (qkv[:, :FOX_DIM] * (HEAD_DIM ** -0.5 * LOG2E)).astype(BF16)
    k = qkv[:, FOX_DIM:2 * FOX_DIM]
    v = qkv[:, 2 * FOX_DIM:]
    k_ref[...] = k.astype(BF16)
    v_ref[...] = v.astype(BF16)
    if keys_minor:
        ko_ref[0] = k.T.reshape(HEADS, HEAD_DIM, k.shape[0])
        vo_ref[0] = v.T.reshape(HEADS, HEAD_DIM, v.shape[0])
    else:
        for h in range(HEADS):
            ko_ref[:, h, :] = k[:, h * HEAD_DIM:(h + 1) * HEAD_DIM]
            vo_ref[:, h, :] = v[:, h * HEAD_DIM:(h + 1) * HEAD_DIM]
    logf = _log_sigmoid(_dot(xn, wf_ref[...]) + bf_ref[...])
    logf_ref[...] = logf[:, :HEADS]
    lane = lax.broadcasted_iota(I32, logf.shape, 1)
    lfp_ref[...] = jnp.where(lane < HEADS, logf, 0.0)
    c3 = _dot(xn, wc_ref[...])
    gate_b = c3[:, :CONV_DIM]
    z = c3[:, CONV_DIM:2 * CONV_DIM] * c3[:, 2 * CONV_DIM:]
    gl = _dot(xn, wgl_ref[...])
    d = gl.shape[1] // 2
    sa_ref[...] = _sigmoid(gl[:, :d]).astype(BF16)
    return gate_b, z, _sigmoid(gl[:, d:])


def _conv_out(gate_b, z, z1, z2, sig_b, wconv_ref, wpb_ref, mb_ref):
    zc = wconv_ref[0:1, :] * z2 + wconv_ref[1:2, :] * z1 + wconv_ref[2:3, :] * z
    yb = _dot((gate_b * zc).astype(BF16), wpb_ref[...])
    mb_ref[...] = (sig_b * yb).astype(BF16)


N_MIXER_W = 8
N_MIXER_OUT = 9


def _mixer_front(x_ref, w_refs, out_refs, keys_minor):
    g_ref, wqkv_ref, wf_ref, bf_ref, wc_ref, wconv_ref, wgl_ref, wpb_ref = w_refs
    gate_b, z, sig_b = _mixer_inputs(x_ref, g_ref, wqkv_ref, wf_ref, bf_ref, wc_ref, wgl_ref, *out_refs[:-1],
                                     keys_minor=keys_minor)
    return gate_b, z, sig_b, wconv_ref, wpb_ref, out_refs[-1]


def _in_proj_seq_kernel(x_ref, prev_ref, *refs, tm):
    w_refs, out_refs = refs[:N_MIXER_W], refs[N_MIXER_W:N_MIXER_W + N_MIXER_OUT]
    tail_ref, zbuf_ref = refs[N_MIXER_W + N_MIXER_OUT:]
    gate_b, z, sig_b, wconv_ref, wpb_ref, mb_ref = _mixer_front(x_ref, w_refs, out_refs, True)
    zbuf_ref[pl.ds(SUBLANES, tm), :] = z

    @pl.when(pl.program_id(1) == 0)
    def _():
        zbuf_ref[pl.ds(SUBLANES - 2, 2), :] = prev_ref[0]

    z1 = zbuf_ref[pl.ds(SUBLANES - 1, tm), :]
    z2 = zbuf_ref[pl.ds(SUBLANES - 2, tm), :]
    _conv_out(gate_b, z, z1, z2, sig_b, wconv_ref, wpb_ref, mb_ref)
    tail = zbuf_ref[pl.ds(tm + SUBLANES - 2, 2), :]
    zbuf_ref[pl.ds(SUBLANES - 2, 2), :] = tail
    tail_ref[0] = tail


def _in_proj_multi_kernel(x_ref, ov1_ref, ov2_ref, *refs, tm, seq_len):
    w_refs, out_refs = refs[:N_MIXER_W], refs[N_MIXER_W:N_MIXER_W + N_MIXER_OUT]
    z_ref, zbuf_ref = refs[N_MIXER_W + N_MIXER_OUT:]
    gate_b, z, sig_b, wconv_ref, wpb_ref, mb_ref = _mixer_front(x_ref, w_refs, out_refs, False)
    z_ref[...] = z
    zbuf_ref[pl.ds(0, SUBLANES), :] = jnp.zeros((SUBLANES, CONV_DIM), F32)
    zbuf_ref[pl.ds(SUBLANES, tm), :] = z
    t = lax.broadcasted_iota(I32, (tm, CONV_DIM), 0) & (seq_len - 1)
    z1 = jnp.where(t == 0, ov1_ref[...], zbuf_ref[pl.ds(SUBLANES - 1, tm), :])
    z2 = jnp.where(t < 2, ov2_ref[...], zbuf_ref[pl.ds(SUBLANES - 2, tm), :])
    _conv_out(gate_b, z, z1, z2, sig_b, wconv_ref, wpb_ref, mb_ref)


def _full(shape):
    n = len(shape)
    return pl.BlockSpec(shape, lambda *_: (0,) * n)


def _in_proj(x, conv_prev, weights, *, seq_len):
    g, wqkv, wf, bf, wc, wconv, wgl, wpb = weights
    n, d = x.shape
    w_specs = [_full(w.shape) for w in (g, wqkv, wf, bf, wc, wconv, wgl, wpb)]
    shapes = (((FOX_DIM,), BF16), ((FOX_DIM,), BF16), ((FOX_DIM,), BF16), ((HEADS, HEAD_DIM), F32), ((HEADS, HEAD_DIM), F32),
              ((HEADS,), F32), ((LANES,), F32), ((d,), BF16), ((d,), BF16))
    assert len(weights) == N_MIXER_W and len(shapes) == N_MIXER_OUT
    out_shape = [jax.ShapeDtypeStruct((n,) + s, t) for s, t in shapes]
    n_common = N_MIXER_OUT
    tm = TOKEN_TILE

    def out_specs(tile_index):
        return [pl.BlockSpec((tm,) + s, lambda *a, k=len(s): (tile_index(*a),) + (0,) * k) for s, _ in shapes]

    if seq_len % tm == 0:
        nb, nj = n // seq_len, seq_len // tm
        row = lambda b, j: (b * nj + j, 0)
        specs = out_specs(lambda b, j: b * nj + j)
        for i in (3, 4):
            out_shape[i] = jax.ShapeDtypeStruct((nb, HEADS, HEAD_DIM, seq_len), F32)
            specs[i] = pl.BlockSpec((1, HEADS, HEAD_DIM, tm), lambda b, j: (b, 0, 0, j))
        outs = pl.pallas_call(
            functools.partial(_in_proj_seq_kernel, tm=tm),
            out_shape=out_shape + [jax.ShapeDtypeStruct((nb, CONV_WIDTH - 1, CONV_DIM), F32)],
            grid=(nb, nj),
            in_specs=[pl.BlockSpec((tm, d), row), pl.BlockSpec((1, CONV_WIDTH - 1, CONV_DIM), lambda b, j: (b, 0, 0))] + w_specs,
            out_specs=specs + [pl.BlockSpec((1, CONV_WIDTH - 1, CONV_DIM), lambda b, j: (b, 0, 0))],
            scratch_shapes=[pltpu.VMEM((tm + SUBLANES, CONV_DIM), F32)],
            compiler_params=_cparams("arbitrary", "arbitrary"),
            name="in_proj_seq",
        )(x, conv_prev, g, wqkv, wf, bf, wc, wconv, wgl, wpb)
        return outs[:n_common], outs[n_common]
    assert seq_len & (seq_len - 1) == 0 and seq_len >= CONV_WIDTH - 1
    assert n % tm == 0 and tm % seq_len == 0
    first = jnp.zeros((n // seq_len, seq_len, CONV_DIM), F32)
    ov1 = first.at[:, 0].set(conv_prev[:, 1]).reshape(n, CONV_DIM)
    ov2 = first.at[:, 0].set(conv_prev[:, 0]).at[:, 1].set(conv_prev[:, 1]).reshape(n, CONV_DIM)
    row = lambda i: (i, 0)
    outs = pl.pallas_call(
        functools.partial(_in_proj_multi_kernel, tm=tm, seq_len=seq_len),
        out_shape=out_shape + [jax.ShapeDtypeStruct((n, CONV_DIM), F32)],
        grid=(n // tm,),
        in_specs=[pl.BlockSpec((tm, d), row), pl.BlockSpec((tm, CONV_DIM), row), pl.BlockSpec((tm, CONV_DIM), row)] + w_specs,
        out_specs=out_specs(lambda i: i) + [pl.BlockSpec((tm, CONV_DIM), row)],
        scratch_shapes=[pltpu.VMEM((tm + SUBLANES, CONV_DIM), F32)],
        compiler_params=_cparams("arbitrary"),
        name="in_proj_multi",
    )(x, ov1, ov2, g, wqkv, wf, bf, wc, wconv, wgl, wpb)
    tail = outs[n_common].reshape(n // seq_len, seq_len, CONV_DIM)[:, seq_len - (CONV_WIDTH - 1):]
    return outs[:n_common], tail


def _cumsum_kernel(lf_ref, c_ref, *, chunk, nchunk):
    r = lax.broadcasted_iota(I32, (chunk, chunk), 0)
    c = lax.broadcasted_iota(I32, (chunk, chunk), 1)
    upper = jnp.where(r <= c, 1.0, 0.0).astype(BF16)
    carry = jnp.zeros((HEADS, 1), F32)
    for n in range(nchunk):
        a = lf_ref[0, n]
        hi = a.astype(BF16)
        r1 = a - hi.astype(F32)
        mid = r1.astype(BF16)
        lo = (r1 - mid.astype(F32)).astype(BF16)
        cs = _dot(hi, upper) + _dot(mid, upper) + _dot(lo, upper) + carry
        c_ref[0, n] = cs
        carry = cs[:, chunk - 1:chunk]


def _cumsum_chunks(lf):
    b, nchunk, _, chunk = lf.shape
    spec = pl.BlockSpec((1, nchunk, HEADS, chunk), lambda i: (i, 0, 0, 0))
    return pl.pallas_call(
        functools.partial(_cumsum_kernel, chunk=chunk, nchunk=nchunk),
        out_shape=jax.ShapeDtypeStruct(lf.shape, F32),
        grid=(b,), in_specs=[spec], out_specs=spec,
        compiler_params=_cparams("arbitrary"),
        name="logf_cumsum",
    )(lf)


C_TERMS = 3


def _cumsum_rows_kernel(lf_ref, cs_ref, *, chunk, nchunk):
    r = lax.broadcasted_iota(I32, (chunk, chunk), 0)
    c = lax.broadcasted_iota(I32, (chunk, chunk), 1)
    lower = jnp.where(c <= r, 1.0, 0.0).astype(BF16)
    carry = jnp.zeros((1, LANES), F32)
    for n in range(nchunk):
        rest = lf_ref[pl.ds(n * chunk, chunk), :]
        cs = carry
        for _ in range(C_TERMS):
            term = rest.astype(BF16)
            cs = cs + _dot(lower, term)
            rest = rest - term.astype(F32)
        carry = cs[chunk - 1:chunk, :]
        packed = jnp.zeros((chunk, LANES), F32)
        rest = cs * LOG2E
        for j in range(C_TERMS):
            term = rest.astype(BF16).astype(F32)
            packed = packed + (term if j == 0 else pltpu.roll(term, j * HEADS, 1))
            rest = rest - term
        cs_ref[pl.ds(n * chunk, chunk), :] = packed.astype(BF16)


def _cumsum_rows(lfp, *, seq):
    n = lfp.shape[0]
    spec = pl.BlockSpec((seq, LANES), lambda b: (b, 0))
    return pl.pallas_call(
        functools.partial(_cumsum_rows_kernel, chunk=TOKEN_TILE, nchunk=seq // TOKEN_TILE),
        out_shape=jax.ShapeDtypeStruct((n, LANES), BF16),
        grid=(n // seq,), in_specs=[spec], out_specs=spec,
        compiler_params=_cparams("arbitrary"),
        name="logf_cumsum_rows",
    )(lfp)


def _attn_prompt_kernel(q_ref, k_ref, v_ref, cs_ref, o_ref, ka_ref, vt_ref, m_ref, acc_ref, *, t, tk, nk):
    hp, i = pl.program_id(1), pl.program_id(2)
    rr = lax.broadcasted_iota(I32, (LANES, LANES), 0)
    cc = lax.broadcasted_iota(I32, (LANES, LANES), 1)
    sel = [jnp.where(jnp.where(cc < HEAD_DIM, rr - cc, -1) == HEAD_DIM * h, 1.0, 0.0).astype(BF16) for h in range(2)]

    @pl.when(i == 0)
    def _():
        kb = k_ref[...]
        cs = cs_ref[...]
        vt = v_ref[...].astype(F32).T
        row = lax.broadcasted_iota(I32, (SUBLANES, tk), 0)
        ones = jnp.where(row == 0, 1.0, 0.0)
        pad = jnp.zeros((LANES - HEAD_DIM - SUBLANES, tk), F32)
        for h in range(2):
            head = 2 * hp + h
            src = jnp.where(cc >= HEAD_DIM, (cc - HEAD_DIM) * HEADS + head, -1)
            src = jnp.where(cc < HEAD_DIM + C_TERMS, src, -1)
            place = jnp.where(rr == src, -1.0, 0.0).astype(BF16)
            ka_ref[h] = (_dot(kb, sel[h]) + _dot(cs, place)).astype(BF16)
            for n in range(nk):
                vh = vt[h * HEAD_DIM:(h + 1) * HEAD_DIM, n * tk:(n + 1) * tk]
                vt_ref[h, n] = jnp.concatenate([vh, ones, pad], axis=0).astype(BF16)

    lane = lax.broadcasted_iota(I32, (1, LANES), 1)
    one_lanes = jnp.where(jnp.logical_and(lane >= HEAD_DIM, lane < HEAD_DIM + C_TERMS), 1.0, 0.0)
    q = q_ref[...]
    qa = [(_dot(q, sel[h]) + one_lanes).astype(BF16) for h in range(2)]
    m_ref[...] = jnp.full(m_ref.shape, -jnp.inf, F32)
    acc_ref[...] = jnp.zeros(acc_ref.shape, F32)

    def scores(j):
        start = pl.multiple_of(j * tk, tk)
        return tuple(_dot_nt(ka_ref[h, pl.ds(start, tk), :], qa[h]) for h in range(2))

    def softmax(st_pair, first_key):
        out = []
        for h in range(2):
            st = st_pair[h]
            if first_key is not None:
                ki = lax.broadcasted_iota(I32, (tk, t), 0) + first_key
                qi = lax.broadcasted_iota(I32, (tk, t), 1)
                st = jnp.where(ki <= qi, st, -jnp.inf)
            m_prev = m_ref[h]
            m_new = jnp.maximum(m_prev, jnp.max(st, axis=0, keepdims=True))
            m_ref[h] = m_new
            out += [jnp.exp2(st - m_new).astype(BF16), jnp.exp2(m_prev - m_new)]
        return tuple(out)

    def add_values(j, weights):
        for h in range(2):
            p, rescale = weights[2 * h], weights[2 * h + 1]
            acc_ref[h] = acc_ref[h] * rescale + _dot(vt_ref[h, j], p)

    def body(j, st_pair):
        st_next = scores(j + 1)
        add_values(j, softmax(st_pair, None))
        return st_next

    per_tile = t // tk
    first = i * per_tile
    st_pair = lax.fori_loop(0, first, body, scores(0))
    for d in range(per_tile):
        st_next = scores(first + d + 1) if d + 1 < per_tile else None
        add_values(first + d, softmax(st_pair, d * tk))
        st_pair = st_next
    halves = []
    for h in range(2):
        acc = acc_ref[h]
        halves.append(acc[:HEAD_DIM] * (1.0 / acc[HEAD_DIM:HEAD_DIM + 1]))
    o_ref[...] = jnp.concatenate(halves, axis=0).T.astype(BF16)


def _attn_prompt(q, k, v, cs, *, nb, seq):
    t, tk = TOKEN_TILE, KEY_CHUNK
    nq, nk = seq // t, seq // tk
    return pl.pallas_call(
        functools.partial(_attn_prompt_kernel, t=t, tk=tk, nk=nk),
        out_shape=jax.ShapeDtypeStruct(q.shape, BF16),
        grid=(nb, HEADS // 2, nq),
        in_specs=[pl.BlockSpec((t, LANES), lambda b, hp, i: (b * nq + i, hp)),
                  pl.BlockSpec((seq, LANES), lambda b, hp, i: (b, hp)),
                  pl.BlockSpec((seq, LANES), lambda b, hp, i: (b, hp)),
                  pl.BlockSpec((seq, LANES), lambda b, hp, i: (b, 0))],
        out_specs=pl.BlockSpec((t, LANES), lambda b, hp, i: (b * nq + i, hp)),
        scratch_shapes=[pltpu.VMEM((2, seq, LANES), BF16), pltpu.VMEM((2, nk, LANES, tk), BF16),
                        pltpu.VMEM((2, 1, t), F32), pltpu.VMEM((2, LANES, t), F32)],
        compiler_params=_cparams("arbitrary", "arbitrary", "arbitrary"),
        name="attn_prompt",
    )(q, k, v, cs)


def _attn_sample_kernel(q_ref, kc_ref, vc_ref, kn_ref, vn_ref, c_ref, o_ref,
                        qbd_ref, m_ref, l_ref, acc_ref, kpad_ref, vpad_ref, *, nq, nchunk):
    j = pl.program_id(1)
    rows = HEADS * nq
    row_head = lax.broadcasted_iota(I32, (rows, FOX_DIM), 0) >> (nq.bit_length() - 1)
    col_head = lax.broadcasted_iota(I32, (rows, FOX_DIM), 1) >> (HEAD_DIM.bit_length() - 1)
    own = row_head == col_head

    @pl.when(j == 0)
    def _():
        qt = jnp.concatenate([q_ref[...]] * HEADS, axis=0)
        qbd_ref[...] = jnp.where(own, qt, jnp.zeros_like(qt))
        m_ref[...] = jnp.full(m_ref.shape, -jnp.inf, F32)
        l_ref[...] = jnp.zeros(l_ref.shape, F32)
        acc_ref[...] = jnp.zeros(acc_ref.shape, F32)

    def update(s, cvals, visible, weighted_values):
        width = s.shape[1]
        bias = jnp.concatenate([jnp.broadcast_to(cvals[h:h + 1, :], (nq, width)) for h in range(HEADS)], axis=0)
        s = s - LOG2E * bias
        if visible is not None:
            s = jnp.where(visible, s, -jnp.inf)
        m_prev = m_ref[...]
        m_new = jnp.maximum(m_prev, jnp.max(s, axis=-1, keepdims=True))
        a = jnp.exp2(m_prev - m_new)
        p = jnp.exp2(s - m_new)
        l_ref[...] = a * l_ref[...] + jnp.sum(p, axis=-1, keepdims=True)
        m_ref[...] = m_new
        acc_ref[...] = acc_ref[...] * a + weighted_values(p.astype(BF16))

    @pl.when(j < nchunk)
    def _():
        chunk = kc_ref.shape[3]
        kt = kc_ref[0].reshape(FOX_DIM, chunk).astype(BF16)
        vt = vc_ref[0].reshape(FOX_DIM, chunk).astype(BF16)
        update(_dot(qbd_ref[...], kt), c_ref[0, j], None, lambda p: _dot_nt(p, vt))

    @pl.when(j == nchunk)
    def _():
        kpad_ref[...] = jnp.zeros(kpad_ref.shape, BF16)
        vpad_ref[...] = jnp.zeros(vpad_ref.shape, BF16)
        kpad_ref[pl.ds(0, nq), :] = kn_ref[...]
        vpad_ref[pl.ds(0, nq), :] = vn_ref[...]
        ki = lax.broadcasted_iota(I32, (rows, LANES), 1)
        qi = lax.broadcasted_iota(I32, (rows, LANES), 0) & (nq - 1)
        update(_dot_nt(qbd_ref[...], kpad_ref[...]), c_ref[0, nchunk][:, :LANES], ki <= qi,
               lambda p: _dot(p, vpad_ref[...]))
        out = jnp.where(own, acc_ref[...] * (1.0 / l_ref[...]), 0.0)
        o = out[0:nq]
        for h in range(1, HEADS):
            o = o + out[h * nq:(h + 1) * nq]
        o_ref[...] = o.astype(BF16)


def _attn_sample(q, k_cache, v_cache, k_new, v_new, c, *, nq):
    nb, past = k_cache.shape[0], k_cache.shape[3]
    chunk = c.shape[-1]
    nchunk = past // chunk
    assert nq & (nq - 1) == 0 and nq <= LANES and past % chunk == 0 and c.shape[1] == nchunk + 1
    rows = HEADS * nq
    cache_spec = pl.BlockSpec((1, HEADS, HEAD_DIM, chunk), lambda b, j: (b, 0, 0, jnp.minimum(j, nchunk - 1)))
    new_spec = pl.BlockSpec((nq, FOX_DIM), lambda b, j: (b, 0))
    return pl.pallas_call(
        functools.partial(_attn_sample_kernel, nq=nq, nchunk=nchunk),
        out_shape=jax.ShapeDtypeStruct(q.shape, BF16),
        grid=(nb, nchunk + 1),
        in_specs=[new_spec, cache_spec, cache_spec, new_spec, new_spec,
                  pl.BlockSpec((1, nchunk + 1, HEADS, chunk), lambda b, j: (b, 0, 0, 0))],
        out_specs=new_spec,
        scratch_shapes=[pltpu.VMEM((rows, FOX_DIM), BF16), pltpu.VMEM((rows, 1), F32), pltpu.VMEM((rows, 1), F32),
                        pltpu.VMEM((rows, FOX_DIM), F32), pltpu.VMEM((LANES, FOX_DIM), BF16),
                        pltpu.VMEM((LANES, FOX_DIM), BF16)],
        compiler_params=_cparams("arbitrary", "arbitrary"),
        name="attn_sample",
    )(q, k_cache, v_cache, k_new, v_new, c)


def _pick(is_first, a_ref, b_ref):
    return jnp.where(is_first, a_ref[...], b_ref[...])


def _post_attn_kernel(xp_ref, xs_ref, op_ref, os_ref, sap_ref, sas_ref, mbp_ref, mbs_ref,
                      wpa_ref, wo_ref, g_ref, wrt_ref, br_ref,
                      h1_ref, xn_ref, pos_ref, gate_ref, cnt_ref, *, n_first, tm):
    is_p = pl.program_id(0) < n_first
    ya = _dot(_pick(is_p, op_ref, os_ref), wpa_ref[...])
    merged = _pick(is_p, sap_ref, sas_ref).astype(F32) * ya + _pick(is_p, mbp_ref, mbs_ref).astype(F32)
    h1 = _pick(is_p, xp_ref, xs_ref) + _dot(merged.astype(BF16), wo_ref[...])
    h1_ref[...] = h1
    xn = _rms(h1, g_ref[...]).astype(BF16)
    xn_ref[...] = xn

    lt = _dot_nt(wrt_ref[...], xn) + br_ref[...]
    eio = lax.broadcasted_iota(I32, (N_EXPERTS, tm), 0).astype(F32)
    vals, hots = [], []
    for _ in range(TOP_K):
        m = jnp.max(lt, axis=0, keepdims=True)
        idx = jnp.min(jnp.where(lt == m, eio, float(N_EXPERTS)), axis=0, keepdims=True)
        hot = eio == idx
        vals.append(m)
        hots.append(hot)
        lt = jnp.where(hot, -jnp.inf, lt)
    ex = [jnp.exp(v - vals[0]) for v in vals]
    den = ex[0] + ex[1] + ex[2] + ex[3]
    gate_ref[...] = jnp.concatenate([e / den for e in ex], axis=0)

    chosen = jnp.zeros((N_EXPERTS, tm), F32)
    for hot in hots:
        chosen = jnp.where(hot, 1.0, chosen)
    td = DISPATCH_TILE
    r = lax.broadcasted_iota(I32, (td, td), 0)
    c = lax.broadcasted_iota(I32, (td, td), 1)
    before = jnp.where(r < c, 1.0, 0.0).astype(BF16)
    er = lax.broadcasted_iota(I32, (N_EXPERTS, N_EXPERTS), 0)
    ec = lax.broadcasted_iota(I32, (N_EXPERTS, N_EXPERTS), 1)
    lower = jnp.where(ec < er, 1.0, 0.0).astype(BF16)
    for sub in range(tm // td):
        sl = slice(sub * td, (sub + 1) * td)
        ch = chosen[:, sl]
        rank = _dot(ch.astype(BF16), before)
        cnt = rank[:, td - 1:td] + ch[:, td - 1:td]
        units = jnp.floor((cnt + (ROW_ALIGN - 1)) * (1.0 / ROW_ALIGN))
        start = ROW_ALIGN * _dot(lower, jnp.broadcast_to(units, (N_EXPERTS, td)).astype(BF16))
        base = start + rank
        pos = [jnp.sum(jnp.where(hot[:, sl], base, 0.0), axis=0, keepdims=True) for hot in hots]
        pos_ref[:, sl] = jnp.concatenate(pos, axis=0).astype(I32)
        cnt_ref[sub] = jnp.broadcast_to(cnt, (N_EXPERTS, LANES)).astype(I32)


def _post_attn(xp, xs, op, os_, sap, sas, mbp, mbs, wpa, wo, g, wrt, br):
    n_p, d = xp.shape
    n_s = xs.shape[0]
    tm = TOKEN_TILE
    assert n_p % tm == 0 and n_s % tm == 0
    n_first, n_tiles = n_p // tm, (n_p + n_s) // tm
    n = n_p + n_s
    sub = tm // DISPATCH_TILE
    first = lambda i: (jnp.minimum(i, n_first - 1), 0)
    second = lambda i: (jnp.maximum(i - n_first, 0), 0)
    row = lambda i: (i, 0)
    col = lambda i: (0, i)

    def pair(width):
        return [pl.BlockSpec((tm, width), first), pl.BlockSpec((tm, width), second)]

    return pl.pallas_call(
        functools.partial(_post_attn_kernel, n_first=n_first, tm=tm),
        out_shape=[jax.ShapeDtypeStruct((n, d), F32), jax.ShapeDtypeStruct((n, d), BF16),
                   jax.ShapeDtypeStruct((TOP_K, n), I32), jax.ShapeDtypeStruct((TOP_K, n), F32),
                   jax.ShapeDtypeStruct((n // DISPATCH_TILE, N_EXPERTS, LANES), I32)],
        grid=(n_tiles,),
        in_specs=pair(d) + pair(FOX_DIM) + pair(d) + pair(d) + [_full(w.shape) for w in (wpa, wo, g, wrt, br)],
        out_specs=[pl.BlockSpec((tm, d), row), pl.BlockSpec((tm, d), row),
                   pl.BlockSpec((TOP_K, tm), col), pl.BlockSpec((TOP_K, tm), col),
                   pl.BlockSpec((sub, N_EXPERTS, LANES), lambda i: (i, 0, 0))],
        compiler_params=_cparams("arbitrary"),
        name="post_attn_router",
    )(xp, xs, op, os_, sap, sas, mbp, mbs, wpa, wo, g, wrt, br)


def _for_each_run(tile, len_ref, loc_ref, dst_ref, fn):
    def body(e, carry):
        idx = tile * N_EXPERTS + e
        n, loc, dst = len_ref[idx], loc_ref[idx], dst_ref[idx]
        for c in RUN_CHUNKS:
            off = n & ~(2 * c - 1)

            @pl.when((n & c) != 0)
            def _():
                fn(pl.multiple_of(loc + off, ROW_ALIGN), pl.multiple_of(dst + off, ROW_ALIGN), c)
        return carry

    for e in range(N_EXPERTS):
        body(e, 0)


TILE_CHUNKS = tuple(ROW_ALIGN << b for b in reversed(range((LOCAL_ROWS // ROW_ALIGN).bit_length())))


def _wait_tile_rows(tile, len_ref, loc_ref, wait_rows):
    last = tile * N_EXPERTS + N_EXPERTS - 1
    total = loc_ref[last] + len_ref[last]
    for c in TILE_CHUNKS:
        @pl.when((total & c) != 0)
        def _():
            wait_rows(c)


def _dispatch_kernel(len_ref, loc_ref, dst_ref, xn_ref, pos_ref, xs_hbm, xloc_ref, sem_ref, *, n_tiles):
    i = pl.program_id(0)
    slot = lax.rem(i, 2)

    def copy(s, loc, dst, rows):
        return pltpu.make_async_copy(xloc_ref.at[s, pl.ds(loc, rows), :], xs_hbm.at[pl.ds(dst, rows), :], sem_ref.at[s])

    def start_tile(tile, s):
        _for_each_run(tile, len_ref, loc_ref, dst_ref, lambda loc, dst, rows: copy(s, loc, dst, rows).start())

    def wait_tile(tile, s):
        _wait_tile_rows(tile, len_ref, loc_ref, lambda rows: copy(s, 0, 0, rows).wait())

    pos = pos_ref[...]
    aio = lax.broadcasted_iota(I32, (LOCAL_ROWS, DISPATCH_TILE), 0)
    hit = aio == pos[0:1, :]
    for k in range(1, TOP_K):
        hit = jnp.logical_or(hit, aio == pos[k:k + 1, :])
    sel = jnp.where(hit, 1.0, 0.0).astype(BF16)
    rows_sorted = _dot(sel, xn_ref[...])

    @pl.when(i >= 2)
    def _():
        wait_tile(i - 2, slot)

    xloc_ref[slot] = rows_sorted
    start_tile(i, slot)

    @pl.when(i == n_tiles - 1)
    def _():
        if n_tiles >= 2:
            wait_tile(i - 1, 1 - slot)
        wait_tile(i, slot)


def _dispatch(run_len, run_loc, run_dst, xn, pos, *, total_rows):
    n, d = xn.shape
    td = DISPATCH_TILE
    n_tiles = n // td
    return pl.pallas_call(
        functools.partial(_dispatch_kernel, n_tiles=n_tiles),
        out_shape=jax.ShapeDtypeStruct((total_rows, d), F32),
        grid_spec=pltpu.PrefetchScalarGridSpec(
            num_scalar_prefetch=3, grid=(n_tiles,),
            in_specs=[pl.BlockSpec((td, d), lambda i, *_: (i, 0)), pl.BlockSpec((TOP_K, td), lambda i, *_: (0, i))],
            out_specs=pl.BlockSpec(memory_space=pl.ANY),
            scratch_shapes=[pltpu.VMEM((2, LOCAL_ROWS, d), F32), pltpu.SemaphoreType.DMA((2,))]),
        compiler_params=_cparams("arbitrary"),
        name="moe_dispatch",
    )(run_len, run_loc, run_dst, xn, pos)


def _ffn_kernel(bexp_ref, nval_ref, last_ref, xs_ref, wgu_ref, bgu_ref, wdn_ref, bdn_ref, o_ref, wgu_s, wdn_s):
    i = pl.program_id(0)
    nv = nval_ref[i]

    @pl.when(jnp.logical_or(i == 0, bexp_ref[i] != bexp_ref[jnp.maximum(i - 1, 0)]))
    def _():
        wgu_s[...] = wgu_ref[0].astype(BF16)
        wdn_s[...] = wdn_ref[0].astype(BF16)

    @pl.when(nv > 0)
    def _():
        half = FFN_ROWS // 2
        for r in range(2):
            rows = lax.broadcasted_iota(I32, (half, xs_ref.shape[1]), 0) + r * half
            x = jnp.where(rows < nv, xs_ref[pl.ds(r * half, half), :], 0.0).astype(BF16)
            gu = _dot(x, wgu_s[...]) + bgu_ref[0]
            gt = jnp.minimum(gu[:, :D_FF], SWIGLU_LIMIT)
            up = jnp.clip(gu[:, D_FF:], -SWIGLU_LIMIT, SWIGLU_LIMIT)
            act = (up + 1.0) * gt * _sigmoid(SWIGLU_ALPHA * gt)
            o_ref[pl.ds(r * half, half), :] = _dot(act.astype(BF16), wdn_s[...]) + bdn_ref[0]


def _ffn(bexp, nval, last, xs, wgu, bgu, wdn, bdn):
    rows, d = xs.shape
    nblk = rows // FFN_ROWS
    blk = lambda i, bexp, nval, last: (jnp.minimum(i, last[0]), 0)
    exp3 = lambda i, bexp, nval, last: (bexp[i], 0, 0)
    return pl.pallas_call(
        _ffn_kernel,
        out_shape=jax.ShapeDtypeStruct((rows, d), F32),
        grid_spec=pltpu.PrefetchScalarGridSpec(
            num_scalar_prefetch=3, grid=(nblk,),
            in_specs=[pl.BlockSpec((FFN_ROWS, d), blk),
                      pl.BlockSpec((1, d, 2 * D_FF), exp3), pl.BlockSpec((1, 1, 2 * D_FF), exp3),
                      pl.BlockSpec((1, D_FF, d), exp3), pl.BlockSpec((1, 1, d), exp3)],
            out_specs=pl.BlockSpec((FFN_ROWS, d), blk),
            scratch_shapes=[pltpu.VMEM((d, 2 * D_FF), BF16), pltpu.VMEM((D_FF, d), BF16)]),
        compiler_params=_cparams("arbitrary"),
        name="moe_ffn",
    )(bexp, nval, last, xs, wgu, bgu, wdn, bdn)


def _combine_kernel(len_ref, loc_ref, dst_ref, os_hbm, pos_ref, gate_ref, h1_ref, pp_ref, ps_ref,
                    gple_ref, wpg_ref, wpp_ref, gfin_ref, yp_ref, ys_ref, oloc_ref, sem_ref, *, n_tiles, n_first):
    i = pl.program_id(0)
    slot = lax.rem(i, 2)

    def copy(s, loc, dst, rows):
        return pltpu.make_async_copy(os_hbm.at[pl.ds(dst, rows), :], oloc_ref.at[s, pl.ds(loc, rows), :], sem_ref.at[s])

    def start_tile(tile, s):
        _for_each_run(tile, len_ref, loc_ref, dst_ref, lambda loc, dst, rows: copy(s, loc, dst, rows).start())

    def wait_tile(tile, s):
        _wait_tile_rows(tile, len_ref, loc_ref, lambda rows: copy(s, 0, 0, rows).wait())

    @pl.when(i == 0)
    def _():
        oloc_ref[...] = jnp.zeros(oloc_ref.shape, F32)
        start_tile(0, 0)

    @pl.when(i + 1 < n_tiles)
    def _():
        start_tile(i + 1, 1 - slot)

    wait_tile(i, slot)

    pos, gate = pos_ref[...], gate_ref[...]
    lio = lax.broadcasted_iota(I32, (DISPATCH_TILE, LOCAL_ROWS), 1)
    w = jnp.zeros((DISPATCH_TILE, LOCAL_ROWS), F32)
    for k in range(TOP_K):
        w = jnp.where(lio == pos[:, k:k + 1], gate[:, k:k + 1], w)
    h2 = h1_ref[...] + _dot(w.astype(BF16), oloc_ref[slot].astype(BF16))

    is_p = i < n_first
    xn = _rms(h2, gple_ref[...]).astype(BF16)
    ple = _dot(_pick(is_p, pp_ref, ps_ref).astype(BF16), wpp_ref[...])
    h3 = h2 + _sigmoid(_dot(xn, wpg_ref[...])) * ple
    y = _rms(h3, gfin_ref[...])

    @pl.when(is_p)
    def _():
        yp_ref[...] = y

    @pl.when(jnp.logical_not(is_p))
    def _():
        ys_ref[...] = y


def _combine(run_len, run_loc, run_dst, os_, pos, gate, h1, pp, ps, gple, wpg, wpp, gfin):
    n, d = h1.shape
    td = DISPATCH_TILE
    n_p, n_s = pp.shape[0], ps.shape[0]
    assert n_p % td == 0 and n_s % td == 0
    n_tiles, n_first = n // td, n_p // td
    pw = pp.shape[1]
    first = lambda i, *_: (jnp.minimum(i, n_first - 1), 0)
    second = lambda i, *_: (jnp.maximum(i - n_first, 0), 0)
    row = lambda i, *_: (i, 0)
    full2 = lambda i, *_: (0, 0)
    return pl.pallas_call(
        functools.partial(_combine_kernel, n_tiles=n_tiles, n_first=n_first),
        out_shape=[jax.ShapeDtypeStruct((n_p, d), F32), jax.ShapeDtypeStruct((n_s, d), F32)],
        grid_spec=pltpu.PrefetchScalarGridSpec(
            num_scalar_prefetch=3, grid=(n_tiles,),
            in_specs=[pl.BlockSpec(memory_space=pl.ANY),
                      pl.BlockSpec((td, TOP_K), row), pl.BlockSpec((td, TOP_K), row), pl.BlockSpec((td, d), row),
                      pl.BlockSpec((td, pw), first), pl.BlockSpec((td, pw), second),
                      pl.BlockSpec(gple.shape, full2), pl.BlockSpec(wpg.shape, full2),
                      pl.BlockSpec(wpp.shape, full2), pl.BlockSpec(gfin.shape, full2)],
            out_specs=[pl.BlockSpec((td, d), first), pl.BlockSpec((td, d), second)],
            scratch_shapes=[pltpu.VMEM((2, LOCAL_ROWS, d), F32), pltpu.SemaphoreType.DMA((2,))]),
        compiler_params=_cparams("arbitrary"),
        name="moe_combine_ple",
    )(run_len, run_loc, run_dst, os_, pos, gate, h1, pp, ps, gple, wpg, wpp, gfin)


def _routing_tables(cnt, total_rows):
    n_tiles = cnt.shape[0]
    run_len = (cnt + (ROW_ALIGN - 1)) // ROW_ALIGN * ROW_ALIGN
    run_loc = jnp.cumsum(run_len, axis=1) - run_len
    per_expert = jnp.sum(run_len, axis=0)
    region = (per_expert + (FFN_ROWS - 1)) // FFN_ROWS * FFN_ROWS
    region_end = jnp.cumsum(region)
    region_start = region_end - region
    run_dst = region_start[None, :] + jnp.cumsum(run_len, axis=0) - run_len
    nblk = total_rows // FFN_ROWS
    blk_row = jnp.arange(nblk, dtype=I32) * FFN_ROWS
    used = region_end[-1] // FFN_ROWS
    last = jnp.maximum(used - 1, 0)
    bexp = jnp.minimum(jnp.sum(blk_row[:, None] >= region_end[None, :], axis=1), N_EXPERTS - 1).astype(I32)
    nval = jnp.clip(per_expert[bexp] - (blk_row - region_start[bexp]), 0, FFN_ROWS)
    nval = jnp.where(blk_row < region_end[-1], nval, 0).astype(I32)
    bexp = jnp.where(blk_row < region_end[-1], bexp, bexp[last])
    flat = lambda a: a.reshape(n_tiles * N_EXPERTS).astype(I32)
    return flat(run_len), flat(run_loc), flat(run_dst), bexp, nval, last.reshape(1).astype(I32)


def _chunked_t(lf, chunk):
    b, length, h = lf.shape
    return lf.reshape(b, length // chunk, chunk, h).transpose(0, 1, 3, 2)


def kernel(x_prompt, x_sample, p_prompt, p_sample, cache_k, cache_v, cache_logf, state_conv, g_mix, w_in, b_f, w_conv, w_pa, w_pb, w_o, g_ffn, w_router, b_router, w_gu, b_gu, w_dn, b_dn, g_ple, w_ple_gate, w_ple_proj, g_final):
    depth = g_mix.shape[0]
    assert depth == 1
    nb, seq, d = x_prompt.shape
    db, dseq, _ = x_sample.shape
    past = cache_k.shape[2]
    n_p, n_s = nb * seq, db * dseq

    w = w_in[0].astype(BF16)
    o_f = 3 * FOX_DIM
    o_c = o_f + HEADS
    o_g = o_c + 3 * CONV_DIM
    wf = jnp.pad(w[:, o_f:o_c], ((0, 0), (0, LANES - HEADS)))
    bf = jnp.pad(b_f[0], (0, LANES - HEADS)).reshape(1, LANES)
    mixer_w = (g_mix[0].reshape(1, d), w[:, :o_f], wf, bf, w[:, o_c:o_g], w_conv[0], w[:, o_g:], w_pb[0].astype(BF16))

    (q_p, k_p, v_p, ko_p, vo_p, logf_p, lfp_p, sa_p, mb_p), tail_p = _in_proj(
        x_prompt.reshape(n_p, d), jnp.zeros((nb, CONV_WIDTH - 1, CONV_DIM), F32), mixer_w, seq_len=seq)
    o_p = _attn_prompt(q_p, k_p, v_p, _cumsum_rows(lfp_p, seq=seq), nb=nb, seq=seq)

    (q_s, k_s, v_s, ko_s, vo_s, logf_s, _, sa_s, mb_s), tail_s = _in_proj(
        x_sample.reshape(n_s, d), state_conv[0], mixer_w, seq_len=dseq)
    chunk = min(past, 2 * TOKEN_TILE)
    lf_all = jnp.concatenate([cache_logf[0].astype(F32), logf_s.reshape(db, dseq, HEADS),
                              jnp.zeros((db, chunk - dseq, HEADS), F32)], axis=1)
    c_s = _cumsum_chunks(_chunked_t(lf_all, chunk))
    o_s = _attn_sample(q_s, cache_k[0].transpose(0, 2, 3, 1), cache_v[0].transpose(0, 2, 3, 1), k_s, v_s, c_s, nq=dseq)

    h1, xn2, pos_t, gate_t, cnt = _post_attn(
        x_prompt.reshape(n_p, d), x_sample.reshape(n_s, d), o_p, o_s, sa_p, sa_s, mb_p, mb_s,
        w_pa[0].astype(BF16), w_o[0].astype(BF16), g_ffn[0].reshape(1, d),
        w_router[0].T.astype(BF16), b_router[0].reshape(N_EXPERTS, 1))

    n = n_p + n_s
    n_tiles = n // DISPATCH_TILE
    max_rows = n * TOP_K + n_tiles * N_EXPERTS * (ROW_ALIGN - 1) + N_EXPERTS * (FFN_ROWS - 1)
    total_rows = -(-max_rows // FFN_ROWS) * FFN_ROWS
    run_len, run_loc, run_dst, bexp, nval, last = _routing_tables(cnt[:, :, 0], total_rows)
    xs = _dispatch(run_len, run_loc, run_dst, xn2, pos_t, total_rows=total_rows)
    os_ = _ffn(bexp, nval, last, xs, w_gu[0], b_gu[0].reshape(N_EXPERTS, 1, 2 * D_FF),
               w_dn[0], b_dn[0].reshape(N_EXPERTS, 1, d))

    y_p, y_s = _combine(run_len, run_loc, run_dst, os_, pos_t.T, gate_t.T, h1,
                        p_prompt[0].reshape(n_p, -1), p_sample[0].reshape(n_s, -1),
                        g_ple[0].reshape(1, d), w_ple_gate[0].astype(BF16), w_ple_proj[0].astype(BF16),
                        g_final.reshape(1, d))

    return (y_p.reshape(nb, seq, d), y_s.reshape(db, dseq, d),
            ko_p.transpose(0, 3, 1, 2)[None], vo_p.transpose(0, 3, 1, 2)[None],
            logf_p.reshape(1, nb, seq, HEADS), tail_p[None],
            ko_s.reshape(1, db, dseq, HEADS, HEAD_DIM), vo_s.reshape(1, db, dseq, HEADS, HEAD_DIM),
            logf_s.reshape(1, db, dseq, HEADS), tail_s[None])
```

```python
import functools

import jax
import jax.numpy as jnp
from jax import lax
from jax.experimental import pallas as pl
from jax.experimental.pallas import tpu as pltpu

F32, BF16, I32 = jnp.float32, jnp.bfloat16, jnp.int32

HEADS = 8
HEAD_DIM = 64
FOX_DIM = HEADS * HEAD_DIM
CONV_DIM = 512
CONV_WIDTH = 3
N_EXPERTS = 32
TOP_K = 4
D_FF = 1024
SWIGLU_ALPHA = 1.702
SWIGLU_LIMIT = 7.0
RMS_EPS = 1e-6
LOG2E = 1.4426950408889634

LANES = 128
SUBLANES = 8
TOKEN_TILE = 512
KEY_CHUNK = 512
DISPATCH_TILE = 256
ROW_ALIGN = SUBLANES
LOCAL_ROWS = DISPATCH_TILE * TOP_K + N_EXPERTS * ROW_ALIGN
FFN_ROWS = 512
RUN_CHUNKS = (256, 128, 64, 32, 16, 8)
VMEM_LIMIT_BYTES = 56 * 1024 * 1024


def _cparams(*sem):
    return pltpu.CompilerParams(dimension_semantics=sem, vmem_limit_bytes=VMEM_LIMIT_BYTES)


def _rms(x, g):
    return x * lax.rsqrt(jnp.mean(x * x, axis=-1, keepdims=True) + RMS_EPS) * g


def _sigmoid(x):
    return 1.0 / (1.0 + jnp.exp(-x))


def _log_sigmoid(x):
    return jnp.minimum(x, 0.0) - jnp.log1p(jnp.exp(-jnp.abs(x)))


def _dot(a, b):
    return jnp.dot(a, b, preferred_element_type=F32)


def _dot_nt(a, b):
    return lax.dot_general(a, b, (((1,), (1,)), ((), ())), preferred_element_type=F32)


def _mixer_inputs(x_ref, g_ref, wqkv_ref, wf_ref, bf_ref, wc_ref, wgl_ref,
                  q_ref, k_ref, v_ref, ko_ref, vo_ref, logf_ref, lfp_ref, sa_ref, *, keys_minor):
    xn = _rms(x_ref[...], g_ref[...]).astype(BF16)
    qkv = _dot(xn, wqkv_ref[...])
    q_ref[...] = (qkv[:, :FOX_DIM] * (HEAD_DIM ** -0.5 * LOG2E)).astype(BF16)
    k = qkv[:, FOX_DIM:2 * FOX_DIM]
    v = qkv[:, 2 * FOX_DIM:]
    k_ref[...] = k.astype(BF16)
    v_ref[...] = v.astype(BF16)
    if keys_minor:
        ko_ref[0] = k.T.reshape(HEADS, HEAD_DIM, k.shape[0])
        vo_ref[0] = v.T.reshape(HEADS, HEAD_DIM, v.shape[0])
    else:
        for h in range(HEADS):
            ko_ref[:, h, :] = k[:, h * HEAD_DIM:(h + 1) * HEAD_DIM]
            vo_ref[:, h, :] = v[:, h * HEAD_DIM:(h + 1) * HEAD_DIM]
    logf = _log_sigmoid(_dot(xn, wf_ref[...]) + bf_ref[...])
    logf_ref[...] = logf[:, :HEADS]
    lane = lax.broadcasted_iota(I32, logf.shape, 1)
    lfp_ref[...] = jnp.where(lane < HEADS, logf, 0.0)
    c3 = _dot(xn, wc_ref[...])
    gate_b = c3[:, :CONV_DIM]
    z = c3[:, CONV_DIM:2 * CONV_DIM] * c3[:, 2 * CONV_DIM:]
    gl = _dot(xn, wgl_ref[...])
    d = gl.shape[1] // 2
    sa_ref[...] = _sigmoid(gl[:, :d]).astype(BF16)
    return gate_b, z, _sigmoid(gl[:, d:])


def _conv_out(gate_b, z, z1, z2, sig_b, wconv_ref, wpb_ref, mb_ref):
    zc = wconv_ref[0:1, :] * z2 + wconv_ref[1:2, :] * z1 + wconv_ref[2:3, :] * z
    yb = _dot((gate_b * zc).astype(BF16), wpb_ref[...])
    mb_ref[...] = (sig_b * yb).astype(BF16)


N_MIXER_W = 8
N_MIXER_OUT = 9


def _mixer_front(x_ref, w_refs, out_refs, keys_minor):
    g_ref, wqkv_ref, wf_ref, bf_ref, wc_ref, wconv_ref, wgl_ref, wpb_ref = w_refs
    gate_b, z, sig_b = _mixer_inputs(x_ref, g_ref, wqkv_ref, wf_ref, bf_ref, wc_ref, wgl_ref, *out_refs[:-1],
                                     keys_minor=keys_minor)
    return gate_b, z, sig_b, wconv_ref, wpb_ref, out_refs[-1]


def _in_proj_seq_kernel(x_ref, prev_ref, *refs, tm):
    w_refs, out_refs = refs[:N_MIXER_W], refs[N_MIXER_W:N_MIXER_W + N_MIXER_OUT]
    tail_ref, zbuf_ref = refs[N_MIXER_W + N_MIXER_OUT:]
    gate_b, z, sig_b, wconv_ref, wpb_ref, mb_ref = _mixer_front(x_ref, w_refs, out_refs, True)
    zbuf_ref[pl.ds(SUBLANES, tm), :] = z

    @pl.when(pl.program_id(1) == 0)
    def _():
        zbuf_ref[pl.ds(SUBLANES - 2, 2), :] = prev_ref[0]

    z1 = zbuf_ref[pl.ds(SUBLANES - 1, tm), :]
    z2 = zbuf_ref[pl.ds(SUBLANES - 2, tm), :]
    _conv_out(gate_b, z, z1, z2, sig_b, wconv_ref, wpb_ref, mb_ref)
    tail = zbuf_ref[pl.ds(tm + SUBLANES - 2, 2), :]
    zbuf_ref[pl.ds(SUBLANES - 2, 2), :] = tail
    tail_ref[0] = tail


def _in_proj_multi_kernel(x_ref, ov1_ref, ov2_ref, *refs, tm, seq_len):
    w_refs, out_refs = refs[:N_MIXER_W], refs[N_MIXER_W:N_MIXER_W + N_MIXER_OUT]
    z_ref, zbuf_ref = refs[N_MIXER_W + N_MIXER_OUT:]
    gate_b, z, sig_b, wconv_ref, wpb_ref, mb_ref = _mixer_front(x_ref, w_refs, out_refs, False)
    z_ref[...] = z
    zbuf_ref[pl.ds(0, SUBLANES), :] = jnp.zeros((SUBLANES, CONV_DIM), F32)
    zbuf_ref[pl.ds(SUBLANES, tm), :] = z
    t = lax.broadcasted_iota(I32, (tm, CONV_DIM), 0) & (seq_len - 1)
    z1 = jnp.where(t == 0, ov1_ref[...], zbuf_ref[pl.ds(SUBLANES - 1, tm), :])
    z2 = jnp.where(t < 2, ov2_ref[...], zbuf_ref[pl.ds(SUBLANES - 2, tm), :])
    _conv_out(gate_b, z, z1, z2, sig_b, wconv_ref, wpb_ref, mb_ref)


def _full(shape):
    n = len(shape)
    return pl.BlockSpec(shape, lambda *_: (0,) * n)


def _in_proj(x, conv_prev, weights, *, seq_len):
    g, wqkv, wf, bf, wc, wconv, wgl, wpb = weights
    n, d = x.shape
    w_specs = [_full(w.shape) for w in (g, wqkv, wf, bf, wc, wconv, wgl, wpb)]
    shapes = (((FOX_DIM,), BF16), ((FOX_DIM,), BF16), ((FOX_DIM,), BF16), ((HEADS, HEAD_DIM), F32), ((HEADS, HEAD_DIM), F32),
              ((HEADS,), F32), ((LANES,), F32), ((d,), BF16), ((d,), BF16))
    assert len(weights) == N_MIXER_W and len(shapes) == N_MIXER_OUT
    out_shape = [jax.ShapeDtypeStruct((n,) + s, t) for s, t in shapes]
    n_common = N_MIXER_OUT
    tm = TOKEN_TILE

    def out_specs(tile_index):
        return [pl.BlockSpec((tm,) + s, lambda *a, k=len(s): (tile_index(*a),) + (0,) * k) for s, _ in shapes]

    if seq_len % tm == 0:
        nb, nj = n // seq_len, seq_len // tm
        row = lambda b, j: (b * nj + j, 0)
        specs = out_specs(lambda b, j: b * nj + j)
        for i in (3, 4):
            out_shape[i] = jax.ShapeDtypeStruct((nb, HEADS, HEAD_DIM, seq_len), F32)
            specs[i] = pl.BlockSpec((1, HEADS, HEAD_DIM, tm), lambda b, j: (b, 0, 0, j))
        outs = pl.pallas_call(
            functools.partial(_in_proj_seq_kernel, tm=tm),
            out_shape=out_shape + [jax.ShapeDtypeStruct((nb, CONV_WIDTH - 1, CONV_DIM), F32)],
            grid=(nb, nj),
            in_specs=[pl.BlockSpec((tm, d), row), pl.BlockSpec((1, CONV_WIDTH - 1, CONV_DIM), lambda b, j: (b, 0, 0))] + w_specs,
            out_specs=specs + [pl.BlockSpec((1, CONV_WIDTH - 1, CONV_DIM), lambda b, j: (b, 0, 0))],
            scratch_shapes=[pltpu.VMEM((tm + SUBLANES, CONV_DIM), F32)],
            compiler_params=_cparams("arbitrary", "arbitrary"),
            name="in_proj_seq",
        )(x, conv_prev, g, wqkv, wf, bf, wc, wconv, wgl, wpb)
        return outs[:n_common], outs[n_common]
    assert seq_len & (seq_len - 1) == 0 and seq_len >= CONV_WIDTH - 1
    assert n % tm == 0 and tm % seq_len == 0
    first = jnp.zeros((n // seq_len, seq_len, CONV_DIM), F32)
    ov1 = first.at[:, 0].set(conv_prev[:, 1]).reshape(n, CONV_DIM)
    ov2 = first.at[:, 0].set(conv_prev[:, 0]).at[:, 1].set(conv_prev[:, 1]).reshape(n, CONV_DIM)
    row = lambda i: (i, 0)
    outs = pl.pallas_call(
        functools.partial(_in_proj_multi_kernel, tm=tm, seq_len=seq_len),
        out_shape=out_shape + [jax.ShapeDtypeStruct((n, CONV_DIM), F32)],
        grid=(n // tm,),
        in_specs=[pl.BlockSpec((tm, d), row), pl.BlockSpec((tm, CONV_DIM), row), pl.BlockSpec((tm, CONV_DIM), row)] + w_specs,
        out_specs=out_specs(lambda i: i) + [pl.BlockSpec((tm, CONV_DIM), row)],
        scratch_shapes=[pltpu.VMEM((tm + SUBLANES, CONV_DIM), F32)],
        compiler_params=_cparams("arbitrary"),
        name="in_proj_multi",
    )(x, ov1, ov2, g, wqkv, wf, bf, wc, wconv, wgl, wpb)
    tail = outs[n_common].reshape(n // seq_len, seq_len, CONV_DIM)[:, seq_len - (CONV_WIDTH - 1):]
    return outs[:n_common], tail


def _cumsum_kernel(lf_ref, c_ref, *, chunk, nchunk):
    r = lax.broadcasted_iota(I32, (chunk, chunk), 0)
    c = lax.broadcasted_iota(I32, (chunk, chunk), 1)
    upper = jnp.where(r <= c, 1.0, 0.0).astype(BF16)
    carry = jnp.zeros((HEADS, 1), F32)
    for n in range(nchunk):
        a = lf_ref[0, n]
        hi = a.astype(BF16)
        r1 = a - hi.astype(F32)
        mid = r1.astype(BF16)
        lo = (r1 - mid.astype(F32)).astype(BF16)
        cs = _dot(hi, upper) + _dot(mid, upper) + _dot(lo, upper) + carry
        c_ref[0, n] = cs
        carry = cs[:, chunk - 1:chunk]


def _cumsum_chunks(lf):
    b, nchunk, _, chunk = lf.shape
    spec = pl.BlockSpec((1, nchunk, HEADS, chunk), lambda i: (i, 0, 0, 0))
    return pl.pallas_call(
        functools.partial(_cumsum_kernel, chunk=chunk, nchunk=nchunk),
        out_shape=jax.ShapeDtypeStruct(lf.shape, F32),
        grid=(b,), in_specs=[spec], out_specs=spec,
        compiler_params=_cparams("arbitrary"),
        name="logf_cumsum",
    )(lf)


C_TERMS = 3


def _cumsum_rows_kernel(lf_ref, cs_ref, *, chunk, nchunk):
    r = lax.broadcasted_iota(I32, (chunk, chunk), 0)
    c = lax.broadcasted_iota(I32, (chunk, chunk), 1)
    lower = jnp.where(c <= r, 1.0, 0.0).astype(BF16)
    carry = jnp.zeros((1, LANES), F32)
    for n in range(nchunk):
        rest = lf_ref[pl.ds(n * chunk, chunk), :]
        cs = carry
        for _ in range(C_TERMS):
            term = rest.astype(BF16)
            cs = cs + _dot(lower, term)
            rest = rest - term.astype(F32)
        carry = cs[chunk - 1:chunk, :]
        packed = jnp.zeros((chunk, LANES), F32)
        rest = cs * LOG2E
        for j in range(C_TERMS):
            term = rest.astype(BF16).astype(F32)
            packed = packed + (term if j == 0 else pltpu.roll(term, j * HEADS, 1))
            rest = rest - term
        cs_ref[pl.ds(n * chunk, chunk), :] = packed.astype(BF16)


def _cumsum_rows(lfp, *, seq):
    n = lfp.shape[0]
    spec = pl.BlockSpec((seq, LANES), lambda b: (b, 0))
    return pl.pallas_call(
        functools.partial(_cumsum_rows_kernel, chunk=TOKEN_TILE, nchunk=seq // TOKEN_TILE),
        out_shape=jax.ShapeDtypeStruct((n, LANES), BF16),
        grid=(n // seq,), in_specs=[spec], out_specs=spec,
        compiler_params=_cparams("arbitrary"),
        name="logf_cumsum_rows",
    )(lfp)


def _attn_prompt_kernel(q_ref, k_ref, v_ref, cs_ref, o_ref, ka_ref, vt_ref, m_ref, acc_ref, *, t, tk, nk):
    hp, i = pl.program_id(1), pl.program_id(2)
    rr = lax.broadcasted_iota(I32, (LANES, LANES), 0)
    cc = lax.broadcasted_iota(I32, (LANES, LANES), 1)
    sel = [jnp.where(jnp.where(cc < HEAD_DIM, rr - cc, -1) == HEAD_DIM * h, 1.0, 0.0).astype(BF16) for h in range(2)]

    @pl.when(i == 0)
    def _():
        kb = k_ref[...]
        cs = cs_ref[...]
        vt = v_ref[...].astype(F32).T
        row = lax.broadcasted_iota(I32, (SUBLANES, tk), 0)
        ones = jnp.where(row == 0, 1.0, 0.0)
        pad = jnp.zeros((LANES - HEAD_DIM - SUBLANES, tk), F32)
        for h in range(2):
            head = 2 * hp + h
            src = jnp.where(cc >= HEAD_DIM, (cc - HEAD_DIM) * HEADS + head, -1)
            src = jnp.where(cc < HEAD_DIM + C_TERMS, src, -1)
            place = jnp.where(rr == src, -1.0, 0.0).astype(BF16)
            ka_ref[h] = (_dot(kb, sel[h]) + _dot(cs, place)).astype(BF16)
            for n in range(nk):
                vh = vt[h * HEAD_DIM:(h + 1) * HEAD_DIM, n * tk:(n + 1) * tk]
                vt_ref[h, n] = jnp.concatenate([vh, ones, pad], axis=0).astype(BF16)

    lane = lax.broadcasted_iota(I32, (1, LANES), 1)
    one_lanes = jnp.where(jnp.logical_and(lane >= HEAD_DIM, lane < HEAD_DIM + C_TERMS), 1.0, 0.0)
    q = q_ref[...]
    qa = [(_dot(q, sel[h]) + one_lanes).astype(BF16) for h in range(2)]
    m_ref[...] = jnp.full(m_ref.shape, -jnp.inf, F32)
    acc_ref[...] = jnp.zeros(acc_ref.shape, F32)

    def scores(j):
        start = pl.multiple_of(j * tk, tk)
        return tuple(_dot_nt(ka_ref[h, pl.ds(start, tk), :], qa[h]) for h in range(2))

    assert tk == t

    def visible(st, limit):
        ki = lax.broadcasted_iota(I32, (tk, t), 0)
        qi = lax.broadcasted_iota(I32, (tk, t), 1)
        return jnp.where(ki <= qi + limit, st, -jnp.inf)

    def max_pass(j, limit):
        st_pair = scores(j)
        for h in range(2):
            st = st_pair[h] if limit is None else visible(st_pair[h], limit)
            m_ref[j + 1, h] = jnp.maximum(m_ref[j, h], jnp.max(st, axis=0, keepdims=True))

    def weight_pass(j, limit):
        st_pair = scores(j)
        for h in range(2):
            st = st_pair[h] if limit is None else visible(st_pair[h], limit)
            m_new = m_ref[j + 1, h]
            p = jnp.exp2(st - m_new).astype(BF16)
            acc_ref[h] = acc_ref[h] * jnp.exp2(m_ref[j, h] - m_new) + _dot(vt_ref[h, j], p)

    max_pass(0, i * t)

    def body(j, carry):
        max_pass(j + 1, None)
        weight_pass(j, None)
        return carry

    lax.fori_loop(0, i - 1, body, 0)

    @pl.when(i >= 1)
    def _():
        max_pass(i, 0)
        weight_pass(i - 1, None)

    weight_pass(i, 0)
    halves = []
    for h in range(2):
        acc = acc_ref[h]
        halves.append(acc[:HEAD_DIM] * (1.0 / acc[HEAD_DIM:HEAD_DIM + 1]))
    o_ref[...] = jnp.concatenate(halves, axis=0).T.astype(BF16)


def _attn_prompt(q, k, v, cs, *, nb, seq):
    t, tk = TOKEN_TILE, KEY_CHUNK
    nq, nk = seq // t, seq // tk
    return pl.pallas_call(
        functools.partial(_attn_prompt_kernel, t=t, tk=tk, nk=nk),
        out_shape=jax.ShapeDtypeStruct(q.shape, BF16),
        grid=(nb, HEADS // 2, nq),
        in_specs=[pl.BlockSpec((t, LANES), lambda b, hp, i: (b * nq + i, hp)),
                  pl.BlockSpec((seq, LANES), lambda b, hp, i: (b, hp)),
                  pl.BlockSpec((seq, LANES), lambda b, hp, i: (b, hp)),
                  pl.BlockSpec((seq, LANES), lambda b, hp, i: (b, 0))],
        out_specs=pl.BlockSpec((t, LANES), lambda b, hp, i: (b * nq + i, hp)),
        scratch_shapes=[pltpu.VMEM((2, seq, LANES), BF16), pltpu.VMEM((2, nk, LANES, tk), BF16),
                        pltpu.VMEM((nk + 1, 2, 1, t), F32), pltpu.VMEM((2, LANES, t), F32)],
        compiler_params=_cparams("arbitrary", "arbitrary", "arbitrary"),
        name="attn_prompt",
    )(q, k, v, cs)


def _attn_sample_kernel(q_ref, kc_ref, vc_ref, kn_ref, vn_ref, c_ref, o_ref,
                        qbd_ref, m_ref, l_ref, acc_ref, kpad_ref, vpad_ref, *, nq, nchunk):
    j = pl.program_id(1)
    rows = HEADS * nq
    row_head = lax.broadcasted_iota(I32, (rows, FOX_DIM), 0) >> (nq.bit_length() - 1)
    col_head = lax.broadcasted_iota(I32, (rows, FOX_DIM), 1) >> (HEAD_DIM.bit_length() - 1)
    own = row_head == col_head

    @pl.when(j == 0)
    def _():
        qt = jnp.concatenate([q_ref[...]] * HEADS, axis=0)
        qbd_ref[...] = jnp.where(own, qt, jnp.zeros_like(qt))
        m_ref[...] = jnp.full(m_ref.shape, -jnp.inf, F32)
        l_ref[...] = jnp.zeros(l_ref.shape, F32)
        acc_ref[...] = jnp.zeros(acc_ref.shape, F32)

    def update(s, cvals, visible, weighted_values):
        width = s.shape[1]
        bias = jnp.concatenate([jnp.broadcast_to(cvals[h:h + 1, :], (nq, width)) for h in range(HEADS)], axis=0)
        s = s - LOG2E * bias
        if visible is not None:
            s = jnp.where(visible, s, -jnp.inf)
        m_prev = m_ref[...]
        m_new = jnp.maximum(m_prev, jnp.max(s, axis=-1, keepdims=True))
        a = jnp.exp2(m_prev - m_new)
        p = jnp.exp2(s - m_new)
        l_ref[...] = a * l_ref[...] + jnp.sum(p, axis=-1, keepdims=True)
        m_ref[...] = m_new
        acc_ref[...] = acc_ref[...] * a + weighted_values(p.astype(BF16))

    @pl.when(j < nchunk)
    def _():
        chunk = kc_ref.shape[3]
        kt = kc_ref[0].reshape(FOX_DIM, chunk).astype(BF16)
        vt = vc_ref[0].reshape(FOX_DIM, chunk).astype(BF16)
        update(_dot(qbd_ref[...], kt), c_ref[0, j], None, lambda p: _dot_nt(p, vt))

    @pl.when(j == nchunk)
    def _():
        kpad_ref[...] = jnp.zeros(kpad_ref.shape, BF16)
        vpad_ref[...] = jnp.zeros(vpad_ref.shape, BF16)
        kpad_ref[pl.ds(0, nq), :] = kn_ref[...]
        vpad_ref[pl.ds(0, nq), :] = vn_ref[...]
        ki = lax.broadcasted_iota(I32, (rows, LANES), 1)
        qi = lax.broadcasted_iota(I32, (rows, LANES), 0) & (nq - 1)
        update(_dot_nt(qbd_ref[...], kpad_ref[...]), c_ref[0, nchunk][:, :LANES], ki <= qi,
               lambda p: _dot(p, vpad_ref[...]))
        out = jnp.where(own, acc_ref[...] * (1.0 / l_ref[...]), 0.0)
        o = out[0:nq]
        for h in range(1, HEADS):
            o = o + out[h * nq:(h + 1) * nq]
        o_ref[...] = o.astype(BF16)


def _attn_sample(q, k_cache, v_cache, k_new, v_new, c, *, nq):
    nb, past = k_cache.shape[0], k_cache.shape[3]
    chunk = c.shape[-1]
    nchunk = past // chunk
    assert nq & (nq - 1) == 0 and nq <= LANES and past % chunk == 0 and c.shape[1] == nchunk + 1
    rows = HEADS * nq
    cache_spec = pl.BlockSpec((1, HEADS, HEAD_DIM, chunk), lambda b, j: (b, 0, 0, jnp.minimum(j, nchunk - 1)))
    new_spec = pl.BlockSpec((nq, FOX_DIM), lambda b, j: (b, 0))
    return pl.pallas_call(
        functools.partial(_attn_sample_kernel, nq=nq, nchunk=nchunk),
        out_shape=jax.ShapeDtypeStruct(q.shape, BF16),
        grid=(nb, nchunk + 1),
        in_specs=[new_spec, cache_spec, cache_spec, new_spec, new_spec,
                  pl.BlockSpec((1, nchunk + 1, HEADS, chunk), lambda b, j: (b, 0, 0, 0))],
        out_specs=new_spec,
        scratch_shapes=[pltpu.VMEM((rows, FOX_DIM), BF16), pltpu.VMEM((rows, 1), F32), pltpu.VMEM((rows, 1), F32),
                        pltpu.VMEM((rows, FOX_DIM), F32), pltpu.VMEM((LANES, FOX_DIM), BF16),
                        pltpu.VMEM((LANES, FOX_DIM), BF16)],
        compiler_params=_cparams("arbitrary", "arbitrary"),
        name="attn_sample",
    )(q, k_cache, v_cache, k_new, v_new, c)


def _pick(is_first, a_ref, b_ref):
    return jnp.where(is_first, a_ref[...], b_ref[...])


def _post_attn_kernel(xp_ref, xs_ref, op_ref, os_ref, sap_ref, sas_ref, mbp_ref, mbs_ref,
                      wpa_ref, wo_ref, g_ref, wrt_ref, br_ref,
                      h1_ref, xn_ref, pos_ref, gate_ref, cnt_ref, *, n_first, tm):
    is_p = pl.program_id(0) < n_first
    ya = _dot(_pick(is_p, op_ref, os_ref), wpa_ref[...])
    merged = _pick(is_p, sap_ref, sas_ref).astype(F32) * ya + _pick(is_p, mbp_ref, mbs_ref).astype(F32)
    h1 = _pick(is_p, xp_ref, xs_ref) + _dot(merged.astype(BF16), wo_ref[...])
    h1_ref[...] = h1
    xn = _rms(h1, g_ref[...]).astype(BF16)
    xn_ref[...] = xn

    lt = _dot_nt(wrt_ref[...], xn) + br_ref[...]
    eio = lax.broadcasted_iota(I32, (N_EXPERTS, tm), 0).astype(F32)
    vals, hots = [], []
    for _ in range(TOP_K):
        m = jnp.max(lt, axis=0, keepdims=True)
        idx = jnp.min(jnp.where(lt == m, eio, float(N_EXPERTS)), axis=0, keepdims=True)
        hot = eio == idx
        vals.append(m)
        hots.append(hot)
        lt = jnp.where(hot, -jnp.inf, lt)
    ex = [jnp.exp(v - vals[0]) for v in vals]
    den = ex[0] + ex[1] + ex[2] + ex[3]
    gate_ref[...] = jnp.concatenate([e / den for e in ex], axis=0)

    chosen = jnp.zeros((N_EXPERTS, tm), F32)
    for hot in hots:
        chosen = jnp.where(hot, 1.0, chosen)
    td = DISPATCH_TILE
    r = lax.broadcasted_iota(I32, (td, td), 0)
    c = lax.broadcasted_iota(I32, (td, td), 1)
    before = jnp.where(r < c, 1.0, 0.0).astype(BF16)
    er = lax.broadcasted_iota(I32, (N_EXPERTS, N_EXPERTS), 0)
    ec = lax.broadcasted_iota(I32, (N_EXPERTS, N_EXPERTS), 1)
    lower = jnp.where(ec < er, 1.0, 0.0).astype(BF16)
    for sub in range(tm // td):
        sl = slice(sub * td, (sub + 1) * td)
        ch = chosen[:, sl]
        rank = _dot(ch.astype(BF16), before)
        cnt = rank[:, td - 1:td] + ch[:, td - 1:td]
        units = jnp.floor((cnt + (ROW_ALIGN - 1)) * (1.0 / ROW_ALIGN))
        start = ROW_ALIGN * _dot(lower, jnp.broadcast_to(units, (N_EXPERTS, td)).astype(BF16))
        base = start + rank
        pos = [jnp.sum(jnp.where(hot[:, sl], base, 0.0), axis=0, keepdims=True) for hot in hots]
        pos_ref[:, sl] = jnp.concatenate(pos, axis=0).astype(I32)
        cnt_ref[sub] = jnp.broadcast_to(cnt, (N_EXPERTS, LANES)).astype(I32)


def _post_attn(xp, xs, op, os_, sap, sas, mbp, mbs, wpa, wo, g, wrt, br):
    n_p, d = xp.shape
    n_s = xs.shape[0]
    tm = TOKEN_TILE
    assert n_p % tm == 0 and n_s % tm == 0
    n_first, n_tiles = n_p // tm, (n_p + n_s) // tm
    n = n_p + n_s
    sub = tm // DISPATCH_TILE
    first = lambda i: (jnp.minimum(i, n_first - 1), 0)
    second = lambda i: (jnp.maximum(i - n_first, 0), 0)
    row = lambda i: (i, 0)
    col = lambda i: (0, i)

    def pair(width):
        return [pl.BlockSpec((tm, width), first), pl.BlockSpec((tm, width), second)]

    return pl.pallas_call(
        functools.partial(_post_attn_kernel, n_first=n_first, tm=tm),
        out_shape=[jax.ShapeDtypeStruct((n, d), F32), jax.ShapeDtypeStruct((n, d), BF16),
                   jax.ShapeDtypeStruct((TOP_K, n), I32), jax.ShapeDtypeStruct((TOP_K, n), F32),
                   jax.ShapeDtypeStruct((n // DISPATCH_TILE, N_EXPERTS, LANES), I32)],
        grid=(n_tiles,),
        in_specs=pair(d) + pair(FOX_DIM) + pair(d) + pair(d) + [_full(w.shape) for w in (wpa, wo, g, wrt, br)],
        out_specs=[pl.BlockSpec((tm, d), row), pl.BlockSpec((tm, d), row),
                   pl.BlockSpec((TOP_K, tm), col), pl.BlockSpec((TOP_K, tm), col),
                   pl.BlockSpec((sub, N_EXPERTS, LANES), lambda i: (i, 0, 0))],
        compiler_params=_cparams("arbitrary"),
        name="post_attn_router",
    )(xp, xs, op, os_, sap, sas, mbp, mbs, wpa, wo, g, wrt, br)


def _for_each_run(tile, len_ref, loc_ref, dst_ref, fn):
    def body(e, carry):
        idx = tile * N_EXPERTS + e
        n, loc, dst = len_ref[idx], loc_ref[idx], dst_ref[idx]

        for c in RUN_CHUNKS:
            off = n & ~(2 * c - 1)

            @pl.when((n & c) != 0)
            def _():
                fn(pl.multiple_of(loc + off, ROW_ALIGN), pl.multiple_of(dst + off, ROW_ALIGN), c)
        return carry

    for e in range(N_EXPERTS):
        body(e, 0)


TILE_CHUNKS = tuple(ROW_ALIGN << b for b in reversed(range((LOCAL_ROWS // ROW_ALIGN).bit_length())))


def _wait_tile_rows(tile, len_ref, loc_ref, wait_rows):
    last = tile * N_EXPERTS + N_EXPERTS - 1
    total = loc_ref[last] + len_ref[last]
    for c in TILE_CHUNKS:
        @pl.when((total & c) != 0)
        def _():
            wait_rows(c)


def _dispatch_kernel(len_ref, loc_ref, dst_ref, xn_ref, pos_ref, xs_hbm, xloc_ref, sem_ref, *, n_tiles):
    i = pl.program_id(0)
    slot = lax.rem(i, 2)

    def copy(s, loc, dst, rows):
        return pltpu.make_async_copy(xloc_ref.at[s, pl.ds(loc, rows), :], xs_hbm.at[pl.ds(dst, rows), :], sem_ref.at[s])

    def start_tile(tile, s):
        _for_each_run(tile, len_ref, loc_ref, dst_ref, lambda loc, dst, rows: copy(s, loc, dst, rows).start())

    def wait_tile(tile, s):
        _wait_tile_rows(tile, len_ref, loc_ref, lambda rows: copy(s, 0, 0, rows).wait())

    pos = pos_ref[...]
    aio = lax.broadcasted_iota(I32, (LOCAL_ROWS, DISPATCH_TILE), 0)
    hit = aio == pos[0:1, :]
    for k in range(1, TOP_K):
        hit = jnp.logical_or(hit, aio == pos[k:k + 1, :])
    sel = jnp.where(hit, 1.0, 0.0).astype(BF16)
    rows_sorted = _dot(sel, xn_ref[...])

    @pl.when(i >= 2)
    def _():
        wait_tile(i - 2, slot)

    xloc_ref[slot] = rows_sorted
    start_tile(i, slot)

    @pl.when(i == n_tiles - 1)
    def _():
        if n_tiles >= 2:
            wait_tile(i - 1, 1 - slot)
        wait_tile(i, slot)


def _dispatch(run_len, run_loc, run_dst, xn, pos, *, total_rows):
    n, d = xn.shape
    td = DISPATCH_TILE
    n_tiles = n // td
    return pl.pallas_call(
        functools.partial(_dispatch_kernel, n_tiles=n_tiles),
        out_shape=jax.ShapeDtypeStruct((total_rows, d), F32),
        grid_spec=pltpu.PrefetchScalarGridSpec(
            num_scalar_prefetch=3, grid=(n_tiles,),
            in_specs=[pl.BlockSpec((td, d), lambda i, *_: (i, 0)), pl.BlockSpec((TOP_K, td), lambda i, *_: (0, i))],
            out_specs=pl.BlockSpec(memory_space=pl.ANY),
            scratch_shapes=[pltpu.VMEM((2, LOCAL_ROWS, d), F32), pltpu.SemaphoreType.DMA((2,))]),
        compiler_params=_cparams("arbitrary"),
        name="moe_dispatch",
    )(run_len, run_loc, run_dst, xn, pos)


def _ffn_kernel(bexp_ref, nval_ref, last_ref, xs_ref, wgu_ref, bgu_ref, wdn_ref, bdn_ref, o_ref, wgu_s, wdn_s):
    i = pl.program_id(0)
    nv = nval_ref[i]

    @pl.when(jnp.logical_or(i == 0, bexp_ref[i] != bexp_ref[jnp.maximum(i - 1, 0)]))
    def _():
        wgu_s[...] = wgu_ref[0].astype(BF16)
        wdn_s[...] = wdn_ref[0].astype(BF16)

    @pl.when(nv > 0)
    def _():
        half = FFN_ROWS // 2
        for r in range(2):
            rows = lax.broadcasted_iota(I32, (half, xs_ref.shape[1]), 0) + r * half
            x = jnp.where(rows < nv, xs_ref[pl.ds(r * half, half), :], 0.0).astype(BF16)
            gu = _dot(x, wgu_s[...]) + bgu_ref[0]
            gt = jnp.minimum(gu[:, :D_FF], SWIGLU_LIMIT)
            up = jnp.clip(gu[:, D_FF:], -SWIGLU_LIMIT, SWIGLU_LIMIT)
            act = (up + 1.0) * gt * _sigmoid(SWIGLU_ALPHA * gt)
            o_ref[pl.ds(r * half, half), :] = _dot(act.astype(BF16), wdn_s[...]) + bdn_ref[0]


def _ffn(bexp, nval, last, xs, wgu, bgu, wdn, bdn):
    rows, d = xs.shape
    nblk = rows // FFN_ROWS
    blk = lambda i, bexp, nval, last: (jnp.minimum(i, last[0]), 0)
    exp3 = lambda i, bexp, nval, last: (bexp[i], 0, 0)
    return pl.pallas_call(
        _ffn_kernel,
        out_shape=jax.ShapeDtypeStruct((rows, d), F32),
        grid_spec=pltpu.PrefetchScalarGridSpec(
            num_scalar_prefetch=3, grid=(nblk,),
            in_specs=[pl.BlockSpec((FFN_ROWS, d), blk),
                      pl.BlockSpec((1, d, 2 * D_FF), exp3), pl.BlockSpec((1, 1, 2 * D_FF), exp3),
                      pl.BlockSpec((1, D_FF, d), exp3), pl.BlockSpec((1, 1, d), exp3)],
            out_specs=pl.BlockSpec((FFN_ROWS, d), blk),
            scratch_shapes=[pltpu.VMEM((d, 2 * D_FF), BF16), pltpu.VMEM((D_FF, d), BF16)]),
        compiler_params=_cparams("arbitrary"),
        name="moe_ffn",
    )(bexp, nval, last, xs, wgu, bgu, wdn, bdn)


def _combine_kernel(len_ref, loc_ref, dst_ref, os_hbm, pos_ref, gate_ref, h1_ref, pp_ref, ps_ref,
                    gple_ref, wpg_ref, wpp_ref, gfin_ref, yp_ref, ys_ref, oloc_ref, sem_ref, *, n_tiles, n_first):
    i = pl.program_id(0)
    slot = lax.rem(i, 2)

    def copy(s, loc, dst, rows):
        return pltpu.make_async_copy(os_hbm.at[pl.ds(dst, rows), :], oloc_ref.at[s, pl.ds(loc, rows), :], sem_ref.at[s])

    def start_tile(tile, s):
        _for_each_run(tile, len_ref, loc_ref, dst_ref, lambda loc, dst, rows: copy(s, loc, dst, rows).start())

    def wait_tile(tile, s):
        _wait_tile_rows(tile, len_ref, loc_ref, lambda rows: copy(s, 0, 0, rows).wait())

    @pl.when(i == 0)
    def _():
        oloc_ref[...] = jnp.zeros(oloc_ref.shape, F32)
        start_tile(0, 0)

    @pl.when(i + 1 < n_tiles)
    def _():
        start_tile(i + 1, 1 - slot)

    wait_tile(i, slot)

    pos, gate = pos_ref[...], gate_ref[...]
    lio = lax.broadcasted_iota(I32, (DISPATCH_TILE, LOCAL_ROWS), 1)
    w = jnp.zeros((DISPATCH_TILE, LOCAL_ROWS), F32)
    for k in range(TOP_K):
        w = jnp.where(lio == pos[:, k:k + 1], gate[:, k:k + 1], w)
    h2 = h1_ref[...] + _dot(w.astype(BF16), oloc_ref[slot].astype(BF16))

    is_p = i < n_first
    xn = _rms(h2, gple_ref[...]).astype(BF16)
    ple = _dot(_pick(is_p, pp_ref, ps_ref).astype(BF16), wpp_ref[...])
    h3 = h2 + _sigmoid(_dot(xn, wpg_ref[...])) * ple
    y = _rms(h3, gfin_ref[...])

    @pl.when(is_p)
    def _():
        yp_ref[...] = y

    @pl.when(jnp.logical_not(is_p))
    def _():
        ys_ref[...] = y


def _combine(run_len, run_loc, run_dst, os_, pos, gate, h1, pp, ps, gple, wpg, wpp, gfin):
    n, d = h1.shape
    td = DISPATCH_TILE
    n_p, n_s = pp.shape[0], ps.shape[0]
    assert n_p % td == 0 and n_s % td == 0
    n_tiles, n_first = n // td, n_p // td
    pw = pp.shape[1]
    first = lambda i, *_: (jnp.minimum(i, n_first - 1), 0)
    second = lambda i, *_: (jnp.maximum(i - n_first, 0), 0)
    row = lambda i, *_: (i, 0)
    full2 = lambda i, *_: (0, 0)
    return pl.pallas_call(
        functools.partial(_combine_kernel, n_tiles=n_tiles, n_first=n_first),
        out_shape=[jax.ShapeDtypeStruct((n_p, d), F32), jax.ShapeDtypeStruct((n_s, d), F32)],
        grid_spec=pltpu.PrefetchScalarGridSpec(
            num_scalar_prefetch=3, grid=(n_tiles,),
            in_specs=[pl.BlockSpec(memory_space=pl.ANY),
                      pl.BlockSpec((td, TOP_K), row), pl.BlockSpec((td, TOP_K), row), pl.BlockSpec((td, d), row),
                      pl.BlockSpec((td, pw), first), pl.BlockSpec((td, pw), second),
                      pl.BlockSpec(gple.shape, full2), pl.BlockSpec(wpg.shape, full2),
                      pl.BlockSpec(wpp.shape, full2), pl.BlockSpec(gfin.shape, full2)],
            out_specs=[pl.BlockSpec((td, d), first), pl.BlockSpec((td, d), second)],
            scratch_shapes=[pltpu.VMEM((2, LOCAL_ROWS, d), F32), pltpu.SemaphoreType.DMA((2,))]),
        compiler_params=_cparams("arbitrary"),
        name="moe_combine_ple",
    )(run_len, run_loc, run_dst, os_, pos, gate, h1, pp, ps, gple, wpg, wpp, gfin)


def _routing_tables(cnt, total_rows):
    n_tiles = cnt.shape[0]
    run_len = (cnt + (ROW_ALIGN - 1)) // ROW_ALIGN * ROW_ALIGN
    run_loc = jnp.cumsum(run_len, axis=1) - run_len
    per_expert = jnp.sum(run_len, axis=0)
    region = (per_expert + (FFN_ROWS - 1)) // FFN_ROWS * FFN_ROWS
    region_end = jnp.cumsum(region)
    region_start = region_end - region
    run_dst = region_start[None, :] + jnp.cumsum(run_len, axis=0) - run_len
    nblk = total_rows // FFN_ROWS
    blk_row = jnp.arange(nblk, dtype=I32) * FFN_ROWS
    used = region_end[-1] // FFN_ROWS
    last = jnp.maximum(used - 1, 0)
    bexp = jnp.minimum(jnp.sum(blk_row[:, None] >= region_end[None, :], axis=1), N_EXPERTS - 1).astype(I32)
    nval = jnp.clip(per_expert[bexp] - (blk_row - region_start[bexp]), 0, FFN_ROWS)
    nval = jnp.where(blk_row < region_end[-1], nval, 0).astype(I32)
    bexp = jnp.where(blk_row < region_end[-1], bexp, bexp[last])
    flat = lambda a: a.reshape(n_tiles * N_EXPERTS).astype(I32)
    return flat(run_len), flat(run_loc), flat(run_dst), bexp, nval, last.reshape(1).astype(I32)


def _chunked_t(lf, chunk):
    b, length, h = lf.shape
    return lf.reshape(b, length // chunk, chunk, h).transpose(0, 1, 3, 2)


def kernel(x_prompt, x_sample, p_prompt, p_sample, cache_k, cache_v, cache_logf, state_conv, g_mix, w_in, b_f, w_conv, w_pa, w_pb, w_o, g_ffn, w_router, b_router, w_gu, b_gu, w_dn, b_dn, g_ple, w_ple_gate, w_ple_proj, g_final):
    depth = g_mix.shape[0]
    assert depth == 1
    nb, seq, d = x_prompt.shape
    db, dseq, _ = x_sample.shape
    past = cache_k.shape[2]
    n_p, n_s = nb * seq, db * dseq

    w = w_in[0].astype(BF16)
    o_f = 3 * FOX_DIM
    o_c = o_f + HEADS
    o_g = o_c + 3 * CONV_DIM
    wf = jnp.pad(w[:, o_f:o_c], ((0, 0), (0, LANES - HEADS)))
    bf = jnp.pad(b_f[0], (0, LANES - HEADS)).reshape(1, LANES)
    mixer_w = (g_mix[0].reshape(1, d), w[:, :o_f], wf, bf, w[:, o_c:o_g], w_conv[0], w[:, o_g:], w_pb[0].astype(BF16))

    (q_p, k_p, v_p, ko_p, vo_p, logf_p, lfp_p, sa_p, mb_p), tail_p = _in_proj(
        x_prompt.reshape(n_p, d), jnp.zeros((nb, CONV_WIDTH - 1, CONV_DIM), F32), mixer_w, seq_len=seq)
    o_p = _attn_prompt(q_p, k_p, v_p, _cumsum_rows(lfp_p, seq=seq), nb=nb, seq=seq)

    (q_s, k_s, v_s, ko_s, vo_s, logf_s, _, sa_s, mb_s), tail_s = _in_proj(
        x_sample.reshape(n_s, d), state_conv[0], mixer_w, seq_len=dseq)
    chunk = min(past, 2 * TOKEN_TILE)
    lf_all = jnp.concatenate([cache_logf[0].astype(F32), logf_s.reshape(db, dseq, HEADS),
                              jnp.zeros((db, chunk - dseq, HEADS), F32)], axis=1)
    c_s = _cumsum_chunks(_chunked_t(lf_all, chunk))
    o_s = _attn_sample(q_s, cache_k[0].transpose(0, 2, 3, 1), cache_v[0].transpose(0, 2, 3, 1), k_s, v_s, c_s, nq=dseq)

    h1, xn2, pos_t, gate_t, cnt = _post_attn(
        x_prompt.reshape(n_p, d), x_sample.reshape(n_s, d), o_p, o_s, sa_p, sa_s, mb_p, mb_s,
        w_pa[0].astype(BF16), w_o[0].astype(BF16), g_ffn[0].reshape(1, d),
        w_router[0].T.astype(BF16), b_router[0].reshape(N_EXPERTS, 1))

    n = n_p + n_s
    n_tiles = n // DISPATCH_TILE
    max_rows = n * TOP_K + n_tiles * N_EXPERTS * (ROW_ALIGN - 1) + N_EXPERTS * (FFN_ROWS - 1)
    total_rows = -(-max_rows // FFN_ROWS) * FFN_ROWS
    run_len, run_loc, run_dst, bexp, nval, last = _routing_tables(cnt[:, :, 0], total_rows)
    xs = _dispatch(run_len, run_loc, run_dst, xn2, pos_t, total_rows=total_rows)
    os_ = _ffn(bexp, nval, last, xs, w_gu[0], b_gu[0].reshape(N_EXPERTS, 1, 2 * D_FF),
               w_dn[0], b_dn[0].reshape(N_EXPERTS, 1, d))

    y_p, y_s = _combine(run_len, run_loc, run_dst, os_, pos_t.T, gate_t.T, h1,
                        p_prompt[0].reshape(n_p, -1), p_sample[0].reshape(n_s, -1),
                        g_ple[0].reshape(1, d), w_ple_gate[0].astype(BF16), w_ple_proj[0].astype(BF16),
                        g_final.reshape(1, d))

    return (y_p.reshape(nb, seq, d), y_s.reshape(db, dseq, d),
            ko_p.transpose(0, 3, 1, 2)[None], vo_p.transpose(0, 3, 1, 2)[None],
            logf_p.reshape(1, nb, seq, HEADS), tail_p[None],
            ko_s.reshape(1, db, dseq, HEADS, HEAD_DIM), vo_s.reshape(1, db, dseq, HEADS, HEAD_DIM),
            logf_s.reshape(1, db, dseq, HEADS), tail_s[None])
```

```python
import functools

import jax
import jax.numpy as jnp
from jax import lax
from jax.experimental import pallas as pl
from jax.experimental.pallas import tpu as pltpu

F32, BF16, I32 = jnp.float32, jnp.bfloat16, jnp.int32

HEADS = 8
HEAD_DIM = 64
FOX_DIM = HEADS * HEAD_DIM
CONV_DIM = 512
CONV_WIDTH = 3
N_EXPERTS = 32
TOP_K = 4
D_FF = 1024
SWIGLU_ALPHA = 1.702
SWIGLU_LIMIT = 7.0
RMS_EPS = 1e-6
LOG2E = 1.4426950408889634

LANES = 128
SUBLANES = 8
TOKEN_TILE = 512
KEY_CHUNK = 512
DISPATCH_TILE = 256
ROW_ALIGN = SUBLANES
LOCAL_ROWS = DISPATCH_TILE * TOP_K + N_EXPERTS * ROW_ALIGN
FFN_ROWS = 1024
FFN_SUB = 512
RUN_CHUNKS = (256, 128, 64, 32, 16, 8)
VMEM_LIMIT_BYTES = 56 * 1024 * 1024


def _cparams(*sem):
    return pltpu.CompilerParams(dimension_semantics=sem, vmem_limit_bytes=VMEM_LIMIT_BYTES)


def _rms(x, g):
    return x * lax.rsqrt(jnp.mean(x * x, axis=-1, keepdims=True) + RMS_EPS) * g


def _sigmoid(x):
    return 1.0 / (1.0 + jnp.exp(-x))


def _log_sigmoid(x):
    return jnp.minimum(x, 0.0) - jnp.log1p(jnp.exp(-jnp.abs(x)))


def _dot(a, b):
    return jnp.dot(a, b, preferred_element_type=F32)


def _dot_nt(a, b):
    return lax.dot_general(a, b, (((1,), (1,)), ((), ())), preferred_element_type=F32)


def _mixer_inputs(x_ref, g_ref, wqkv_ref, wf_ref, bf_ref, wc_ref, wgl_ref,
                  q_ref, k_ref, v_ref, ko_ref, vo_ref, logf_ref, lfp_ref, sa_ref, *, keys_minor):
    xn = _rms(x_ref[...], g_ref[...]).astype(BF16)
    qkv = _dot(xn, wqkv_ref[...])
    q_ref[...] = (qkv[:, :FOX_DIM] * (HEAD_DIM ** -0.5 * LOG2E)).astype(BF16)
    k = qkv[:, FOX_DIM:2 * FOX_DIM]
    v = qkv[:, 2 * FOX_DIM:]
    k_ref[...] = k.astype(BF16)
    v_ref[...] = v.astype(BF16)
    if keys_minor:
        ko_ref[0] = k.T.reshape(HEADS, HEAD_DIM, k.shape[0])
        vo_ref[0] = v.T.reshape(HEADS, HEAD_DIM, v.shape[0])
    else:
        for h in range(HEADS):
            ko_ref[:, h, :] = k[:, h * HEAD_DIM:(h + 1) * HEAD_DIM]
            vo_ref[:, h, :] = v[:, h * HEAD_DIM:(h + 1) * HEAD_DIM]
    logf = _log_sigmoid(_dot(xn, wf_ref[...]) + bf_ref[...])
    logf_ref[...] = logf[:, :HEADS]
    lane = lax.broadcasted_iota(I32, logf.shape, 1)
    lfp_ref[...] = jnp.where(lane < HEADS, logf, 0.0)
    c3 = _dot(xn, wc_ref[...])
    gate_b = c3[:, :CONV_DIM]
    z = c3[:, CONV_DIM:2 * CONV_DIM] * c3[:, 2 * CONV_DIM:]
    gl = _dot(xn, wgl_ref[...])
    d = gl.shape[1] // 2
    sa_ref[...] = _sigmoid(gl[:, :d]).astype(BF16)
    return gate_b, z, _sigmoid(gl[:, d:])


def _conv_out(gate_b, z, z1, z2, sig_b, wconv_ref, wpb_ref, mb_ref):
    zc = wconv_ref[0:1, :] * z2 + wconv_ref[1:2, :] * z1 + wconv_ref[2:3, :] * z
    yb = _dot((gate_b * zc).astype(BF16), wpb_ref[...])
    mb_ref[...] = (sig_b * yb).astype(BF16)


N_MIXER_W = 8
N_MIXER_OUT = 9


def _mixer_front(x_ref, w_refs, out_refs, keys_minor):
    g_ref, wqkv_ref, wf_ref, bf_ref, wc_ref, wconv_ref, wgl_ref, wpb_ref = w_refs
    gate_b, z, sig_b = _mixer_inputs(x_ref, g_ref, wqkv_ref, wf_ref, bf_ref, wc_ref, wgl_ref, *out_refs[:-1],
                                     keys_minor=keys_minor)
    return gate_b, z, sig_b, wconv_ref, wpb_ref, out_refs[-1]


def _in_proj_seq_kernel(x_ref, prev_ref, *refs, tm):
    w_refs, out_refs = refs[:N_MIXER_W], refs[N_MIXER_W:N_MIXER_W + N_MIXER_OUT]
    tail_ref, zbuf_ref = refs[N_MIXER_W + N_MIXER_OUT:]
    gate_b, z, sig_b, wconv_ref, wpb_ref, mb_ref = _mixer_front(x_ref, w_refs, out_refs, True)
    zbuf_ref[pl.ds(SUBLANES, tm), :] = z

    @pl.when(pl.program_id(1) == 0)
    def _():
        zbuf_ref[pl.ds(SUBLANES - 2, 2), :] = prev_ref[0]

    z1 = zbuf_ref[pl.ds(SUBLANES - 1, tm), :]
    z2 = zbuf_ref[pl.ds(SUBLANES - 2, tm), :]
    _conv_out(gate_b, z, z1, z2, sig_b, wconv_ref, wpb_ref, mb_ref)
    tail = zbuf_ref[pl.ds(tm + SUBLANES - 2, 2), :]
    zbuf_ref[pl.ds(SUBLANES - 2, 2), :] = tail
    tail_ref[0] = tail


def _in_proj_multi_kernel(x_ref, ov1_ref, ov2_ref, *refs, tm, seq_len):
    w_refs, out_refs = refs[:N_MIXER_W], refs[N_MIXER_W:N_MIXER_W + N_MIXER_OUT]
    z_ref, zbuf_ref = refs[N_MIXER_W + N_MIXER_OUT:]
    gate_b, z, sig_b, wconv_ref, wpb_ref, mb_ref = _mixer_front(x_ref, w_refs, out_refs, False)
    z_ref[...] = z
    zbuf_ref[pl.ds(0, SUBLANES), :] = jnp.zeros((SUBLANES, CONV_DIM), F32)
    zbuf_ref[pl.ds(SUBLANES, tm), :] = z
    t = lax.broadcasted_iota(I32, (tm, CONV_DIM), 0) & (seq_len - 1)
    z1 = jnp.where(t == 0, ov1_ref[...], zbuf_ref[pl.ds(SUBLANES - 1, tm), :])
    z2 = jnp.where(t < 2, ov2_ref[...], zbuf_ref[pl.ds(SUBLANES - 2, tm), :])
    _conv_out(gate_b, z, z1, z2, sig_b, wconv_ref, wpb_ref, mb_ref)


def _full(shape):
    n = len(shape)
    return pl.BlockSpec(shape, lambda *_: (0,) * n)


def _in_proj(x, conv_prev, weights, *, seq_len):
    g, wqkv, wf, bf, wc, wconv, wgl, wpb = weights
    n, d = x.shape
    w_specs = [_full(w.shape) for w in (g, wqkv, wf, bf, wc, wconv, wgl, wpb)]
    shapes = (((FOX_DIM,), BF16), ((FOX_DIM,), BF16), ((FOX_DIM,), BF16), ((HEADS, HEAD_DIM), F32), ((HEADS, HEAD_DIM), F32),
              ((HEADS,), F32), ((LANES,), F32), ((d,), BF16), ((d,), BF16))
    assert len(weights) == N_MIXER_W and len(shapes) == N_MIXER_OUT
    out_shape = [jax.ShapeDtypeStruct((n,) + s, t) for s, t in shapes]
    n_common = N_MIXER_OUT
    tm = TOKEN_TILE

    def out_specs(tile_index):
        return [pl.BlockSpec((tm,) + s, lambda *a, k=len(s): (tile_index(*a),) + (0,) * k) for s, _ in shapes]

    if seq_len % tm == 0:
        nb, nj = n // seq_len, seq_len // tm
        row = lambda b, j: (b * nj + j, 0)
        specs = out_specs(lambda b, j: b * nj + j)
        for i in (3, 4):
            out_shape[i] = jax.ShapeDtypeStruct((nb, HEADS, HEAD_DIM, seq_len), F32)
            specs[i] = pl.BlockSpec((1, HEADS, HEAD_DIM, tm), lambda b, j: (b, 0, 0, j))
        outs = pl.pallas_call(
            functools.partial(_in_proj_seq_kernel, tm=tm),
            out_shape=out_shape + [jax.ShapeDtypeStruct((nb, CONV_WIDTH - 1, CONV_DIM), F32)],
            grid=(nb, nj),
            in_specs=[pl.BlockSpec((tm, d), row), pl.BlockSpec((1, CONV_WIDTH - 1, CONV_DIM), lambda b, j: (b, 0, 0))] + w_specs,
            out_specs=specs + [pl.BlockSpec((1, CONV_WIDTH - 1, CONV_DIM), lambda b, j: (b, 0, 0))],
            scratch_shapes=[pltpu.VMEM((tm + SUBLANES, CONV_DIM), F32)],
            compiler_params=_cparams("arbitrary", "arbitrary"),
            name="in_proj_seq",
        )(x, conv_prev, g, wqkv, wf, bf, wc, wconv, wgl, wpb)
        return outs[:n_common], outs[n_common]
    assert seq_len & (seq_len - 1) == 0 and seq_len >= CONV_WIDTH - 1
    assert n % tm == 0 and tm % seq_len == 0
    first = jnp.zeros((n // seq_len, seq_len, CONV_DIM), F32)
    ov1 = first.at[:, 0].set(conv_prev[:, 1]).reshape(n, CONV_DIM)
    ov2 = first.at[:, 0].set(conv_prev[:, 0]).at[:, 1].set(conv_prev[:, 1]).reshape(n, CONV_DIM)
    row = lambda i: (i, 0)
    outs = pl.pallas_call(
        functools.partial(_in_proj_multi_kernel, tm=tm, seq_len=seq_len),
        out_shape=out_shape + [jax.ShapeDtypeStruct((n, CONV_DIM), F32)],
        grid=(n // tm,),
        in_specs=[pl.BlockSpec((tm, d), row), pl.BlockSpec((tm, CONV_DIM), row), pl.BlockSpec((tm, CONV_DIM), row)] + w_specs,
        out_specs=out_specs(lambda i: i) + [pl.BlockSpec((tm, CONV_DIM), row)],
        scratch_shapes=[pltpu.VMEM((tm + SUBLANES, CONV_DIM), F32)],
        compiler_params=_cparams("arbitrary"),
        name="in_proj_multi",
    )(x, ov1, ov2, g, wqkv, wf, bf, wc, wconv, wgl, wpb)
    tail = outs[n_common].reshape(n // seq_len, seq_len, CONV_DIM)[:, seq_len - (CONV_WIDTH - 1):]
    return outs[:n_common], tail


def _cumsum_kernel(lf_ref, c_ref, *, chunk, nchunk):
    r = lax.broadcasted_iota(I32, (chunk, chunk), 0)
    c = lax.broadcasted_iota(I32, (chunk, chunk), 1)
    upper = jnp.where(r <= c, 1.0, 0.0).astype(BF16)
    carry = jnp.zeros((HEADS, 1), F32)
    for n in range(nchunk):
        a = lf_ref[0, n]
        hi = a.astype(BF16)
        r1 = a - hi.astype(F32)
        mid = r1.astype(BF16)
        lo = (r1 - mid.astype(F32)).astype(BF16)
        cs = _dot(hi, upper) + _dot(mid, upper) + _dot(lo, upper) + carry
        c_ref[0, n] = cs
        carry = cs[:, chunk - 1:chunk]


def _cumsum_chunks(lf):
    b, nchunk, _, chunk = lf.shape
    spec = pl.BlockSpec((1, nchunk, HEADS, chunk), lambda i: (i, 0, 0, 0))
    return pl.pallas_call(
        functools.partial(_cumsum_kernel, chunk=chunk, nchunk=nchunk),
        out_shape=jax.ShapeDtypeStruct(lf.shape, F32),
        grid=(b,), in_specs=[spec], out_specs=spec,
        compiler_params=_cparams("arbitrary"),
        name="logf_cumsum",
    )(lf)


C_TERMS = 3


def _cumsum_rows_kernel(lf_ref, cs_ref, *, chunk, nchunk):
    r = lax.broadcasted_iota(I32, (chunk, chunk), 0)
    c = lax.broadcasted_iota(I32, (chunk, chunk), 1)
    lower = jnp.where(c <= r, 1.0, 0.0).astype(BF16)
    carry = jnp.zeros((1, LANES), F32)
    for n in range(nchunk):
        rest = lf_ref[pl.ds(n * chunk, chunk), :]
        cs = carry
        for _ in range(C_TERMS):
            term = rest.astype(BF16)
            cs = cs + _dot(lower, term)
            rest = rest - term.astype(F32)
        carry = cs[chunk - 1:chunk, :]
        packed = jnp.zeros((chunk, LANES), F32)
        rest = cs * LOG2E
        for j in range(C_TERMS):
            term = rest.astype(BF16).astype(F32)
            packed = packed + (term if j == 0 else pltpu.roll(term, j * HEADS, 1))
            rest = rest - term
        cs_ref[pl.ds(n * chunk, chunk), :] = packed.astype(BF16)


def _cumsum_rows(lfp, *, seq):
    n = lfp.shape[0]
    spec = pl.BlockSpec((seq, LANES), lambda b: (b, 0))
    return pl.pallas_call(
        functools.partial(_cumsum_rows_kernel, chunk=TOKEN_TILE, nchunk=seq // TOKEN_TILE),
        out_shape=jax.ShapeDtypeStruct((n, LANES), BF16),
        grid=(n // seq,), in_specs=[spec], out_specs=spec,
        compiler_params=_cparams("arbitrary"),
        name="logf_cumsum_rows",
    )(lfp)


def _attn_prompt_kernel(q_ref, k_ref, v_ref, cs_ref, o_ref, ka_ref, vt_ref, m_ref, acc_ref, *, t, tk, nk):
    hp, i = pl.program_id(1), pl.program_id(2)
    rr = lax.broadcasted_iota(I32, (LANES, LANES), 0)
    cc = lax.broadcasted_iota(I32, (LANES, LANES), 1)
    sel = [jnp.where(jnp.where(cc < HEAD_DIM, rr - cc, -1) == HEAD_DIM * h, 1.0, 0.0).astype(BF16) for h in range(2)]

    @pl.when(i == 0)
    def _():
        kb = k_ref[...]
        cs = cs_ref[...]
        vt = v_ref[...].astype(F32).T
        row = lax.broadcasted_iota(I32, (SUBLANES, tk), 0)
        ones = jnp.where(row == 0, 1.0, 0.0)
        pad = jnp.zeros((LANES - HEAD_DIM - SUBLANES, tk), F32)
        for h in range(2):
            head = 2 * hp + h
            src = jnp.where(cc >= HEAD_DIM, (cc - HEAD_DIM) * HEADS + head, -1)
            src = jnp.where(cc < HEAD_DIM + C_TERMS, src, -1)
            place = jnp.where(rr == src, -1.0, 0.0).astype(BF16)
            ka_ref[h] = (_dot(kb, sel[h]) + _dot(cs, place)).astype(BF16)
            for n in range(nk):
                vh = vt[h * HEAD_DIM:(h + 1) * HEAD_DIM, n * tk:(n + 1) * tk]
                vt_ref[h, n] = jnp.concatenate([vh, ones, pad], axis=0).astype(BF16)

    lane = lax.broadcasted_iota(I32, (1, LANES), 1)
    one_lanes = jnp.where(jnp.logical_and(lane >= HEAD_DIM, lane < HEAD_DIM + C_TERMS), 1.0, 0.0)
    q = q_ref[...]
    qa = [(_dot(q, sel[h]) + one_lanes).astype(BF16) for h in range(2)]
    m_ref[...] = jnp.full(m_ref.shape, -jnp.inf, F32)
    acc_ref[...] = jnp.zeros(acc_ref.shape, F32)

    def scores(j):
        start = pl.multiple_of(j * tk, tk)
        return tuple(_dot_nt(ka_ref[h, pl.ds(start, tk), :], qa[h]) for h in range(2))

    assert tk == t

    def visible(st, limit):
        ki = lax.broadcasted_iota(I32, (tk, t), 0)
        qi = lax.broadcasted_iota(I32, (tk, t), 1)
        return jnp.where(ki <= qi + limit, st, -jnp.inf)

    def max_pass(j, limit):
        st_pair = scores(j)
        for h in range(2):
            st = st_pair[h] if limit is None else visible(st_pair[h], limit)
            m_ref[j + 1, h] = jnp.maximum(m_ref[j, h], jnp.max(st, axis=0, keepdims=True))

    def weight_pass(j, limit):
        st_pair = scores(j)
        for h in range(2):
            st = st_pair[h] if limit is None else visible(st_pair[h], limit)
            m_new = m_ref[j + 1, h]
            p = jnp.exp2(st - m_new).astype(BF16)
            acc_ref[h] = acc_ref[h] * jnp.exp2(m_ref[j, h] - m_new) + _dot(vt_ref[h, j], p)

    max_pass(0, i * t)

    def body(j, carry):
        max_pass(j + 1, None)
        weight_pass(j, None)
        return carry

    lax.fori_loop(0, i - 1, body, 0)

    @pl.when(i >= 1)
    def _():
        max_pass(i, 0)
        weight_pass(i - 1, None)

    weight_pass(i, 0)
    halves = []
    for h in range(2):
        acc = acc_ref[h]
        halves.append(acc[:HEAD_DIM] * (1.0 / acc[HEAD_DIM:HEAD_DIM + 1]))
    o_ref[...] = jnp.concatenate(halves, axis=0).T.astype(BF16)


def _attn_prompt(q, k, v, cs, *, nb, seq):
    t, tk = TOKEN_TILE, KEY_CHUNK
    nq, nk = seq // t, seq // tk
    return pl.pallas_call(
        functools.partial(_attn_prompt_kernel, t=t, tk=tk, nk=nk),
        out_shape=jax.ShapeDtypeStruct(q.shape, BF16),
        grid=(nb, HEADS // 2, nq),
        in_specs=[pl.BlockSpec((t, LANES), lambda b, hp, i: (b * nq + i, hp)),
                  pl.BlockSpec((seq, LANES), lambda b, hp, i: (b, hp)),
                  pl.BlockSpec((seq, LANES), lambda b, hp, i: (b, hp)),
                  pl.BlockSpec((seq, LANES), lambda b, hp, i: (b, 0))],
        out_specs=pl.BlockSpec((t, LANES), lambda b, hp, i: (b * nq + i, hp)),
        scratch_shapes=[pltpu.VMEM((2, seq, LANES), BF16), pltpu.VMEM((2, nk, LANES, tk), BF16),
                        pltpu.VMEM((nk + 1, 2, 1, t), F32), pltpu.VMEM((2, LANES, t), F32)],
        compiler_params=_cparams("arbitrary", "arbitrary", "arbitrary"),
        name="attn_prompt",
    )(q, k, v, cs)


def _attn_sample_kernel(q_ref, kc_ref, vc_ref, kn_ref, vn_ref, c_ref, o_ref,
                        qbd_ref, m_ref, l_ref, acc_ref, kpad_ref, vpad_ref, *, nq, nchunk):
    j = pl.program_id(1)
    rows = HEADS * nq
    row_head = lax.broadcasted_iota(I32, (rows, FOX_DIM), 0) >> (nq.bit_length() - 1)
    col_head = lax.broadcasted_iota(I32, (rows, FOX_DIM), 1) >> (HEAD_DIM.bit_length() - 1)
    own = row_head == col_head

    @pl.when(j == 0)
    def _():
        qt = jnp.concatenate([q_ref[...]] * HEADS, axis=0)
        qbd_ref[...] = jnp.where(own, qt, jnp.zeros_like(qt))
        m_ref[...] = jnp.full(m_ref.shape, -jnp.inf, F32)
        l_ref[...] = jnp.zeros(l_ref.shape, F32)
        acc_ref[...] = jnp.zeros(acc_ref.shape, F32)

    def update(s, cvals, visible, weighted_values):
        width = s.shape[1]
        bias = jnp.concatenate([jnp.broadcast_to(cvals[h:h + 1, :], (nq, width)) for h in range(HEADS)], axis=0)
        s = s - LOG2E * bias
        if visible is not None:
            s = jnp.where(visible, s, -jnp.inf)
        m_prev = m_ref[...]
        m_new = jnp.maximum(m_prev, jnp.max(s, axis=-1, keepdims=True))
        a = jnp.exp2(m_prev - m_new)
        p = jnp.exp2(s - m_new)
        l_ref[...] = a * l_ref[...] + jnp.sum(p, axis=-1, keepdims=True)
        m_ref[...] = m_new
        acc_ref[...] = acc_ref[...] * a + weighted_values(p.astype(BF16))

    @pl.when(j < nchunk)
    def _():
        chunk = kc_ref.shape[3]
        kt = kc_ref[0].reshape(FOX_DIM, chunk).astype(BF16)
        vt = vc_ref[0].reshape(FOX_DIM, chunk).astype(BF16)
        update(_dot(qbd_ref[...], kt), c_ref[0, j], None, lambda p: _dot_nt(p, vt))

    @pl.when(j == nchunk)
    def _():
        kpad_ref[...] = jnp.zeros(kpad_ref.shape, BF16)
        vpad_ref[...] = jnp.zeros(vpad_ref.shape, BF16)
        kpad_ref[pl.ds(0, nq), :] = kn_ref[...]
        vpad_ref[pl.ds(0, nq), :] = vn_ref[...]
        ki = lax.broadcasted_iota(I32, (rows, LANES), 1)
        qi = lax.broadcasted_iota(I32, (rows, LANES), 0) & (nq - 1)
        update(_dot_nt(qbd_ref[...], kpad_ref[...]), c_ref[0, nchunk][:, :LANES], ki <= qi,
               lambda p: _dot(p, vpad_ref[...]))
        out = jnp.where(own, acc_ref[...] * (1.0 / l_ref[...]), 0.0)
        o = out[0:nq]
        for h in range(1, HEADS):
            o = o + out[h * nq:(h + 1) * nq]
        o_ref[...] = o.astype(BF16)


def _attn_sample(q, k_cache, v_cache, k_new, v_new, c, *, nq):
    nb, past = k_cache.shape[0], k_cache.shape[3]
    chunk = c.shape[-1]
    nchunk = past // chunk
    assert nq & (nq - 1) == 0 and nq <= LANES and past % chunk == 0 and c.shape[1] == nchunk + 1
    rows = HEADS * nq
    cache_spec = pl.BlockSpec((1, HEADS, HEAD_DIM, chunk), lambda b, j: (b, 0, 0, jnp.minimum(j, nchunk - 1)))
    new_spec = pl.BlockSpec((nq, FOX_DIM), lambda b, j: (b, 0))
    return pl.pallas_call(
        functools.partial(_attn_sample_kernel, nq=nq, nchunk=nchunk),
        out_shape=jax.ShapeDtypeStruct(q.shape, BF16),
        grid=(nb, nchunk + 1),
        in_specs=[new_spec, cache_spec, cache_spec, new_spec, new_spec,
                  pl.BlockSpec((1, nchunk + 1, HEADS, chunk), lambda b, j: (b, 0, 0, 0))],
        out_specs=new_spec,
        scratch_shapes=[pltpu.VMEM((rows, FOX_DIM), BF16), pltpu.VMEM((rows, 1), F32), pltpu.VMEM((rows, 1), F32),
                        pltpu.VMEM((rows, FOX_DIM), F32), pltpu.VMEM((LANES, FOX_DIM), BF16),
                        pltpu.VMEM((LANES, FOX_DIM), BF16)],
        compiler_params=_cparams("arbitrary", "arbitrary"),
        name="attn_sample",
    )(q, k_cache, v_cache, k_new, v_new, c)


def _pick(is_first, a_ref, b_ref):
    return jnp.where(is_first, a_ref[...], b_ref[...])


def _post_attn_kernel(xp_ref, xs_ref, op_ref, os_ref, sap_ref, sas_ref, mbp_ref, mbs_ref,
                      wpa_ref, wo_ref, g_ref, wrt_ref, br_ref,
                      h1_ref, xn_ref, pos_ref, gate_ref, cnt_ref, *, n_first, tm):
    is_p = pl.program_id(0) < n_first
    ya = _dot(_pick(is_p, op_ref, os_ref), wpa_ref[...])
    merged = _pick(is_p, sap_ref, sas_ref).astype(F32) * ya + _pick(is_p, mbp_ref, mbs_ref).astype(F32)
    h1 = _pick(is_p, xp_ref, xs_ref) + _dot(merged.astype(BF16), wo_ref[...])
    h1_ref[...] = h1
    xn = _rms(h1, g_ref[...]).astype(BF16)
    xn_ref[...] = xn

    lt = _dot_nt(wrt_ref[...], xn) + br_ref[...]
    eio = lax.broadcasted_iota(I32, (N_EXPERTS, tm), 0).astype(F32)
    vals, hots = [], []
    for _ in range(TOP_K):
        m = jnp.max(lt, axis=0, keepdims=True)
        idx = jnp.min(jnp.where(lt == m, eio, float(N_EXPERTS)), axis=0, keepdims=True)
        hot = eio == idx
        vals.append(m)
        hots.append(hot)
        lt = jnp.where(hot, -jnp.inf, lt)
    ex = [jnp.exp(v - vals[0]) for v in vals]
    den = ex[0] + ex[1] + ex[2] + ex[3]
    gate_ref[...] = jnp.concatenate([e / den for e in ex], axis=0)

    chosen = jnp.zeros((N_EXPERTS, tm), F32)
    for hot in hots:
        chosen = jnp.where(hot, 1.0, chosen)
    td = DISPATCH_TILE
    r = lax.broadcasted_iota(I32, (td, td), 0)
    c = lax.broadcasted_iota(I32, (td, td), 1)
    before = jnp.where(r < c, 1.0, 0.0).astype(BF16)
    er = lax.broadcasted_iota(I32, (N_EXPERTS, N_EXPERTS), 0)
    ec = lax.broadcasted_iota(I32, (N_EXPERTS, N_EXPERTS), 1)
    lower = jnp.where(ec < er, 1.0, 0.0).astype(BF16)
    for sub in range(tm // td):
        sl = slice(sub * td, (sub + 1) * td)
        ch = chosen[:, sl]
        rank = _dot(ch.astype(BF16), before)
        cnt = rank[:, td - 1:td] + ch[:, td - 1:td]
        units = jnp.floor((cnt + (ROW_ALIGN - 1)) * (1.0 / ROW_ALIGN))
        start = ROW_ALIGN * _dot(lower, jnp.broadcast_to(units, (N_EXPERTS, td)).astype(BF16))
        base = start + rank
        pos = [jnp.sum(jnp.where(hot[:, sl], base, 0.0), axis=0, keepdims=True) for hot in hots]
        pos_ref[:, sl] = jnp.concatenate(pos, axis=0).astype(I32)
        cnt_ref[sub] = jnp.broadcast_to(cnt, (N_EXPERTS, LANES)).astype(I32)


def _post_attn(xp, xs, op, os_, sap, sas, mbp, mbs, wpa, wo, g, wrt, br):
    n_p, d = xp.shape
    n_s = xs.shape[0]
    tm = TOKEN_TILE
    assert n_p % tm == 0 and n_s % tm == 0
    n_first, n_tiles = n_p // tm, (n_p + n_s) // tm
    n = n_p + n_s
    sub = tm // DISPATCH_TILE
    first = lambda i: (jnp.minimum(i, n_first - 1), 0)
    second = lambda i: (jnp.maximum(i - n_first, 0), 0)
    row = lambda i: (i, 0)
    col = lambda i: (0, i)

    def pair(width):
        return [pl.BlockSpec((tm, width), first), pl.BlockSpec((tm, width), second)]

    return pl.pallas_call(
        functools.partial(_post_attn_kernel, n_first=n_first, tm=tm),
        out_shape=[jax.ShapeDtypeStruct((n, d), F32), jax.ShapeDtypeStruct((n, d), BF16),
                   jax.ShapeDtypeStruct((TOP_K, n), I32), jax.ShapeDtypeStruct((TOP_K, n), F32),
                   jax.ShapeDtypeStruct((n // DISPATCH_TILE, N_EXPERTS, LANES), I32)],
        grid=(n_tiles,),
        in_specs=pair(d) + pair(FOX_DIM) + pair(d) + pair(d) + [_full(w.shape) for w in (wpa, wo, g, wrt, br)],
        out_specs=[pl.BlockSpec((tm, d), row), pl.BlockSpec((tm, d), row),
                   pl.BlockSpec((TOP_K, tm), col), pl.BlockSpec((TOP_K, tm), col),
                   pl.BlockSpec((sub, N_EXPERTS, LANES), lambda i: (i, 0, 0))],
        compiler_params=_cparams("arbitrary"),
        name="post_attn_router",
    )(xp, xs, op, os_, sap, sas, mbp, mbs, wpa, wo, g, wrt, br)


def _for_each_run(tile, len_ref, loc_ref, dst_ref, fn):
    def body(e, carry):
        idx = tile * N_EXPERTS + e
        n, loc, dst = len_ref[idx], loc_ref[idx], dst_ref[idx]

        for c in RUN_CHUNKS:
            off = n & ~(2 * c - 1)

            @pl.when((n & c) != 0)
            def _():
                fn(pl.multiple_of(loc + off, ROW_ALIGN), pl.multiple_of(dst + off, ROW_ALIGN), c)
        return carry

    for e in range(N_EXPERTS):
        body(e, 0)


TILE_CHUNKS = tuple(ROW_ALIGN << b for b in reversed(range((LOCAL_ROWS // ROW_ALIGN).bit_length())))


def _wait_tile_rows(tile, len_ref, loc_ref, wait_rows):
    last = tile * N_EXPERTS + N_EXPERTS - 1
    total = loc_ref[last] + len_ref[last]
    for c in TILE_CHUNKS:
        @pl.when((total & c) != 0)
        def _():
            wait_rows(c)


def _dispatch_kernel(len_ref, loc_ref, dst_ref, xn_ref, pos_ref, xs_hbm, xloc_ref, sem_ref, *, n_tiles):
    i = pl.program_id(0)
    slot = lax.rem(i, 2)

    def copy(s, loc, dst, rows):
        return pltpu.make_async_copy(xloc_ref.at[s, pl.ds(loc, rows), :], xs_hbm.at[pl.ds(dst, rows), :], sem_ref.at[s])

    def start_tile(tile, s):
        _for_each_run(tile, len_ref, loc_ref, dst_ref, lambda loc, dst, rows: copy(s, loc, dst, rows).start())

    def wait_tile(tile, s):
        _wait_tile_rows(tile, len_ref, loc_ref, lambda rows: copy(s, 0, 0, rows).wait())

    pos = pos_ref[...]
    aio = lax.broadcasted_iota(I32, (LOCAL_ROWS, DISPATCH_TILE), 0)
    sel = jnp.zeros((LOCAL_ROWS, DISPATCH_TILE), F32)
    for k in range(TOP_K):
        sel = jnp.where(aio == pos[k:k + 1, :], 1.0, sel)
    rows_sorted = _dot(sel.astype(BF16), xn_ref[...])

    @pl.when(i >= 2)
    def _():
        wait_tile(i - 2, slot)

    xloc_ref[slot] = rows_sorted
    start_tile(i, slot)

    @pl.when(i == n_tiles - 1)
    def _():
        if n_tiles >= 2:
            wait_tile(i - 1, 1 - slot)
        wait_tile(i, slot)


def _dispatch(run_len, run_loc, run_dst, xn, pos, *, total_rows):
    n, d = xn.shape
    td = DISPATCH_TILE
    n_tiles = n // td
    return pl.pallas_call(
        functools.partial(_dispatch_kernel, n_tiles=n_tiles),
        out_shape=jax.ShapeDtypeStruct((total_rows, d), F32),
        grid_spec=pltpu.PrefetchScalarGridSpec(
            num_scalar_prefetch=3, grid=(n_tiles,),
            in_specs=[pl.BlockSpec((td, d), lambda i, *_: (i, 0)), pl.BlockSpec((TOP_K, td), lambda i, *_: (0, i))],
            out_specs=pl.BlockSpec(memory_space=pl.ANY),
            scratch_shapes=[pltpu.VMEM((2, LOCAL_ROWS, d), F32), pltpu.SemaphoreType.DMA((2,))]),
        compiler_params=_cparams("arbitrary"),
        name="moe_dispatch",
    )(run_len, run_loc, run_dst, xn, pos)


def _ffn_kernel(bexp_ref, nval_ref, last_ref, xs_ref, wgu_ref, bgu_ref, wdn_ref, bdn_ref, o_ref, wgu_s, wdn_s):
    i = pl.program_id(0)
    nv = nval_ref[i]

    @pl.when(jnp.logical_or(i == 0, bexp_ref[i] != bexp_ref[jnp.maximum(i - 1, 0)]))
    def _():
        wgu_s[...] = wgu_ref[0].astype(BF16)
        wdn_s[...] = wdn_ref[0].astype(BF16)

    for r in range(FFN_ROWS // FFN_SUB):
        @pl.when(nv > r * FFN_SUB)
        def _():
            rows = lax.broadcasted_iota(I32, (FFN_SUB, xs_ref.shape[1]), 0) + r * FFN_SUB
            x = jnp.where(rows < nv, xs_ref[pl.ds(r * FFN_SUB, FFN_SUB), :], 0.0).astype(BF16)
            gu = _dot(x, wgu_s[...]) + bgu_ref[0]
            gt = jnp.minimum(gu[:, :D_FF], SWIGLU_LIMIT)
            up = jnp.clip(gu[:, D_FF:], -SWIGLU_LIMIT, SWIGLU_LIMIT)
            act = (up + 1.0) * gt * _sigmoid(SWIGLU_ALPHA * gt)
            o_ref[pl.ds(r * FFN_SUB, FFN_SUB), :] = _dot(act.astype(BF16), wdn_s[...]) + bdn_ref[0]


def _ffn(bexp, nval, last, xs, wgu, bgu, wdn, bdn):
    rows, d = xs.shape
    nblk = rows // FFN_ROWS
    blk = lambda i, bexp, nval, last: (jnp.minimum(i, last[0]), 0)
    exp3 = lambda i, bexp, nval, last: (bexp[i], 0, 0)
    return pl.pallas_call(
        _ffn_kernel,
        out_shape=jax.ShapeDtypeStruct((rows, d), F32),
        grid_spec=pltpu.PrefetchScalarGridSpec(
            num_scalar_prefetch=3, grid=(nblk,),
            in_specs=[pl.BlockSpec((FFN_ROWS, d), blk),
                      pl.BlockSpec((1, d, 2 * D_FF), exp3), pl.BlockSpec((1, 1, 2 * D_FF), exp3),
                      pl.BlockSpec((1, D_FF, d), exp3), pl.BlockSpec((1, 1, d), exp3)],
            out_specs=pl.BlockSpec((FFN_ROWS, d), blk),
            scratch_shapes=[pltpu.VMEM((d, 2 * D_FF), BF16), pltpu.VMEM((D_FF, d), BF16)]),
        compiler_params=_cparams("arbitrary"),
        name="moe_ffn",
    )(bexp, nval, last, xs, wgu, bgu, wdn, bdn)


def _combine_kernel(len_ref, loc_ref, dst_ref, os_hbm, pos_ref, gate_ref, h1_ref, pp_ref, ps_ref,
                    gple_ref, wpg_ref, wpp_ref, gfin_ref, yp_ref, ys_ref, oloc_ref, sem_ref, *, n_tiles, n_first):
    i = pl.program_id(0)
    slot = lax.rem(i, 2)

    def copy(s, loc, dst, rows):
        return pltpu.make_async_copy(os_hbm.at[pl.ds(dst, rows), :], oloc_ref.at[s, pl.ds(loc, rows), :], sem_ref.at[s])

    def start_tile(tile, s):
        _for_each_run(tile, len_ref, loc_ref, dst_ref, lambda loc, dst, rows: copy(s, loc, dst, rows).start())

    def wait_tile(tile, s):
        _wait_tile_rows(tile, len_ref, loc_ref, lambda rows: copy(s, 0, 0, rows).wait())

    @pl.when(i == 0)
    def _():
        oloc_ref[...] = jnp.zeros(oloc_ref.shape, F32)
        start_tile(0, 0)

    @pl.when(i + 1 < n_tiles)
    def _():
        start_tile(i + 1, 1 - slot)

    wait_tile(i, slot)

    pos, gate = pos_ref[...], gate_ref[...]
    lio = lax.broadcasted_iota(I32, (DISPATCH_TILE, LOCAL_ROWS), 1)
    w = jnp.zeros((DISPATCH_TILE, LOCAL_ROWS), F32)
    for k in range(TOP_K):
        w = jnp.where(lio == pos[:, k:k + 1], gate[:, k:k + 1], w)
    h2 = h1_ref[...] + _dot(w.astype(BF16), oloc_ref[slot].astype(BF16))

    is_p = i < n_first
    xn = _rms(h2, gple_ref[...]).astype(BF16)
    ple = _dot(_pick(is_p, pp_ref, ps_ref).astype(BF16), wpp_ref[...])
    h3 = h2 + _sigmoid(_dot(xn, wpg_ref[...])) * ple
    y = _rms(h3, gfin_ref[...])

    @pl.when(is_p)
    def _():
        yp_ref[...] = y

    @pl.when(jnp.logical_not(is_p))
    def _():
        ys_ref[...] = y


def _combine(run_len, run_loc, run_dst, os_, pos, gate, h1, pp, ps, gple, wpg, wpp, gfin):
    n, d = h1.shape
    td = DISPATCH_TILE
    n_p, n_s = pp.shape[0], ps.shape[0]
    assert n_p % td == 0 and n_s % td == 0
    n_tiles, n_first = n // td, n_p // td
    pw = pp.shape[1]
    first = lambda i, *_: (jnp.minimum(i, n_first - 1), 0)
    second = lambda i, *_: (jnp.maximum(i - n_first, 0), 0)
    row = lambda i, *_: (i, 0)
    full2 = lambda i, *_: (0, 0)
    return pl.pallas_call(
        functools.partial(_combine_kernel, n_tiles=n_tiles, n_first=n_first),
        out_shape=[jax.ShapeDtypeStruct((n_p, d), F32), jax.ShapeDtypeStruct((n_s, d), F32)],
        grid_spec=pltpu.PrefetchScalarGridSpec(
            num_scalar_prefetch=3, grid=(n_tiles,),
            in_specs=[pl.BlockSpec(memory_space=pl.ANY),
                      pl.BlockSpec((td, TOP_K), row), pl.BlockSpec((td, TOP_K), row), pl.BlockSpec((td, d), row),
                      pl.BlockSpec((td, pw), first), pl.BlockSpec((td, pw), second),
                      pl.BlockSpec(gple.shape, full2), pl.BlockSpec(wpg.shape, full2),
                      pl.BlockSpec(wpp.shape, full2), pl.BlockSpec(gfin.shape, full2)],
            out_specs=[pl.BlockSpec((td, d), first), pl.BlockSpec((td, d), second)],
            scratch_shapes=[pltpu.VMEM((2, LOCAL_ROWS, d), F32), pltpu.SemaphoreType.DMA((2,))]),
        compiler_params=_cparams("arbitrary"),
        name="moe_combine_ple",
    )(run_len, run_loc, run_dst, os_, pos, gate, h1, pp, ps, gple, wpg, wpp, gfin)


def _routing_tables(cnt, total_rows):
    n_tiles = cnt.shape[0]
    run_len = (cnt + (ROW_ALIGN - 1)) // ROW_ALIGN * ROW_ALIGN
    run_loc = jnp.cumsum(run_len, axis=1) - run_len
    per_expert = jnp.sum(run_len, axis=0)
    region = (per_expert + (FFN_ROWS - 1)) // FFN_ROWS * FFN_ROWS
    region_end = jnp.cumsum(region)
    region_start = region_end - region
    run_dst = region_start[None, :] + jnp.cumsum(run_len, axis=0) - run_len
    nblk = total_rows // FFN_ROWS
    blk_row = jnp.arange(nblk, dtype=I32) * FFN_ROWS
    used = region_end[-1] // FFN_ROWS
    last = jnp.maximum(used - 1, 0)
    bexp = jnp.minimum(jnp.sum(blk_row[:, None] >= region_end[None, :], axis=1), N_EXPERTS - 1).astype(I32)
    nval = jnp.clip(per_expert[bexp] - (blk_row - region_start[bexp]), 0, FFN_ROWS)
    nval = jnp.where(blk_row < region_end[-1], nval, 0).astype(I32)
    bexp = jnp.where(blk_row < region_end[-1], bexp, bexp[last])
    flat = lambda a: a.reshape(n_tiles * N_EXPERTS).astype(I32)
    return flat(run_len), flat(run_loc), flat(run_dst), bexp, nval, last.reshape(1).astype(I32)


def _chunked_t(lf, chunk):
    b, length, h = lf.shape
    return lf.reshape(b, length // chunk, chunk, h).transpose(0, 1, 3, 2)


def kernel(x_prompt, x_sample, p_prompt, p_sample, cache_k, cache_v, cache_logf, state_conv, g_mix, w_in, b_f, w_conv, w_pa, w_pb, w_o, g_ffn, w_router, b_router, w_gu, b_gu, w_dn, b_dn, g_ple, w_ple_gate, w_ple_proj, g_final):
    depth = g_mix.shape[0]
    assert depth == 1
    nb, seq, d = x_prompt.shape
    db, dseq, _ = x_sample.shape
    past = cache_k.shape[2]
    n_p, n_s = nb * seq, db * dseq

    w = w_in[0].astype(BF16)
    o_f = 3 * FOX_DIM
    o_c = o_f + HEADS
    o_g = o_c + 3 * CONV_DIM
    wf = jnp.pad(w[:, o_f:o_c], ((0, 0), (0, LANES - HEADS)))
    bf = jnp.pad(b_f[0], (0, LANES - HEADS)).reshape(1, LANES)
    mixer_w = (g_mix[0].reshape(1, d), w[:, :o_f], wf, bf, w[:, o_c:o_g], w_conv[0], w[:, o_g:], w_pb[0].astype(BF16))

    (q_p, k_p, v_p, ko_p, vo_p, logf_p, lfp_p, sa_p, mb_p), tail_p = _in_proj(
        x_prompt.reshape(n_p, d), jnp.zeros((nb, CONV_WIDTH - 1, CONV_DIM), F32), mixer_w, seq_len=seq)
    o_p = _attn_prompt(q_p, k_p, v_p, _cumsum_rows(lfp_p, seq=seq), nb=nb, seq=seq)

    (q_s, k_s, v_s, ko_s, vo_s, logf_s, _, sa_s, mb_s), tail_s = _in_proj(
        x_sample.reshape(n_s, d), state_conv[0], mixer_w, seq_len=dseq)
    chunk = min(past, 2 * TOKEN_TILE)
    lf_all = jnp.concatenate([cache_logf[0].astype(F32), logf_s.reshape(db, dseq, HEADS),
                              jnp.zeros((db, chunk - dseq, HEADS), F32)], axis=1)
    c_s = _cumsum_chunks(_chunked_t(lf_all, chunk))
    o_s = _attn_sample(q_s, cache_k[0].transpose(0, 2, 3, 1), cache_v[0].transpose(0, 2, 3, 1), k_s, v_s, c_s, nq=dseq)

    h1, xn2, pos_t, gate_t, cnt = _post_attn(
        x_prompt.reshape(n_p, d), x_sample.reshape(n_s, d), o_p, o_s, sa_p, sa_s, mb_p, mb_s,
        w_pa[0].astype(BF16), w_o[0].astype(BF16), g_ffn[0].reshape(1, d),
        w_router[0].T.astype(BF16), b_router[0].reshape(N_EXPERTS, 1))

    n = n_p + n_s
    n_tiles = n // DISPATCH_TILE
    max_rows = n * TOP_K + n_tiles * N_EXPERTS * (ROW_ALIGN - 1) + N_EXPERTS * (FFN_ROWS - 1)
    total_rows = -(-max_rows // FFN_ROWS) * FFN_ROWS
    run_len, run_loc, run_dst, bexp, nval, last = _routing_tables(cnt[:, :, 0], total_rows)
    xs = _dispatch(run_len, run_loc, run_dst, xn2, pos_t, total_rows=total_rows)
    os_ = _ffn(bexp, nval, last, xs, w_gu[0], b_gu[0].reshape(N_EXPERTS, 1, 2 * D_FF),
               w_dn[0], b_dn[0].reshape(N_EXPERTS, 1, d))

    y_p, y_s = _combine(run_len, run_loc, run_dst, os_, pos_t.T, gate_t.T, h1,
                        p_prompt[0].reshape(n_p, -1), p_sample[0].reshape(n_s, -1),
                        g_ple[0].reshape(1, d), w_ple_gate[0].astype(BF16), w_ple_proj[0].astype(BF16),
                        g_final.reshape(1, d))

    return (y_p.reshape(nb, seq, d), y_s.reshape(db, dseq, d),
            ko_p.transpose(0, 3, 1, 2)[None], vo_p.transpose(0, 3, 1, 2)[None],
            logf_p.reshape(1, nb, seq, HEADS), tail_p[None],
            ko_s.reshape(1, db, dseq, HEADS, HEAD_DIM), vo_s.reshape(1, db, dseq, HEADS, HEAD_DIM),
            logf_s.reshape(1, db, dseq, HEADS), tail_s[None])
```

```python
import functools

import jax
import jax.numpy as jnp
from jax import lax
from jax.experimental import pallas as pl
from jax.experimental.pallas import tpu as pltpu

F32, BF16, I32 = jnp.float32, jnp.bfloat16, jnp.int32

HEADS = 8
HEAD_DIM = 64
FOX_DIM = HEADS * HEAD_DIM
CONV_DIM = 512
CONV_WIDTH = 3
N_EXPERTS = 32
TOP_K = 4
D_FF = 1024
SWIGLU_ALPHA = 1.702
SWIGLU_LIMIT = 7.0
RMS_EPS = 1e-6
LOG2E = 1.4426950408889634

LANES = 128
SUBLANES = 8
TOKEN_TILE = 512
KEY_CHUNK = 512
DISPATCH_TILE = 256
ROW_ALIGN = SUBLANES
LOCAL_ROWS = DISPATCH_TILE * TOP_K + N_EXPERTS * ROW_ALIGN
FFN_ROWS = 1024
FFN_SUB = 512
LOCAL_GROUPS = LOCAL_ROWS // ROW_ALIGN
RUN_CHUNKS = (32, 16, 8, 4, 2, 1)
VMEM_LIMIT_BYTES = 56 * 1024 * 1024


def _cparams(*sem):
    return pltpu.CompilerParams(dimension_semantics=sem, vmem_limit_bytes=VMEM_LIMIT_BYTES)


def _rms(x, g):
    return x * lax.rsqrt(jnp.mean(x * x, axis=-1, keepdims=True) + RMS_EPS) * g


def _sigmoid(x):
    return 1.0 / (1.0 + jnp.exp(-x))


def _log_sigmoid(x):
    return jnp.minimum(x, 0.0) - jnp.log1p(jnp.exp(-jnp.abs(x)))


def _dot(a, b):
    return jnp.dot(a, b, preferred_element_type=F32)


def _dot_nt(a, b):
    return lax.dot_general(a, b, (((1,), (1,)), ((), ())), preferred_element_type=F32)


def _mixer_inputs(x_ref, g_ref, wqkv_ref, wf_ref, bf_ref, wc_ref, wgl_ref,
                  q_ref, k_ref, v_ref, ko_ref, vo_ref, logf_ref, lfp_ref, sa_ref, *, keys_minor):
    xn = _rms(x_ref[...], g_ref[...]).astype(BF16)
    qkv = _dot(xn, wqkv_ref[...])
    q_ref[...] = (qkv[:, :FOX_DIM] * (HEAD_DIM ** -0.5 * LOG2E)).astype(BF16)
    k = qkv[:, FOX_DIM:2 * FOX_DIM]
    v = qkv[:, 2 * FOX_DIM:]
    k_ref[...] = k.astype(BF16)
    v_ref[...] = v.astype(BF16)
    if keys_minor:
        ko_ref[0] = k.T.reshape(HEADS, HEAD_DIM, k.shape[0])
        vo_ref[0] = v.T.reshape(HEADS, HEAD_DIM, v.shape[0])
    else:
        for h in range(HEADS):
            ko_ref[:, h, :] = k[:, h * HEAD_DIM:(h + 1) * HEAD_DIM]
            vo_ref[:, h, :] = v[:, h * HEAD_DIM:(h + 1) * HEAD_DIM]
    logf = _log_sigmoid(_dot(xn, wf_ref[...]) + bf_ref[...])
    logf_ref[...] = logf[:, :HEADS]
    lane = lax.broadcasted_iota(I32, logf.shape, 1)
    lfp_ref[...] = jnp.where(lane < HEADS, logf, 0.0)
    c3 = _dot(xn, wc_ref[...])
    gate_b = c3[:, :CONV_DIM]
    z = c3[:, CONV_DIM:2 * CONV_DIM] * c3[:, 2 * CONV_DIM:]
    gl = _dot(xn, wgl_ref[...])
    d = gl.shape[1] // 2
    sa_ref[...] = _sigmoid(gl[:, :d]).astype(BF16)
    return gate_b, z, _sigmoid(gl[:, d:])


def _conv_out(gate_b, z, z1, z2, sig_b, wconv_ref, wpb_ref, mb_ref):
    zc = wconv_ref[0:1, :] * z2 + wconv_ref[1:2, :] * z1 + wconv_ref[2:3, :] * z
    yb = _dot((gate_b * zc).astype(BF16), wpb_ref[...])
    mb_ref[...] = (sig_b * yb).astype(BF16)


N_MIXER_W = 8
N_MIXER_OUT = 9


def _mixer_front(x_ref, w_refs, out_refs, keys_minor):
    g_ref, wqkv_ref, wf_ref, bf_ref, wc_ref, wconv_ref, wgl_ref, wpb_ref = w_refs
    gate_b, z, sig_b = _mixer_inputs(x_ref, g_ref, wqkv_ref, wf_ref, bf_ref, wc_ref, wgl_ref, *out_refs[:-1],
                                     keys_minor=keys_minor)
    return gate_b, z, sig_b, wconv_ref, wpb_ref, out_refs[-1]


def _in_proj_seq_kernel(x_ref, prev_ref, *refs, tm):
    w_refs, out_refs = refs[:N_MIXER_W], refs[N_MIXER_W:N_MIXER_W + N_MIXER_OUT]
    tail_ref, zbuf_ref = refs[N_MIXER_W + N_MIXER_OUT:]
    gate_b, z, sig_b, wconv_ref, wpb_ref, mb_ref = _mixer_front(x_ref, w_refs, out_refs, True)
    zbuf_ref[pl.ds(SUBLANES, tm), :] = z

    @pl.when(pl.program_id(1) == 0)
    def _():
        zbuf_ref[pl.ds(SUBLANES - 2, 2), :] = prev_ref[0]

    z1 = zbuf_ref[pl.ds(SUBLANES - 1, tm), :]
    z2 = zbuf_ref[pl.ds(SUBLANES - 2, tm), :]
    _conv_out(gate_b, z, z1, z2, sig_b, wconv_ref, wpb_ref, mb_ref)
    tail = zbuf_ref[pl.ds(tm + SUBLANES - 2, 2), :]
    zbuf_ref[pl.ds(SUBLANES - 2, 2), :] = tail
    tail_ref[0] = tail


def _in_proj_multi_kernel(x_ref, ov1_ref, ov2_ref, *refs, tm, seq_len):
    w_refs, out_refs = refs[:N_MIXER_W], refs[N_MIXER_W:N_MIXER_W + N_MIXER_OUT]
    z_ref, zbuf_ref = refs[N_MIXER_W + N_MIXER_OUT:]
    gate_b, z, sig_b, wconv_ref, wpb_ref, mb_ref = _mixer_front(x_ref, w_refs, out_refs, False)
    z_ref[...] = z
    zbuf_ref[pl.ds(0, SUBLANES), :] = jnp.zeros((SUBLANES, CONV_DIM), F32)
    zbuf_ref[pl.ds(SUBLANES, tm), :] = z
    t = lax.broadcasted_iota(I32, (tm, CONV_DIM), 0) & (seq_len - 1)
    z1 = jnp.where(t == 0, ov1_ref[...], zbuf_ref[pl.ds(SUBLANES - 1, tm), :])
    z2 = jnp.where(t < 2, ov2_ref[...], zbuf_ref[pl.ds(SUBLANES - 2, tm), :])
    _conv_out(gate_b, z, z1, z2, sig_b, wconv_ref, wpb_ref, mb_ref)


def _full(shape):
    n = len(shape)
    return pl.BlockSpec(shape, lambda *_: (0,) * n)


def _in_proj(x, conv_prev, weights, *, seq_len):
    g, wqkv, wf, bf, wc, wconv, wgl, wpb = weights
    n, d = x.shape
    w_specs = [_full(w.shape) for w in (g, wqkv, wf, bf, wc, wconv, wgl, wpb)]
    shapes = (((FOX_DIM,), BF16), ((FOX_DIM,), BF16), ((FOX_DIM,), BF16), ((HEADS, HEAD_DIM), F32), ((HEADS, HEAD_DIM), F32),
              ((HEADS,), F32), ((LANES,), F32), ((d,), BF16), ((d,), BF16))
    assert len(weights) == N_MIXER_W and len(shapes) == N_MIXER_OUT
    out_shape = [jax.ShapeDtypeStruct((n,) + s, t) for s, t in shapes]
    n_common = N_MIXER_OUT
    tm = TOKEN_TILE

    def out_specs(tile_index):
        return [pl.BlockSpec((tm,) + s, lambda *a, k=len(s): (tile_index(*a),) + (0,) * k) for s, _ in shapes]

    if seq_len % tm == 0:
        nb, nj = n // seq_len, seq_len // tm
        row = lambda b, j: (b * nj + j, 0)
        specs = out_specs(lambda b, j: b * nj + j)
        for i in (3, 4):
            out_shape[i] = jax.ShapeDtypeStruct((nb, HEADS, HEAD_DIM, seq_len), F32)
            specs[i] = pl.BlockSpec((1, HEADS, HEAD_DIM, tm), lambda b, j: (b, 0, 0, j))
        outs = pl.pallas_call(
            functools.partial(_in_proj_seq_kernel, tm=tm),
            out_shape=out_shape + [jax.ShapeDtypeStruct((nb, CONV_WIDTH - 1, CONV_DIM), F32)],
            grid=(nb, nj),
            in_specs=[pl.BlockSpec((tm, d), row), pl.BlockSpec((1, CONV_WIDTH - 1, CONV_DIM), lambda b, j: (b, 0, 0))] + w_specs,
            out_specs=specs + [pl.BlockSpec((1, CONV_WIDTH - 1, CONV_DIM), lambda b, j: (b, 0, 0))],
            scratch_shapes=[pltpu.VMEM((tm + SUBLANES, CONV_DIM), F32)],
            compiler_params=_cparams("arbitrary", "arbitrary"),
            name="in_proj_seq",
        )(x, conv_prev, g, wqkv, wf, bf, wc, wconv, wgl, wpb)
        return outs[:n_common], outs[n_common]
    assert seq_len & (seq_len - 1) == 0 and seq_len >= CONV_WIDTH - 1
    assert n % tm == 0 and tm % seq_len == 0
    first = jnp.zeros((n // seq_len, seq_len, CONV_DIM), F32)
    ov1 = first.at[:, 0].set(conv_prev[:, 1]).reshape(n, CONV_DIM)
    ov2 = first.at[:, 0].set(conv_prev[:, 0]).at[:, 1].set(conv_prev[:, 1]).reshape(n, CONV_DIM)
    row = lambda i: (i, 0)
    outs = pl.pallas_call(
        functools.partial(_in_proj_multi_kernel, tm=tm, seq_len=seq_len),
        out_shape=out_shape + [jax.ShapeDtypeStruct((n, CONV_DIM), F32)],
        grid=(n // tm,),
        in_specs=[pl.BlockSpec((tm, d), row), pl.BlockSpec((tm, CONV_DIM), row), pl.BlockSpec((tm, CONV_DIM), row)] + w_specs,
        out_specs=out_specs(lambda i: i) + [pl.BlockSpec((tm, CONV_DIM), row)],
        scratch_shapes=[pltpu.VMEM((tm + SUBLANES, CONV_DIM), F32)],
        compiler_params=_cparams("arbitrary"),
        name="in_proj_multi",
    )(x, ov1, ov2, g, wqkv, wf, bf, wc, wconv, wgl, wpb)
    tail = outs[n_common].reshape(n // seq_len, seq_len, CONV_DIM)[:, seq_len - (CONV_WIDTH - 1):]
    return outs[:n_common], tail


def _cumsum_kernel(lf_ref, c_ref, *, chunk, nchunk):
    r = lax.broadcasted_iota(I32, (chunk, chunk), 0)
    c = lax.broadcasted_iota(I32, (chunk, chunk), 1)
    upper = jnp.where(r <= c, 1.0, 0.0).astype(BF16)
    carry = jnp.zeros((HEADS, 1), F32)
    for n in range(nchunk):
        rest, terms = lf_ref[0, n], []
        for _ in range(C_TERMS):
            terms.append(rest.astype(BF16))
            rest = rest - terms[-1].astype(F32)
        sums = _dot(jnp.concatenate(terms, axis=0), upper)
        cs = carry
        for j in range(C_TERMS):
            cs = cs + sums[j * HEADS:(j + 1) * HEADS]
        c_ref[0, n] = cs
        carry = cs[:, chunk - 1:chunk]


def _cumsum_chunks(lf):
    b, nchunk, _, chunk = lf.shape
    spec = pl.BlockSpec((1, nchunk, HEADS, chunk), lambda i: (i, 0, 0, 0))
    return pl.pallas_call(
        functools.partial(_cumsum_kernel, chunk=chunk, nchunk=nchunk),
        out_shape=jax.ShapeDtypeStruct(lf.shape, F32),
        grid=(b,), in_specs=[spec], out_specs=spec,
        compiler_params=_cparams("arbitrary"),
        name="logf_cumsum",
    )(lf)


C_TERMS = 3


def _cumsum_rows_kernel(lf_ref, cs_ref, *, chunk, nchunk):
    r = lax.broadcasted_iota(I32, (chunk, chunk), 0)
    c = lax.broadcasted_iota(I32, (chunk, chunk), 1)
    lower = jnp.where(c <= r, 1.0, 0.0).astype(BF16)
    carry = jnp.zeros((1, LANES), F32)
    head_lanes = lax.broadcasted_iota(I32, (chunk, LANES), 1) < HEADS
    for n in range(nchunk):
        rest = lf_ref[pl.ds(n * chunk, chunk), :]
        stacked = jnp.zeros((chunk, LANES), F32)
        for j in range(C_TERMS):
            term = rest.astype(BF16).astype(F32)
            stacked = stacked + (term if j == 0 else pltpu.roll(term, j * HEADS, 1))
            rest = rest - term
        sums = _dot(lower, stacked.astype(BF16))
        cs = sums
        for j in range(1, C_TERMS):
            cs = cs + pltpu.roll(sums, LANES - j * HEADS, 1)
        cs = jnp.where(head_lanes, cs, 0.0) + carry
        carry = cs[chunk - 1:chunk, :]
        packed = jnp.zeros((chunk, LANES), F32)
        rest = cs * LOG2E
        for j in range(C_TERMS):
            term = rest.astype(BF16).astype(F32)
            packed = packed + (term if j == 0 else pltpu.roll(term, j * HEADS, 1))
            rest = rest - term
        cs_ref[pl.ds(n * chunk, chunk), :] = packed.astype(BF16)


def _cumsum_rows(lfp, *, seq):
    n = lfp.shape[0]
    spec = pl.BlockSpec((seq, LANES), lambda b: (b, 0))
    return pl.pallas_call(
        functools.partial(_cumsum_rows_kernel, chunk=TOKEN_TILE, nchunk=seq // TOKEN_TILE),
        out_shape=jax.ShapeDtypeStruct((n, LANES), BF16),
        grid=(n // seq,), in_specs=[spec], out_specs=spec,
        compiler_params=_cparams("arbitrary"),
        name="logf_cumsum_rows",
    )(lfp)


def _attn_prompt_kernel(q_ref, k_ref, v_ref, cs_ref, o_ref, ka_ref, vt_ref, m_ref, acc_ref, *, t, tk, nk):
    hp, i = pl.program_id(1), pl.program_id(2)
    rr = lax.broadcasted_iota(I32, (LANES, LANES), 0)
    cc = lax.broadcasted_iota(I32, (LANES, LANES), 1)
    sel = [jnp.where(jnp.where(cc < HEAD_DIM, rr - cc, -1) == HEAD_DIM * h, 1.0, 0.0).astype(BF16) for h in range(2)]

    @pl.when(i == 0)
    def _():
        kb = k_ref[...]
        cs = cs_ref[...]
        vt = v_ref[...].astype(F32).T
        row = lax.broadcasted_iota(I32, (SUBLANES, tk), 0)
        ones = jnp.where(row == 0, 1.0, 0.0)
        pad = jnp.zeros((LANES - HEAD_DIM - SUBLANES, tk), F32)
        for h in range(2):
            head = 2 * hp + h
            src = jnp.where(cc >= HEAD_DIM, (cc - HEAD_DIM) * HEADS + head, -1)
            src = jnp.where(cc < HEAD_DIM + C_TERMS, src, -1)
            place = jnp.where(rr == src, -1.0, 0.0).astype(BF16)
            ka_ref[h] = (_dot(kb, sel[h]) + _dot(cs, place)).astype(BF16)
            for n in range(nk):
                vh = vt[h * HEAD_DIM:(h + 1) * HEAD_DIM, n * tk:(n + 1) * tk]
                vt_ref[h, n] = jnp.concatenate([vh, ones, pad], axis=0).astype(BF16)

    lane = lax.broadcasted_iota(I32, (1, LANES), 1)
    one_lanes = jnp.where(jnp.logical_and(lane >= HEAD_DIM, lane < HEAD_DIM + C_TERMS), 1.0, 0.0)
    q = q_ref[...]
    qa = [(_dot(q, sel[h]) + one_lanes).astype(BF16) for h in range(2)]
    m_ref[...] = jnp.full(m_ref.shape, -jnp.inf, F32)
    acc_ref[...] = jnp.zeros(acc_ref.shape, F32)

    def scores(j):
        start = pl.multiple_of(j * tk, tk)
        return tuple(_dot_nt(ka_ref[h, pl.ds(start, tk), :], qa[h]) for h in range(2))

    assert tk == t

    def visible(st, limit):
        ki = lax.broadcasted_iota(I32, (tk, t), 0)
        qi = lax.broadcasted_iota(I32, (tk, t), 1)
        return jnp.where(ki <= qi + limit, st, -jnp.inf)

    def max_pass(j, limit):
        st_pair = scores(j)
        for h in range(2):
            st = st_pair[h] if limit is None else visible(st_pair[h], limit)
            m_ref[j + 1, h] = jnp.maximum(m_ref[j, h], jnp.max(st, axis=0, keepdims=True))

    def weight_pass(j, limit):
        st_pair = scores(j)
        for h in range(2):
            st = st_pair[h] if limit is None else visible(st_pair[h], limit)
            m_new = m_ref[j + 1, h]
            p = jnp.exp2(st - m_new).astype(BF16)
            acc_ref[h] = acc_ref[h] * jnp.exp2(m_ref[j, h] - m_new) + _dot(vt_ref[h, j], p)

    max_pass(0, i * t)

    def body(j, carry):
        max_pass(j + 1, None)
        weight_pass(j, None)
        return carry

    lax.fori_loop(0, i - 1, body, 0)

    @pl.when(i >= 1)
    def _():
        max_pass(i, 0)
        weight_pass(i - 1, None)

    weight_pass(i, 0)
    halves = []
    for h in range(2):
        acc = acc_ref[h]
        halves.append(acc[:HEAD_DIM] * (1.0 / acc[HEAD_DIM:HEAD_DIM + 1]))
    o_ref[...] = jnp.concatenate(halves, axis=0).T.astype(BF16)


def _attn_prompt(q, k, v, cs, *, nb, seq):
    t, tk = TOKEN_TILE, KEY_CHUNK
    nq, nk = seq // t, seq // tk
    return pl.pallas_call(
        functools.partial(_attn_prompt_kernel, t=t, tk=tk, nk=nk),
        out_shape=jax.ShapeDtypeStruct(q.shape, BF16),
        grid=(nb, HEADS // 2, nq),
        in_specs=[pl.BlockSpec((t, LANES), lambda b, hp, i: (b * nq + i, hp)),
                  pl.BlockSpec((seq, LANES), lambda b, hp, i: (b, hp)),
                  pl.BlockSpec((seq, LANES), lambda b, hp, i: (b, hp)),
                  pl.BlockSpec((seq, LANES), lambda b, hp, i: (b, 0))],
        out_specs=pl.BlockSpec((t, LANES), lambda b, hp, i: (b * nq + i, hp)),
        scratch_shapes=[pltpu.VMEM((2, seq, LANES), BF16), pltpu.VMEM((2, nk, LANES, tk), BF16),
                        pltpu.VMEM((nk + 1, 2, 1, t), F32), pltpu.VMEM((2, LANES, t), F32)],
        compiler_params=_cparams("arbitrary", "arbitrary", "arbitrary"),
        name="attn_prompt",
    )(q, k, v, cs)


def _attn_sample_kernel(q_ref, kc_ref, vc_ref, kn_ref, vn_ref, c_ref, o_ref,
                        qbd_ref, m_ref, l_ref, acc_ref, kpad_ref, vpad_ref, *, nq, nchunk):
    j = pl.program_id(1)
    rows = HEADS * nq
    row_head = lax.broadcasted_iota(I32, (rows, FOX_DIM), 0) >> (nq.bit_length() - 1)
    col_head = lax.broadcasted_iota(I32, (rows, FOX_DIM), 1) >> (HEAD_DIM.bit_length() - 1)
    own = row_head == col_head

    @pl.when(j == 0)
    def _():
        qt = jnp.concatenate([q_ref[...]] * HEADS, axis=0)
        qbd_ref[...] = jnp.where(own, qt, jnp.zeros_like(qt))
        m_ref[...] = jnp.full(m_ref.shape, -jnp.inf, F32)
        l_ref[...] = jnp.zeros(l_ref.shape, F32)
        acc_ref[...] = jnp.zeros(acc_ref.shape, F32)

    def update(s, cvals, visible, weighted_values):
        width = s.shape[1]
        bias = jnp.concatenate([jnp.broadcast_to(cvals[h:h + 1, :], (nq, width)) for h in range(HEADS)], axis=0)
        s = s - LOG2E * bias
        if visible is not None:
            s = jnp.where(visible, s, -jnp.inf)
        m_prev = m_ref[...]
        m_new = jnp.maximum(m_prev, jnp.max(s, axis=-1, keepdims=True))
        a = jnp.exp2(m_prev - m_new)
        p = jnp.exp2(s - m_new)
        l_ref[...] = a * l_ref[...] + jnp.sum(p, axis=-1, keepdims=True)
        m_ref[...] = m_new
        acc_ref[...] = acc_ref[...] * a + weighted_values(p.astype(BF16))

    @pl.when(j < nchunk)
    def _():
        chunk = kc_ref.shape[3]
        kt = kc_ref[0].reshape(FOX_DIM, chunk).astype(BF16)
        vt = vc_ref[0].reshape(FOX_DIM, chunk).astype(BF16)
        update(_dot(qbd_ref[...], kt), c_ref[0, j], None, lambda p: _dot_nt(p, vt))

    @pl.when(j == nchunk)
    def _():
        kpad_ref[...] = jnp.zeros(kpad_ref.shape, BF16)
        vpad_ref[...] = jnp.zeros(vpad_ref.shape, BF16)
        kpad_ref[pl.ds(0, nq), :] = kn_ref[...]
        vpad_ref[pl.ds(0, nq), :] = vn_ref[...]
        ki = lax.broadcasted_iota(I32, (rows, LANES), 1)
        qi = lax.broadcasted_iota(I32, (rows, LANES), 0) & (nq - 1)
        update(_dot_nt(qbd_ref[...], kpad_ref[...]), c_ref[0, nchunk][:, :LANES], ki <= qi,
               lambda p: _dot(p, vpad_ref[...]))
        out = jnp.where(own, acc_ref[...] * (1.0 / l_ref[...]), 0.0)
        o = out[0:nq]
        for h in range(1, HEADS):
            o = o + out[h * nq:(h + 1) * nq]
        o_ref[...] = o.astype(BF16)


def _attn_sample(q, k_cache, v_cache, k_new, v_new, c, *, nq):
    nb, past = k_cache.shape[0], k_cache.shape[3]
    chunk = c.shape[-1]
    nchunk = past // chunk
    assert nq & (nq - 1) == 0 and nq <= LANES and past % chunk == 0 and c.shape[1] == nchunk + 1
    rows = HEADS * nq
    cache_spec = pl.BlockSpec((1, HEADS, HEAD_DIM, chunk), lambda b, j: (b, 0, 0, jnp.minimum(j, nchunk - 1)))
    new_spec = pl.BlockSpec((nq, FOX_DIM), lambda b, j: (b, 0))
    return pl.pallas_call(
        functools.partial(_attn_sample_kernel, nq=nq, nchunk=nchunk),
        out_shape=jax.ShapeDtypeStruct(q.shape, BF16),
        grid=(nb, nchunk + 1),
        in_specs=[new_spec, cache_spec, cache_spec, new_spec, new_spec,
                  pl.BlockSpec((1, nchunk + 1, HEADS, chunk), lambda b, j: (b, 0, 0, 0))],
        out_specs=new_spec,
        scratch_shapes=[pltpu.VMEM((rows, FOX_DIM), BF16), pltpu.VMEM((rows, 1), F32), pltpu.VMEM((rows, 1), F32),
                        pltpu.VMEM((rows, FOX_DIM), F32), pltpu.VMEM((LANES, FOX_DIM), BF16),
                        pltpu.VMEM((LANES, FOX_DIM), BF16)],
        compiler_params=_cparams("arbitrary", "arbitrary"),
        name="attn_sample",
    )(q, k_cache, v_cache, k_new, v_new, c)


def _pick(is_first, a_ref, b_ref):
    return jnp.where(is_first, a_ref[...], b_ref[...])


def _post_attn_kernel(xp_ref, xs_ref, op_ref, os_ref, sap_ref, sas_ref, mbp_ref, mbs_ref,
                      wpa_ref, wo_ref, g_ref, wrt_ref, br_ref,
                      h1_ref, xn_ref, pos_ref, gate_ref, cnt_ref, *, n_first, tm):
    is_p = pl.program_id(0) < n_first
    ya = _dot(_pick(is_p, op_ref, os_ref), wpa_ref[...])
    merged = _pick(is_p, sap_ref, sas_ref).astype(F32) * ya + _pick(is_p, mbp_ref, mbs_ref).astype(F32)
    h1 = _pick(is_p, xp_ref, xs_ref) + _dot(merged.astype(BF16), wo_ref[...])
    h1_ref[...] = h1
    xn = _rms(h1, g_ref[...]).astype(BF16)
    xn_ref[...] = xn

    lt = _dot_nt(wrt_ref[...], xn) + br_ref[...]
    eio = lax.broadcasted_iota(I32, (N_EXPERTS, tm), 0).astype(F32)
    vals, hots = [], []
    for _ in range(TOP_K):
        m = jnp.max(lt, axis=0, keepdims=True)
        idx = jnp.min(jnp.where(lt == m, eio, float(N_EXPERTS)), axis=0, keepdims=True)
        hot = eio == idx
        vals.append(m)
        hots.append(hot)
        lt = jnp.where(hot, -jnp.inf, lt)
    ex = [jnp.exp(v - vals[0]) for v in vals]
    den = ex[0] + ex[1] + ex[2] + ex[3]
    gate_ref[...] = jnp.concatenate([e / den for e in ex], axis=0)

    chosen = jnp.zeros((N_EXPERTS, tm), F32)
    for hot in hots:
        chosen = jnp.where(hot, 1.0, chosen)
    td = DISPATCH_TILE
    r = lax.broadcasted_iota(I32, (td, td), 0)
    c = lax.broadcasted_iota(I32, (td, td), 1)
    before = jnp.where(r < c, 1.0, 0.0).astype(BF16)
    er = lax.broadcasted_iota(I32, (N_EXPERTS, N_EXPERTS), 0)
    ec = lax.broadcasted_iota(I32, (N_EXPERTS, N_EXPERTS), 1)
    lower = jnp.where(ec < er, 1.0, 0.0).astype(BF16)
    for sub in range(tm // td):
        sl = slice(sub * td, (sub + 1) * td)
        ch = chosen[:, sl]
        rank = _dot(ch.astype(BF16), before)
        cnt = rank[:, td - 1:td] + ch[:, td - 1:td]
        units = jnp.floor((cnt + (ROW_ALIGN - 1)) * (1.0 / ROW_ALIGN))
        start = ROW_ALIGN * _dot(lower, jnp.broadcast_to(units, (N_EXPERTS, td)).astype(BF16))
        base = start + rank
        pos = [jnp.sum(jnp.where(hot[:, sl], base, 0.0), axis=0, keepdims=True) for hot in hots]
        pos_ref[:, sl] = jnp.concatenate(pos, axis=0).astype(I32)
        cnt_ref[sub] = jnp.broadcast_to(cnt, (N_EXPERTS, LANES)).astype(I32)


def _post_attn(xp, xs, op, os_, sap, sas, mbp, mbs, wpa, wo, g, wrt, br):
    n_p, d = xp.shape
    n_s = xs.shape[0]
    tm = TOKEN_TILE
    assert n_p % tm == 0 and n_s % tm == 0
    n_first, n_tiles = n_p // tm, (n_p + n_s) // tm
    n = n_p + n_s
    sub = tm // DISPATCH_TILE
    first = lambda i: (jnp.minimum(i, n_first - 1), 0)
    second = lambda i: (jnp.maximum(i - n_first, 0), 0)
    row = lambda i: (i, 0)
    col = lambda i: (0, i)

    def pair(width):
        return [pl.BlockSpec((tm, width), first), pl.BlockSpec((tm, width), second)]

    return pl.pallas_call(
        functools.partial(_post_attn_kernel, n_first=n_first, tm=tm),
        out_shape=[jax.ShapeDtypeStruct((n, d), F32), jax.ShapeDtypeStruct((n, d), BF16),
                   jax.ShapeDtypeStruct((TOP_K, n), I32), jax.ShapeDtypeStruct((TOP_K, n), F32),
                   jax.ShapeDtypeStruct((n // DISPATCH_TILE, N_EXPERTS, LANES), I32)],
        grid=(n_tiles,),
        in_specs=pair(d) + pair(FOX_DIM) + pair(d) + pair(d) + [_full(w.shape) for w in (wpa, wo, g, wrt, br)],
        out_specs=[pl.BlockSpec((tm, d), row), pl.BlockSpec((tm, d), row),
                   pl.BlockSpec((TOP_K, tm), col), pl.BlockSpec((TOP_K, tm), col),
                   pl.BlockSpec((sub, N_EXPERTS, LANES), lambda i: (i, 0, 0))],
        compiler_params=_cparams("arbitrary"),
        name="post_attn_router",
    )(xp, xs, op, os_, sap, sas, mbp, mbs, wpa, wo, g, wrt, br)


def _for_each_run(tile, len_ref, loc_ref, dst_ref, fn):
    def body(e, carry):
        idx = tile * N_EXPERTS + e
        n, loc, dst = len_ref[idx], loc_ref[idx], dst_ref[idx]

        for c in RUN_CHUNKS:
            off = n & ~(2 * c - 1)

            @pl.when((n & c) != 0)
            def _():
                fn(loc + off, dst + off, c)
        return carry

    for e in range(N_EXPERTS):
        body(e, 0)


TILE_CHUNKS = tuple(1 << b for b in reversed(range(LOCAL_GROUPS.bit_length())))


def _wait_tile_rows(tile, len_ref, loc_ref, wait_rows):
    last = tile * N_EXPERTS + N_EXPERTS - 1
    total = loc_ref[last] + len_ref[last]
    for c in TILE_CHUNKS:
        @pl.when((total & c) != 0)
        def _():
            wait_rows(c)


def _dispatch_kernel(len_ref, loc_ref, dst_ref, xn_ref, pos_ref, xs_hbm, xloc_ref, sem_ref, *, n_tiles):
    i = pl.program_id(0)
    slot = lax.rem(i, 2)

    def copy(s, loc, dst, groups):
        return pltpu.make_async_copy(xloc_ref.at[s, pl.ds(loc, groups)], xs_hbm.at[pl.ds(dst, groups)], sem_ref.at[s])

    def start_tile(tile, s):
        _for_each_run(tile, len_ref, loc_ref, dst_ref, lambda loc, dst, rows: copy(s, loc, dst, rows).start())

    def wait_tile(tile, s):
        _wait_tile_rows(tile, len_ref, loc_ref, lambda rows: copy(s, 0, 0, rows).wait())

    pos = pos_ref[...]
    aio = lax.broadcasted_iota(I32, (LOCAL_ROWS, DISPATCH_TILE), 0)
    sel = jnp.zeros((LOCAL_ROWS, DISPATCH_TILE), F32)
    for k in range(TOP_K):
        sel = jnp.where(aio == pos[k:k + 1, :], 1.0, sel)
    rows_sorted = _dot(sel.astype(BF16), xn_ref[...])

    @pl.when(i >= 2)
    def _():
        wait_tile(i - 2, slot)

    xloc_ref[slot] = rows_sorted.reshape(xloc_ref.shape[1:])
    start_tile(i, slot)

    @pl.when(i == n_tiles - 1)
    def _():
        if n_tiles >= 2:
            wait_tile(i - 1, 1 - slot)
        wait_tile(i, slot)


def _dispatch(run_len, run_loc, run_dst, xn, pos, *, total_rows):
    n, d = xn.shape
    td = DISPATCH_TILE
    n_tiles = n // td
    return pl.pallas_call(
        functools.partial(_dispatch_kernel, n_tiles=n_tiles),
        out_shape=jax.ShapeDtypeStruct((total_rows // ROW_ALIGN, ROW_ALIGN, d), F32),
        grid_spec=pltpu.PrefetchScalarGridSpec(
            num_scalar_prefetch=3, grid=(n_tiles,),
            in_specs=[pl.BlockSpec((td, d), lambda i, *_: (i, 0)), pl.BlockSpec((TOP_K, td), lambda i, *_: (0, i))],
            out_specs=pl.BlockSpec(memory_space=pl.ANY),
            scratch_shapes=[pltpu.VMEM((2, LOCAL_GROUPS, ROW_ALIGN, d), F32), pltpu.SemaphoreType.DMA((2,))]),
        compiler_params=_cparams("arbitrary"),
        name="moe_dispatch",
    )(run_len, run_loc, run_dst, xn, pos)


def _ffn_kernel(bexp_ref, nval_ref, last_ref, xs_ref, wgu_ref, bgu_ref, wdn_ref, bdn_ref, o_ref, wgu_s, wdn_s):
    i = pl.program_id(0)
    nv = nval_ref[i]

    @pl.when(jnp.logical_or(i == 0, bexp_ref[i] != bexp_ref[jnp.maximum(i - 1, 0)]))
    def _():
        wgu_s[...] = wgu_ref[0].astype(BF16)
        wdn_s[...] = wdn_ref[0].astype(BF16)

    for r in range(FFN_ROWS // FFN_SUB):
        @pl.when(nv > r * FFN_SUB)
        def _():
            rows = lax.broadcasted_iota(I32, (FFN_SUB, xs_ref.shape[1]), 0) + r * FFN_SUB
            x = jnp.where(rows < nv, xs_ref[pl.ds(r * FFN_SUB, FFN_SUB), :], 0.0).astype(BF16)
            gu = _dot(x, wgu_s[...]) + bgu_ref[0]
            gt = jnp.minimum(gu[:, :D_FF], SWIGLU_LIMIT)
            up = jnp.clip(gu[:, D_FF:], -SWIGLU_LIMIT, SWIGLU_LIMIT)
            act = (up + 1.0) * gt * _sigmoid(SWIGLU_ALPHA * gt)
            o_ref[pl.ds(r * FFN_SUB, FFN_SUB), :] = _dot(act.astype(BF16), wdn_s[...]) + bdn_ref[0]


def _ffn(bexp, nval, last, xs, wgu, bgu, wdn, bdn):
    rows, d = xs.shape
    nblk = rows // FFN_ROWS
    blk = lambda i, bexp, nval, last: (jnp.minimum(i, last[0]), 0)
    exp3 = lambda i, bexp, nval, last: (bexp[i], 0, 0)
    return pl.pallas_call(
        _ffn_kernel,
        out_shape=jax.ShapeDtypeStruct((rows, d), F32),
        grid_spec=pltpu.PrefetchScalarGridSpec(
            num_scalar_prefetch=3, grid=(nblk,),
            in_specs=[pl.BlockSpec((FFN_ROWS, d), blk),
                      pl.BlockSpec((1, d, 2 * D_FF), exp3), pl.BlockSpec((1, 1, 2 * D_FF), exp3),
                      pl.BlockSpec((1, D_FF, d), exp3), pl.BlockSpec((1, 1, d), exp3)],
            out_specs=pl.BlockSpec((FFN_ROWS, d), blk),
            scratch_shapes=[pltpu.VMEM((d, 2 * D_FF), BF16), pltpu.VMEM((D_FF, d), BF16)]),
        compiler_params=_cparams("arbitrary"),
        name="moe_ffn",
    )(bexp, nval, last, xs, wgu, bgu, wdn, bdn)


def _combine_kernel(len_ref, loc_ref, dst_ref, os_hbm, pos_ref, gate_ref, h1_ref, pp_ref, ps_ref,
                    gple_ref, wpg_ref, wpp_ref, gfin_ref, yp_ref, ys_ref, oloc_ref, sem_ref, *, n_tiles, n_first):
    i = pl.program_id(0)
    slot = lax.rem(i, 2)

    def copy(s, loc, dst, groups):
        return pltpu.make_async_copy(os_hbm.at[pl.ds(dst, groups)], oloc_ref.at[s, pl.ds(loc, groups)], sem_ref.at[s])

    def start_tile(tile, s):
        _for_each_run(tile, len_ref, loc_ref, dst_ref, lambda loc, dst, rows: copy(s, loc, dst, rows).start())

    def wait_tile(tile, s):
        _wait_tile_rows(tile, len_ref, loc_ref, lambda rows: copy(s, 0, 0, rows).wait())

    @pl.when(i == 0)
    def _():
        oloc_ref[...] = jnp.zeros(oloc_ref.shape, F32)
        start_tile(0, 0)

    @pl.when(i + 1 < n_tiles)
    def _():
        start_tile(i + 1, 1 - slot)

    wait_tile(i, slot)

    pos, gate = pos_ref[...], gate_ref[...]
    lio = lax.broadcasted_iota(I32, (DISPATCH_TILE, LOCAL_ROWS), 1)
    w = jnp.zeros((DISPATCH_TILE, LOCAL_ROWS), F32)
    for k in range(TOP_K):
        w = jnp.where(lio == pos[:, k:k + 1], gate[:, k:k + 1], w)
    h2 = h1_ref[...] + _dot(w.astype(BF16), oloc_ref[slot].reshape(LOCAL_ROWS, -1).astype(BF16))

    is_p = i < n_first
    xn = _rms(h2, gple_ref[...]).astype(BF16)
    ple = _dot(_pick(is_p, pp_ref, ps_ref).astype(BF16), wpp_ref[...])
    h3 = h2 + _sigmoid(_dot(xn, wpg_ref[...])) * ple
    y = _rms(h3, gfin_ref[...])

    @pl.when(is_p)
    def _():
        yp_ref[...] = y

    @pl.when(jnp.logical_not(is_p))
    def _():
        ys_ref[...] = y


def _combine(run_len, run_loc, run_dst, os_, pos, gate, h1, pp, ps, gple, wpg, wpp, gfin):
    n, d = h1.shape
    td = DISPATCH_TILE
    n_p, n_s = pp.shape[0], ps.shape[0]
    assert n_p % td == 0 and n_s % td == 0
    n_tiles, n_first = n // td, n_p // td
    pw = pp.shape[1]
    first = lambda i, *_: (jnp.minimum(i, n_first - 1), 0)
    second = lambda i, *_: (jnp.maximum(i - n_first, 0), 0)
    row = lambda i, *_: (i, 0)
    full2 = lambda i, *_: (0, 0)
    return pl.pallas_call(
        functools.partial(_combine_kernel, n_tiles=n_tiles, n_first=n_first),
        out_shape=[jax.ShapeDtypeStruct((n_p, d), F32), jax.ShapeDtypeStruct((n_s, d), F32)],
        grid_spec=pltpu.PrefetchScalarGridSpec(
            num_scalar_prefetch=3, grid=(n_tiles,),
            in_specs=[pl.BlockSpec(memory_space=pl.ANY),
                      pl.BlockSpec((td, TOP_K), row), pl.BlockSpec((td, TOP_K), row), pl.BlockSpec((td, d), row),
                      pl.BlockSpec((td, pw), first), pl.BlockSpec((td, pw), second),
                      pl.BlockSpec(gple.shape, full2), pl.BlockSpec(wpg.shape, full2),
                      pl.BlockSpec(wpp.shape, full2), pl.BlockSpec(gfin.shape, full2)],
            out_specs=[pl.BlockSpec((td, d), first), pl.BlockSpec((td, d), second)],
            scratch_shapes=[pltpu.VMEM((2, LOCAL_GROUPS, ROW_ALIGN, d), F32), pltpu.SemaphoreType.DMA((2,))]),
        compiler_params=_cparams("arbitrary"),
        name="moe_combine_ple",
    )(run_len, run_loc, run_dst, os_, pos, gate, h1, pp, ps, gple, wpg, wpp, gfin)


def _routing_tables(cnt, total_rows):
    n_tiles = cnt.shape[0]
    run_len = (cnt + (ROW_ALIGN - 1)) // ROW_ALIGN * ROW_ALIGN
    run_loc = jnp.cumsum(run_len, axis=1) - run_len
    per_expert = jnp.sum(run_len, axis=0)
    region = (per_expert + (FFN_ROWS - 1)) // FFN_ROWS * FFN_ROWS
    region_end = jnp.cumsum(region)
    region_start = region_end - region
    run_dst = region_start[None, :] + jnp.cumsum(run_len, axis=0) - run_len
    nblk = total_rows // FFN_ROWS
    blk_row = jnp.arange(nblk, dtype=I32) * FFN_ROWS
    used = region_end[-1] // FFN_ROWS
    last = jnp.maximum(used - 1, 0)
    bexp = jnp.minimum(jnp.sum(blk_row[:, None] >= region_end[None, :], axis=1), N_EXPERTS - 1).astype(I32)
    nval = jnp.clip(per_expert[bexp] - (blk_row - region_start[bexp]), 0, FFN_ROWS)
    nval = jnp.where(blk_row < region_end[-1], nval, 0).astype(I32)
    bexp = jnp.where(blk_row < region_end[-1], bexp, bexp[last])
    flat = lambda a: (a // ROW_ALIGN).reshape(n_tiles * N_EXPERTS).astype(I32)
    return flat(run_len), flat(run_loc), flat(run_dst), bexp, nval, last.reshape(1).astype(I32)


def _chunked_t(lf, chunk):
    b, length, h = lf.shape
    return lf.reshape(b, length // chunk, chunk, h).transpose(0, 1, 3, 2)


def kernel(x_prompt, x_sample, p_prompt, p_sample, cache_k, cache_v, cache_logf, state_conv, g_mix, w_in, b_f, w_conv, w_pa, w_pb, w_o, g_ffn, w_router, b_router, w_gu, b_gu, w_dn, b_dn, g_ple, w_ple_gate, w_ple_proj, g_final):
    depth = g_mix.shape[0]
    assert depth == 1
    nb, seq, d = x_prompt.shape
    db, dseq, _ = x_sample.shape
    past = cache_k.shape[2]
    n_p, n_s = nb * seq, db * dseq

    w = w_in[0].astype(BF16)
    o_f = 3 * FOX_DIM
    o_c = o_f + HEADS
    o_g = o_c + 3 * CONV_DIM
    wf = jnp.pad(w[:, o_f:o_c], ((0, 0), (0, LANES - HEADS)))
    bf = jnp.pad(b_f[0], (0, LANES - HEADS)).reshape(1, LANES)
    mixer_w = (g_mix[0].reshape(1, d), w[:, :o_f], wf, bf, w[:, o_c:o_g], w_conv[0], w[:, o_g:], w_pb[0].astype(BF16))

    (q_p, k_p, v_p, ko_p, vo_p, logf_p, lfp_p, sa_p, mb_p), tail_p = _in_proj(
        x_prompt.reshape(n_p, d), jnp.zeros((nb, CONV_WIDTH - 1, CONV_DIM), F32), mixer_w, seq_len=seq)
    o_p = _attn_prompt(q_p, k_p, v_p, _cumsum_rows(lfp_p, seq=seq), nb=nb, seq=seq)

    (q_s, k_s, v_s, ko_s, vo_s, logf_s, _, sa_s, mb_s), tail_s = _in_proj(
        x_sample.reshape(n_s, d), state_conv[0], mixer_w, seq_len=dseq)
    chunk = min(past, 2 * TOKEN_TILE)
    lf_all = jnp.concatenate([cache_logf[0].astype(F32), logf_s.reshape(db, dseq, HEADS),
                              jnp.zeros((db, chunk - dseq, HEADS), F32)], axis=1)
    c_s = _cumsum_chunks(_chunked_t(lf_all, chunk))
    o_s = _attn_sample(q_s, cache_k[0].transpose(0, 2, 3, 1), cache_v[0].transpose(0, 2, 3, 1), k_s, v_s, c_s, nq=dseq)

    h1, xn2, pos_t, gate_t, cnt = _post_attn(
        x_prompt.reshape(n_p, d), x_sample.reshape(n_s, d), o_p, o_s, sa_p, sa_s, mb_p, mb_s,
        w_pa[0].astype(BF16), w_o[0].astype(BF16), g_ffn[0].reshape(1, d),
        w_router[0].T.astype(BF16), b_router[0].reshape(N_EXPERTS, 1))

    n = n_p + n_s
    n_tiles = n // DISPATCH_TILE
    max_rows = n * TOP_K + n_tiles * N_EXPERTS * (ROW_ALIGN - 1) + N_EXPERTS * (FFN_ROWS - 1)
    total_rows = -(-max_rows // FFN_ROWS) * FFN_ROWS
    run_len, run_loc, run_dst, bexp, nval, last = _routing_tables(cnt[:, :, 0], total_rows)
    xs = _dispatch(run_len, run_loc, run_dst, xn2, pos_t, total_rows=total_rows)
    os_ = _ffn(bexp, nval, last, xs.reshape(total_rows, d), w_gu[0], b_gu[0].reshape(N_EXPERTS, 1, 2 * D_FF),
               w_dn[0], b_dn[0].reshape(N_EXPERTS, 1, d))

    y_p, y_s = _combine(run_len, run_loc, run_dst, os_.reshape(total_rows // ROW_ALIGN, ROW_ALIGN, d), pos_t.T, gate_t.T, h1,
                        p_prompt[0].reshape(n_p, -1), p_sample[0].reshape(n_s, -1),
                        g_ple[0].reshape(1, d), w_ple_gate[0].astype(BF16), w_ple_proj[0].astype(BF16),
                        g_final.reshape(1, d))

    return (y_p.reshape(nb, seq, d), y_s.reshape(db, dseq, d),
            ko_p.transpose(0, 3, 1, 2)[None], vo_p.transpose(0, 3, 1, 2)[None],
            logf_p.reshape(1, nb, seq, HEADS), tail_p[None],
            ko_s.reshape(1, db, dseq, HEADS, HEAD_DIM), vo_s.reshape(1, db, dseq, HEADS, HEAD_DIM),
            logf_s.reshape(1, db, dseq, HEADS), tail_s[None])
```

```python
import functools

import jax
import jax.numpy as jnp
from jax import lax
from jax.experimental import pallas as pl
from jax.experimental.pallas import tpu as pltpu

F32, BF16, I32 = jnp.float32, jnp.bfloat16, jnp.int32

HEADS = 8
HEAD_DIM = 64
FOX_DIM = HEADS * HEAD_DIM
CONV_DIM = 512
CONV_WIDTH = 3
N_EXPERTS = 32
TOP_K = 4
D_FF = 1024
SWIGLU_ALPHA = 1.702
SWIGLU_LIMIT = 7.0
RMS_EPS = 1e-6
LOG2E = 1.4426950408889634

LANES = 128
SUBLANES = 8
TOKEN_TILE = 512
KEY_CHUNK = 512
DISPATCH_TILE = 256
ROW_ALIGN = SUBLANES
LOCAL_ROWS = DISPATCH_TILE * TOP_K + N_EXPERTS * ROW_ALIGN
FFN_ROWS = 1024
FFN_SUB = 512
LOCAL_GROUPS = LOCAL_ROWS // ROW_ALIGN
RUN_CHUNKS = (32, 16, 8, 4, 2, 1)
VMEM_LIMIT_BYTES = 56 * 1024 * 1024


def _cparams(*sem):
    return pltpu.CompilerParams(dimension_semantics=sem, vmem_limit_bytes=VMEM_LIMIT_BYTES)


def _rms(x, g):
    return x * lax.rsqrt(jnp.mean(x * x, axis=-1, keepdims=True) + RMS_EPS) * g


def _sigmoid(x):
    return 1.0 / (1.0 + jnp.exp(-x))


def _log_sigmoid(x):
    return jnp.minimum(x, 0.0) - jnp.log1p(jnp.exp(-jnp.abs(x)))


def _dot(a, b):
    return jnp.dot(a, b, preferred_element_type=F32)


def _dot_nt(a, b):
    return lax.dot_general(a, b, (((1,), (1,)), ((), ())), preferred_element_type=F32)


def _mixer_inputs(x_ref, g_ref, wqkv_ref, wf_ref, bf_ref, wc_ref, wgl_ref,
                  q_ref, k_ref, v_ref, ko_ref, vo_ref, logf_ref, lfp_ref, sa_ref, *, keys_minor):
    xn = _rms(x_ref[...], g_ref[...]).astype(BF16)
    qkv = _dot(xn, wqkv_ref[...])
    q_ref[...] = (qkv[:, :FOX_DIM] * (HEAD_DIM ** -0.5 * LOG2E)).astype(BF16)
    k = qkv[:, FOX_DIM:2 * FOX_DIM]
    v = qkv[:, 2 * FOX_DIM:]
    k_ref[...] = k.astype(BF16)
    v_ref[...] = v.astype(BF16)
    if keys_minor:
        ko_ref[0] = k.T.reshape(HEADS, HEAD_DIM, k.shape[0])
        vo_ref[0] = v.T.reshape(HEADS, HEAD_DIM, v.shape[0])
    else:
        for h in range(HEADS):
            ko_ref[:, h, :] = k[:, h * HEAD_DIM:(h + 1) * HEAD_DIM]
            vo_ref[:, h, :] = v[:, h * HEAD_DIM:(h + 1) * HEAD_DIM]
    logf = _log_sigmoid(_dot(xn, wf_ref[...]) + bf_ref[...])
    logf_ref[...] = logf[:, :HEADS]
    lane = lax.broadcasted_iota(I32, logf.shape, 1)
    lfp_ref[...] = jnp.where(lane < HEADS, logf, 0.0)
    c3 = _dot(xn, wc_ref[...])
    gate_b = c3[:, :CONV_DIM]
    z = c3[:, CONV_DIM:2 * CONV_DIM] * c3[:, 2 * CONV_DIM:]
    gl = _dot(xn, wgl_ref[...])
    d = gl.shape[1] // 2
    sa_ref[...] = _sigmoid(gl[:, :d]).astype(BF16)
    return gate_b, z, _sigmoid(gl[:, d:])


def _conv_out(gate_b, z, z1, z2, sig_b, wconv_ref, wpb_ref, mb_ref):
    zc = wconv_ref[0:1, :] * z2 + wconv_ref[1:2, :] * z1 + wconv_ref[2:3, :] * z
    yb = _dot((gate_b * zc).astype(BF16), wpb_ref[...])
    mb_ref[...] = (sig_b * yb).astype(BF16)


N_MIXER_W = 8
N_MIXER_OUT = 9


def _mixer_front(x_ref, w_refs, out_refs, keys_minor):
    g_ref, wqkv_ref, wf_ref, bf_ref, wc_ref, wconv_ref, wgl_ref, wpb_ref = w_refs
    gate_b, z, sig_b = _mixer_inputs(x_ref, g_ref, wqkv_ref, wf_ref, bf_ref, wc_ref, wgl_ref, *out_refs[:-1],
                                     keys_minor=keys_minor)
    return gate_b, z, sig_b, wconv_ref, wpb_ref, out_refs[-1]


def _in_proj_seq_kernel(x_ref, prev_ref, *refs, tm):
    w_refs, out_refs = refs[:N_MIXER_W], refs[N_MIXER_W:N_MIXER_W + N_MIXER_OUT]
    tail_ref, zbuf_ref = refs[N_MIXER_W + N_MIXER_OUT:]
    gate_b, z, sig_b, wconv_ref, wpb_ref, mb_ref = _mixer_front(x_ref, w_refs, out_refs, True)
    zbuf_ref[pl.ds(SUBLANES, tm), :] = z

    @pl.when(pl.program_id(1) == 0)
    def _():
        zbuf_ref[pl.ds(SUBLANES - 2, 2), :] = prev_ref[0]

    z1 = zbuf_ref[pl.ds(SUBLANES - 1, tm), :]
    z2 = zbuf_ref[pl.ds(SUBLANES - 2, tm), :]
    _conv_out(gate_b, z, z1, z2, sig_b, wconv_ref, wpb_ref, mb_ref)
    tail = zbuf_ref[pl.ds(tm + SUBLANES - 2, 2), :]
    zbuf_ref[pl.ds(SUBLANES - 2, 2), :] = tail
    tail_ref[0] = tail


def _in_proj_multi_kernel(x_ref, ov1_ref, ov2_ref, *refs, tm, seq_len):
    w_refs, out_refs = refs[:N_MIXER_W], refs[N_MIXER_W:N_MIXER_W + N_MIXER_OUT]
    z_ref, zbuf_ref = refs[N_MIXER_W + N_MIXER_OUT:]
    gate_b, z, sig_b, wconv_ref, wpb_ref, mb_ref = _mixer_front(x_ref, w_refs, out_refs, False)
    z_ref[...] = z
    zbuf_ref[pl.ds(0, SUBLANES), :] = jnp.zeros((SUBLANES, CONV_DIM), F32)
    zbuf_ref[pl.ds(SUBLANES, tm), :] = z
    t = lax.broadcasted_iota(I32, (tm, CONV_DIM), 0) & (seq_len - 1)
    z1 = jnp.where(t == 0, ov1_ref[...], zbuf_ref[pl.ds(SUBLANES - 1, tm), :])
    z2 = jnp.where(t < 2, ov2_ref[...], zbuf_ref[pl.ds(SUBLANES - 2, tm), :])
    _conv_out(gate_b, z, z1, z2, sig_b, wconv_ref, wpb_ref, mb_ref)


def _full(shape):
    n = len(shape)
    return pl.BlockSpec(shape, lambda *_: (0,) * n)


def _in_proj(x, conv_prev, weights, *, seq_len):
    g, wqkv, wf, bf, wc, wconv, wgl, wpb = weights
    n, d = x.shape
    w_specs = [_full(w.shape) for w in (g, wqkv, wf, bf, wc, wconv, wgl, wpb)]
    shapes = (((FOX_DIM,), BF16), ((FOX_DIM,), BF16), ((FOX_DIM,), BF16), ((HEADS, HEAD_DIM), F32), ((HEADS, HEAD_DIM), F32),
              ((HEADS,), F32), ((LANES,), F32), ((d,), BF16), ((d,), BF16))
    assert len(weights) == N_MIXER_W and len(shapes) == N_MIXER_OUT
    out_shape = [jax.ShapeDtypeStruct((n,) + s, t) for s, t in shapes]
    n_common = N_MIXER_OUT
    tm = TOKEN_TILE

    def out_specs(tile_index):
        return [pl.BlockSpec((tm,) + s, lambda *a, k=len(s): (tile_index(*a),) + (0,) * k) for s, _ in shapes]

    if seq_len % tm == 0:
        nb, nj = n // seq_len, seq_len // tm
        row = lambda b, j: (b * nj + j, 0)
        specs = out_specs(lambda b, j: b * nj + j)
        for i in (3, 4):
            out_shape[i] = jax.ShapeDtypeStruct((nb, HEADS, HEAD_DIM, seq_len), F32)
            specs[i] = pl.BlockSpec((1, HEADS, HEAD_DIM, tm), lambda b, j: (b, 0, 0, j))
        outs = pl.pallas_call(
            functools.partial(_in_proj_seq_kernel, tm=tm),
            out_shape=out_shape + [jax.ShapeDtypeStruct((nb, CONV_WIDTH - 1, CONV_DIM), F32)],
            grid=(nb, nj),
            in_specs=[pl.BlockSpec((tm, d), row), pl.BlockSpec((1, CONV_WIDTH - 1, CONV_DIM), lambda b, j: (b, 0, 0))] + w_specs,
            out_specs=specs + [pl.BlockSpec((1, CONV_WIDTH - 1, CONV_DIM), lambda b, j: (b, 0, 0))],
            scratch_shapes=[pltpu.VMEM((tm + SUBLANES, CONV_DIM), F32)],
            compiler_params=_cparams("arbitrary", "arbitrary"),
            name="in_proj_seq",
        )(x, conv_prev, g, wqkv, wf, bf, wc, wconv, wgl, wpb)
        return outs[:n_common], outs[n_common]
    assert seq_len & (seq_len - 1) == 0 and seq_len >= CONV_WIDTH - 1
    assert n % tm == 0 and tm % seq_len == 0
    first = jnp.zeros((n // seq_len, seq_len, CONV_DIM), F32)
    ov1 = first.at[:, 0].set(conv_prev[:, 1]).reshape(n, CONV_DIM)
    ov2 = first.at[:, 0].set(conv_prev[:, 0]).at[:, 1].set(conv_prev[:, 1]).reshape(n, CONV_DIM)
    row = lambda i: (i, 0)
    outs = pl.pallas_call(
        functools.partial(_in_proj_multi_kernel, tm=tm, seq_len=seq_len),
        out_shape=out_shape + [jax.ShapeDtypeStruct((n, CONV_DIM), F32)],
        grid=(n // tm,),
        in_specs=[pl.BlockSpec((tm, d), row), pl.BlockSpec((tm, CONV_DIM), row), pl.BlockSpec((tm, CONV_DIM), row)] + w_specs,
        out_specs=out_specs(lambda i: i) + [pl.BlockSpec((tm, CONV_DIM), row)],
        scratch_shapes=[pltpu.VMEM((tm + SUBLANES, CONV_DIM), F32)],
        compiler_params=_cparams("arbitrary"),
        name="in_proj_multi",
    )(x, ov1, ov2, g, wqkv, wf, bf, wc, wconv, wgl, wpb)
    tail = outs[n_common].reshape(n // seq_len, seq_len, CONV_DIM)[:, seq_len - (CONV_WIDTH - 1):]
    return outs[:n_common], tail


def _cumsum_kernel(lf_ref, c_ref, *, chunk, nchunk):
    r = lax.broadcasted_iota(I32, (chunk, chunk), 0)
    c = lax.broadcasted_iota(I32, (chunk, chunk), 1)
    upper = jnp.where(r <= c, 1.0, 0.0).astype(BF16)
    carry = jnp.zeros((HEADS, 1), F32)
    for n in range(nchunk):
        rest, terms = lf_ref[0, n], []
        for _ in range(C_TERMS):
            terms.append(rest.astype(BF16))
            rest = rest - terms[-1].astype(F32)
        sums = _dot(jnp.concatenate(terms, axis=0), upper)
        cs = carry
        for j in range(C_TERMS):
            cs = cs + sums[j * HEADS:(j + 1) * HEADS]
        c_ref[0, n] = cs
        carry = cs[:, chunk - 1:chunk]


def _cumsum_chunks(lf):
    b, nchunk, _, chunk = lf.shape
    spec = pl.BlockSpec((1, nchunk, HEADS, chunk), lambda i: (i, 0, 0, 0))
    return pl.pallas_call(
        functools.partial(_cumsum_kernel, chunk=chunk, nchunk=nchunk),
        out_shape=jax.ShapeDtypeStruct(lf.shape, F32),
        grid=(b,), in_specs=[spec], out_specs=spec,
        compiler_params=_cparams("arbitrary"),
        name="logf_cumsum",
    )(lf)


C_TERMS = 3


def _cumsum_rows_kernel(lf_ref, cs_ref, dl_ref, el_ref, *, chunk, nchunk):
    r = lax.broadcasted_iota(I32, (chunk, chunk), 0)
    c = lax.broadcasted_iota(I32, (chunk, chunk), 1)
    lower = jnp.where(c <= r, 1.0, 0.0).astype(BF16)
    decayed = jnp.zeros((1, LANES), F32)
    for n in range(nchunk):
        rest = lf_ref[pl.ds(n * chunk, chunk), :]
        cs = jnp.zeros((chunk, LANES), F32)
        for _ in range(C_TERMS):
            term = rest.astype(BF16)
            cs = cs + _dot(lower, term)
            rest = rest - term.astype(F32)
        last = cs[chunk - 1:chunk, :]
        dl_ref[0, pl.ds(n, 1), :] = -last * LOG2E
        decayed = decayed - last * LOG2E
        el_ref[0, pl.ds(n, 1), :] = decayed
        packed = jnp.zeros((chunk, LANES), F32)
        rest = (cs - last) * LOG2E
        for j in range(C_TERMS):
            term = rest.astype(BF16).astype(F32)
            packed = packed + (term if j == 0 else pltpu.roll(term, j * HEADS, 1))
            rest = rest - term
        cs_ref[pl.ds(n * chunk, chunk), :] = packed.astype(BF16)


def _cumsum_rows(lfp, *, seq):
    n = lfp.shape[0]
    nchunk = seq // TOKEN_TILE
    nseq = n // seq
    spec = pl.BlockSpec((seq, LANES), lambda b: (b, 0))
    per_chunk = pl.BlockSpec((1, nchunk, LANES), lambda b: (b, 0, 0))
    return pl.pallas_call(
        functools.partial(_cumsum_rows_kernel, chunk=TOKEN_TILE, nchunk=nchunk),
        out_shape=[jax.ShapeDtypeStruct((n, LANES), BF16), jax.ShapeDtypeStruct((nseq, nchunk, LANES), F32),
                   jax.ShapeDtypeStruct((nseq, nchunk, LANES), F32)],
        grid=(nseq,), in_specs=[spec], out_specs=[spec, per_chunk, per_chunk],
        compiler_params=_cparams("arbitrary"),
        name="logf_cumsum_rows",
    )(lfp)


SAFE_LOGIT = 64.0


def _attn_prompt_kernel(q_ref, k_ref, v_ref, cs_ref, dl_ref, el_ref, o_ref, ka_ref, vt_ref, kn_ref, m_ref, acc_ref,
                        *, t, tk, nk):
    assert tk == t
    hp, i = pl.program_id(1), pl.program_id(2)
    rr = lax.broadcasted_iota(I32, (LANES, LANES), 0)
    cc = lax.broadcasted_iota(I32, (LANES, LANES), 1)
    lane = lax.broadcasted_iota(I32, (1, LANES), 1)
    head_lanes = [lane < HEAD_DIM, lane >= HEAD_DIM]
    sel = [jnp.where(jnp.where(cc < HEAD_DIM, rr - cc, -1) == HEAD_DIM * h, 1.0, 0.0).astype(BF16) for h in range(2)]

    def max_sq_norm(x, h):
        sq = jnp.sum(jnp.where(head_lanes[h], x * x, 0.0), axis=1, keepdims=True)
        return jnp.max(sq, axis=0, keepdims=True)

    @pl.when(i == 0)
    def _():
        kb = k_ref[...]
        cs = cs_ref[...]
        kf = kb.astype(F32)
        vt = v_ref[...].astype(F32).T
        row = lax.broadcasted_iota(I32, (SUBLANES, tk), 0)
        ones = jnp.where(row == 0, 1.0, 0.0)
        pad = jnp.zeros((LANES - HEAD_DIM - SUBLANES, tk), F32)
        for h in range(2):
            head = 2 * hp + h
            src = jnp.where(cc >= HEAD_DIM, (cc - HEAD_DIM) * HEADS + head, -1)
            src = jnp.where(cc < HEAD_DIM + C_TERMS, src, -1)
            place = jnp.where(rr == src, -1.0, 0.0).astype(BF16)
            ka_ref[h] = (_dot(kb, sel[h]) + _dot(cs, place)).astype(BF16)
            kn_ref[h] = jnp.broadcast_to(max_sq_norm(kf, h), kn_ref.shape[1:])
            for n in range(nk):
                vh = vt[h * HEAD_DIM:(h + 1) * HEAD_DIM, n * tk:(n + 1) * tk]
                vt_ref[h, n] = jnp.concatenate([vh, ones, pad], axis=0).astype(BF16)

    decay_lanes = jnp.logical_and(lane >= HEAD_DIM, lane < HEAD_DIM + C_TERMS)
    q = q_ref[...]
    qa = [(_dot(q, sel[h]) + jnp.where(decay_lanes, 1.0, 0.0)).astype(BF16) for h in range(2)]
    acc_ref[...] = jnp.zeros(acc_ref.shape, F32)
    qf = q.astype(F32)
    bound_sq = jnp.maximum(max_sq_norm(qf, 0) * kn_ref[0, 0:1, 0:1], max_sq_norm(qf, 1) * kn_ref[1, 0:1, 0:1])
    in_range = bound_sq[0, 0] < SAFE_LOGIT * SAFE_LOGIT

    def scores(j):
        start = pl.multiple_of(j * tk, tk)
        return tuple(_dot_nt(ka_ref[h, pl.ds(start, tk), :], qa[h]) for h in range(2))

    def per_head(ref, j, h):
        return jnp.sum(jnp.where(lane == 2 * hp + h, ref[0, pl.ds(j, 1), :], 0.0), axis=1, keepdims=True)

    def visible(st, limit):
        ki = lax.broadcasted_iota(I32, (tk, t), 0)
        qi = lax.broadcasted_iota(I32, (tk, t), 1)
        return jnp.where(ki <= qi + limit, st, -jnp.inf)

    @pl.when(in_range)
    def _():
        def body(j, carry):
            st_pair = scores(j)
            for h in range(2):
                p = jnp.exp2(st_pair[h]).astype(BF16)
                acc_ref[h] = acc_ref[h] * jnp.exp2(-per_head(dl_ref, j, h)) + _dot(vt_ref[h, j], p)
            return carry

        lax.fori_loop(0, i, body, 0)
        row0 = lax.broadcasted_iota(I32, (SUBLANES, LANES), 0) == 0
        pick = jnp.where(row0, jnp.where(decay_lanes, 1.0, 0.0), 0.0).astype(BF16)
        start = pl.multiple_of(i * t, t)
        st_pair = scores(i)
        for h in range(2):
            rq = _dot_nt(pick, ka_ref[h, pl.ds(start, t), :])[0:1]
            p = jnp.exp2(visible(st_pair[h] - rq, 0)).astype(BF16)
            acc_ref[h] = acc_ref[h] * jnp.exp2(-rq - per_head(dl_ref, i, h)) + _dot(vt_ref[h, i], p)

    @pl.when(jnp.logical_not(in_range))
    def _():
        m_ref[0] = jnp.full(m_ref.shape[1:], -jnp.inf, F32)

        def absolute(j, limit):
            st_pair = scores(j)
            out = []
            for h in range(2):
                st = st_pair[h] + per_head(el_ref, j, h)
                out.append(st if limit is None else visible(st, limit))
            return out

        def max_pass(j, limit):
            st_pair = absolute(j, limit)
            for h in range(2):
                m_ref[j + 1, h] = jnp.maximum(m_ref[j, h], jnp.max(st_pair[h], axis=0, keepdims=True))

        def weight_pass(j, limit):
            st_pair = absolute(j, limit)
            for h in range(2):
                m_new = m_ref[j + 1, h]
                p = jnp.exp2(st_pair[h] - m_new).astype(BF16)
                acc_ref[h] = acc_ref[h] * jnp.exp2(m_ref[j, h] - m_new) + _dot(vt_ref[h, j], p)

        max_pass(0, i * t)

        def body(j, carry):
            max_pass(j + 1, None)
            weight_pass(j, None)
            return carry

        lax.fori_loop(0, i - 1, body, 0)

        @pl.when(i >= 1)
        def _():
            max_pass(i, 0)
            weight_pass(i - 1, None)

        weight_pass(i, 0)

    halves = []
    for h in range(2):
        acc = acc_ref[h]
        halves.append(acc[:HEAD_DIM] * (1.0 / acc[HEAD_DIM:HEAD_DIM + 1]))
    o_ref[...] = jnp.concatenate(halves, axis=0).T.astype(BF16)


def _attn_prompt(q, k, v, cs, dl, el, *, nb, seq):
    t, tk = TOKEN_TILE, KEY_CHUNK
    nq, nk = seq // t, seq // tk
    per_chunk = pl.BlockSpec((1, nk, LANES), lambda b, hp, i: (b, 0, 0))
    return pl.pallas_call(
        functools.partial(_attn_prompt_kernel, t=t, tk=tk, nk=nk),
        out_shape=jax.ShapeDtypeStruct(q.shape, BF16),
        grid=(nb, HEADS // 2, nq),
        in_specs=[pl.BlockSpec((t, LANES), lambda b, hp, i: (b * nq + i, hp)),
                  pl.BlockSpec((seq, LANES), lambda b, hp, i: (b, hp)),
                  pl.BlockSpec((seq, LANES), lambda b, hp, i: (b, hp)),
                  pl.BlockSpec((seq, LANES), lambda b, hp, i: (b, 0)), per_chunk, per_chunk],
        out_specs=pl.BlockSpec((t, LANES), lambda b, hp, i: (b * nq + i, hp)),
        scratch_shapes=[pltpu.VMEM((2, seq, LANES), BF16), pltpu.VMEM((2, nk, LANES, tk), BF16),
                        pltpu.VMEM((2, SUBLANES, LANES), F32),
                        pltpu.VMEM((nk + 1, 2, 1, t), F32), pltpu.VMEM((2, LANES, t), F32)],
        compiler_params=_cparams("arbitrary", "arbitrary", "arbitrary"),
        name="attn_prompt",
    )(q, k, v, cs, dl, el)


def _attn_sample_kernel(q_ref, kc_ref, vc_ref, kn_ref, vn_ref, c_ref, o_ref,
                        qbd_ref, m_ref, l_ref, acc_ref, kpad_ref, vpad_ref, *, nq, nchunk):
    j = pl.program_id(1)
    rows = HEADS * nq
    row_head = lax.broadcasted_iota(I32, (rows, FOX_DIM), 0) >> (nq.bit_length() - 1)
    col_head = lax.broadcasted_iota(I32, (rows, FOX_DIM), 1) >> (HEAD_DIM.bit_length() - 1)
    own = row_head == col_head

    @pl.when(j == 0)
    def _():
        qt = jnp.concatenate([q_ref[...]] * HEADS, axis=0)
        qbd_ref[...] = jnp.where(own, qt, jnp.zeros_like(qt))
        m_ref[...] = jnp.full(m_ref.shape, -jnp.inf, F32)
        l_ref[...] = jnp.zeros(l_ref.shape, F32)
        acc_ref[...] = jnp.zeros(acc_ref.shape, F32)

    def update(s, cvals, visible, weighted_values):
        width = s.shape[1]
        bias = jnp.concatenate([jnp.broadcast_to(cvals[h:h + 1, :], (nq, width)) for h in range(HEADS)], axis=0)
        s = s - LOG2E * bias
        if visible is not None:
            s = jnp.where(visible, s, -jnp.inf)
        m_prev = m_ref[...]
        m_new = jnp.maximum(m_prev, jnp.max(s, axis=-1, keepdims=True))
        a = jnp.exp2(m_prev - m_new)
        p = jnp.exp2(s - m_new)
        l_ref[...] = a * l_ref[...] + jnp.sum(p, axis=-1, keepdims=True)
        m_ref[...] = m_new
        acc_ref[...] = acc_ref[...] * a + weighted_values(p.astype(BF16))

    @pl.when(j < nchunk)
    def _():
        chunk = kc_ref.shape[3]
        kt = kc_ref[0].reshape(FOX_DIM, chunk).astype(BF16)
        vt = vc_ref[0].reshape(FOX_DIM, chunk).astype(BF16)
        update(_dot(qbd_ref[...], kt), c_ref[0, j], None, lambda p: _dot_nt(p, vt))

    @pl.when(j == nchunk)
    def _():
        kpad_ref[...] = jnp.zeros(kpad_ref.shape, BF16)
        vpad_ref[...] = jnp.zeros(vpad_ref.shape, BF16)
        kpad_ref[pl.ds(0, nq), :] = kn_ref[...]
        vpad_ref[pl.ds(0, nq), :] = vn_ref[...]
        ki = lax.broadcasted_iota(I32, (rows, LANES), 1)
        qi = lax.broadcasted_iota(I32, (rows, LANES), 0) & (nq - 1)
        update(_dot_nt(qbd_ref[...], kpad_ref[...]), c_ref[0, nchunk][:, :LANES], ki <= qi,
               lambda p: _dot(p, vpad_ref[...]))
        out = jnp.where(own, acc_ref[...] * (1.0 / l_ref[...]), 0.0)
        o = out[0:nq]
        for h in range(1, HEADS):
            o = o + out[h * nq:(h + 1) * nq]
        o_ref[...] = o.astype(BF16)


def _attn_sample(q, k_cache, v_cache, k_new, v_new, c, *, nq):
    nb, past = k_cache.shape[0], k_cache.shape[3]
    chunk = c.shape[-1]
    nchunk = past // chunk
    assert nq & (nq - 1) == 0 and nq <= LANES and past % chunk == 0 and c.shape[1] == nchunk + 1
    rows = HEADS * nq
    cache_spec = pl.BlockSpec((1, HEADS, HEAD_DIM, chunk), lambda b, j: (b, 0, 0, jnp.minimum(j, nchunk - 1)))
    new_spec = pl.BlockSpec((nq, FOX_DIM), lambda b, j: (b, 0))
    return pl.pallas_call(
        functools.partial(_attn_sample_kernel, nq=nq, nchunk=nchunk),
        out_shape=jax.ShapeDtypeStruct(q.shape, BF16),
        grid=(nb, nchunk + 1),
        in_specs=[new_spec, cache_spec, cache_spec, new_spec, new_spec,
                  pl.BlockSpec((1, nchunk + 1, HEADS, chunk), lambda b, j: (b, 0, 0, 0))],
        out_specs=new_spec,
        scratch_shapes=[pltpu.VMEM((rows, FOX_DIM), BF16), pltpu.VMEM((rows, 1), F32), pltpu.VMEM((rows, 1), F32),
                        pltpu.VMEM((rows, FOX_DIM), F32), pltpu.VMEM((LANES, FOX_DIM), BF16),
                        pltpu.VMEM((LANES, FOX_DIM), BF16)],
        compiler_params=_cparams("arbitrary", "arbitrary"),
        name="attn_sample",
    )(q, k_cache, v_cache, k_new, v_new, c)


def _pick(is_first, a_ref, b_ref):
    return jnp.where(is_first, a_ref[...], b_ref[...])


def _post_attn_kernel(xp_ref, xs_ref, op_ref, os_ref, sap_ref, sas_ref, mbp_ref, mbs_ref,
                      wpa_ref, wo_ref, g_ref, wrt_ref, br_ref,
                      h1_ref, xn_ref, pos_ref, gate_ref, cnt_ref, *, n_first, tm):
    is_p = pl.program_id(0) < n_first
    ya = _dot(_pick(is_p, op_ref, os_ref), wpa_ref[...])
    merged = _pick(is_p, sap_ref, sas_ref).astype(F32) * ya + _pick(is_p, mbp_ref, mbs_ref).astype(F32)
    h1 = _pick(is_p, xp_ref, xs_ref) + _dot(merged.astype(BF16), wo_ref[...])
    h1_ref[...] = h1
    xn = _rms(h1, g_ref[...]).astype(BF16)
    xn_ref[...] = xn

    lt = _dot_nt(wrt_ref[...], xn) + br_ref[...]
    eio = lax.broadcasted_iota(I32, (N_EXPERTS, tm), 0).astype(F32)
    vals, hots = [], []
    for _ in range(TOP_K):
        m = jnp.max(lt, axis=0, keepdims=True)
        idx = jnp.min(jnp.where(lt == m, eio, float(N_EXPERTS)), axis=0, keepdims=True)
        hot = eio == idx
        vals.append(m)
        hots.append(hot)
        lt = jnp.where(hot, -jnp.inf, lt)
    ex = [jnp.exp(v - vals[0]) for v in vals]
    den = ex[0] + ex[1] + ex[2] + ex[3]
    gate_ref[...] = jnp.concatenate([e / den for e in ex], axis=0)

    chosen = jnp.zeros((N_EXPERTS, tm), F32)
    for hot in hots:
        chosen = jnp.where(hot, 1.0, chosen)
    td = DISPATCH_TILE
    r = lax.broadcasted_iota(I32, (td, td), 0)
    c = lax.broadcasted_iota(I32, (td, td), 1)
    before = jnp.where(r < c, 1.0, 0.0).astype(BF16)
    er = lax.broadcasted_iota(I32, (N_EXPERTS, N_EXPERTS), 0)
    ec = lax.broadcasted_iota(I32, (N_EXPERTS, N_EXPERTS), 1)
    lower = jnp.where(ec < er, 1.0, 0.0).astype(BF16)
    for sub in range(tm // td):
        sl = slice(sub * td, (sub + 1) * td)
        ch = chosen[:, sl]
        rank = _dot(ch.astype(BF16), before)
        cnt = rank[:, td - 1:td] + ch[:, td - 1:td]
        units = jnp.floor((cnt + (ROW_ALIGN - 1)) * (1.0 / ROW_ALIGN))
        start = ROW_ALIGN * _dot(lower, jnp.broadcast_to(units, (N_EXPERTS, td)).astype(BF16))
        base = start + rank
        pos = [jnp.sum(jnp.where(hot[:, sl], base, 0.0), axis=0, keepdims=True) for hot in hots]
        pos_ref[:, sl] = jnp.concatenate(pos, axis=0).astype(I32)
        cnt_ref[sub] = jnp.broadcast_to(cnt, (N_EXPERTS, LANES)).astype(I32)


def _post_attn(xp, xs, op, os_, sap, sas, mbp, mbs, wpa, wo, g, wrt, br):
    n_p, d = xp.shape
    n_s = xs.shape[0]
    tm = TOKEN_TILE
    assert n_p % tm == 0 and n_s % tm == 0
    n_first, n_tiles = n_p // tm, (n_p + n_s) // tm
    n = n_p + n_s
    sub = tm // DISPATCH_TILE
    first = lambda i: (jnp.minimum(i, n_first - 1), 0)
    second = lambda i: (jnp.maximum(i - n_first, 0), 0)
    row = lambda i: (i, 0)
    col = lambda i: (0, i)

    def pair(width):
        return [pl.BlockSpec((tm, width), first), pl.BlockSpec((tm, width), second)]

    return pl.pallas_call(
        functools.partial(_post_attn_kernel, n_first=n_first, tm=tm),
        out_shape=[jax.ShapeDtypeStruct((n, d), F32), jax.ShapeDtypeStruct((n, d), BF16),
                   jax.ShapeDtypeStruct((TOP_K, n), I32), jax.ShapeDtypeStruct((TOP_K, n), F32),
                   jax.ShapeDtypeStruct((n // DISPATCH_TILE, N_EXPERTS, LANES), I32)],
        grid=(n_tiles,),
        in_specs=pair(d) + pair(FOX_DIM) + pair(d) + pair(d) + [_full(w.shape) for w in (wpa, wo, g, wrt, br)],
        out_specs=[pl.BlockSpec((tm, d), row), pl.BlockSpec((tm, d), row),
                   pl.BlockSpec((TOP_K, tm), col), pl.BlockSpec((TOP_K, tm), col),
                   pl.BlockSpec((sub, N_EXPERTS, LANES), lambda i: (i, 0, 0))],
        compiler_params=_cparams("arbitrary"),
        name="post_attn_router",
    )(xp, xs, op, os_, sap, sas, mbp, mbs, wpa, wo, g, wrt, br)


def _for_each_run(tile, len_ref, loc_ref, dst_ref, fn):
    def body(e, carry):
        idx = tile * N_EXPERTS + e
        n, loc, dst = len_ref[idx], loc_ref[idx], dst_ref[idx]

        for c in RUN_CHUNKS:
            off = n & ~(2 * c - 1)

            @pl.when((n & c) != 0)
            def _():
                fn(loc + off, dst + off, c)
        return carry

    for e in range(N_EXPERTS):
        body(e, 0)


TILE_CHUNKS = tuple(1 << b for b in reversed(range(LOCAL_GROUPS.bit_length())))


def _wait_tile_rows(tile, len_ref, loc_ref, wait_rows):
    last = tile * N_EXPERTS + N_EXPERTS - 1
    total = loc_ref[last] + len_ref[last]
    for c in TILE_CHUNKS:
        @pl.when((total & c) != 0)
        def _():
            wait_rows(c)


def _dispatch_kernel(len_ref, loc_ref, dst_ref, xn_ref, pos_ref, xs_hbm, xloc_ref, sem_ref, *, n_tiles):
    i = pl.program_id(0)
    slot = lax.rem(i, 2)

    def copy(s, loc, dst, groups):
        return pltpu.make_async_copy(xloc_ref.at[s, pl.ds(loc, groups)], xs_hbm.at[pl.ds(dst, groups)], sem_ref.at[s])

    def start_tile(tile, s):
        _for_each_run(tile, len_ref, loc_ref, dst_ref, lambda loc, dst, rows: copy(s, loc, dst, rows).start())

    def wait_tile(tile, s):
        _wait_tile_rows(tile, len_ref, loc_ref, lambda rows: copy(s, 0, 0, rows).wait())

    pos = pos_ref[...]
    aio = lax.broadcasted_iota(I32, (LOCAL_ROWS, DISPATCH_TILE), 0)
    sel = jnp.zeros((LOCAL_ROWS, DISPATCH_TILE), F32)
    for k in range(TOP_K):
        sel = jnp.where(aio == pos[k:k + 1, :], 1.0, sel)
    rows_sorted = _dot(sel.astype(BF16), xn_ref[...])

    @pl.when(i >= 2)
    def _():
        wait_tile(i - 2, slot)

    xloc_ref[slot] = rows_sorted.reshape(xloc_ref.shape[1:])
    start_tile(i, slot)

    @pl.when(i == n_tiles - 1)
    def _():
        if n_tiles >= 2:
            wait_tile(i - 1, 1 - slot)
        wait_tile(i, slot)


def _dispatch(run_len, run_loc, run_dst, xn, pos, *, total_rows):
    n, d = xn.shape
    td = DISPATCH_TILE
    n_tiles = n // td
    return pl.pallas_call(
        functools.partial(_dispatch_kernel, n_tiles=n_tiles),
        out_shape=jax.ShapeDtypeStruct((total_rows // ROW_ALIGN, ROW_ALIGN, d), F32),
        grid_spec=pltpu.PrefetchScalarGridSpec(
            num_scalar_prefetch=3, grid=(n_tiles,),
            in_specs=[pl.BlockSpec((td, d), lambda i, *_: (i, 0)), pl.BlockSpec((TOP_K, td), lambda i, *_: (0, i))],
            out_specs=pl.BlockSpec(memory_space=pl.ANY),
            scratch_shapes=[pltpu.VMEM((2, LOCAL_GROUPS, ROW_ALIGN, d), F32), pltpu.SemaphoreType.DMA((2,))]),
        compiler_params=_cparams("arbitrary"),
        name="moe_dispatch",
    )(run_len, run_loc, run_dst, xn, pos)


def _ffn_kernel(bexp_ref, nval_ref, last_ref, xs_ref, wgu_ref, bgu_ref, wdn_ref, bdn_ref, o_ref, wgu_s, wdn_s):
    i = pl.program_id(0)
    nv = nval_ref[i]

    @pl.when(jnp.logical_or(i == 0, bexp_ref[i] != bexp_ref[jnp.maximum(i - 1, 0)]))
    def _():
        wgu_s[...] = wgu_ref[0].astype(BF16)
        wdn_s[...] = wdn_ref[0].astype(BF16)

    for r in range(FFN_ROWS // FFN_SUB):
        @pl.when(nv > r * FFN_SUB)
        def _():
            rows = lax.broadcasted_iota(I32, (FFN_SUB, xs_ref.shape[1]), 0) + r * FFN_SUB
            x = jnp.where(rows < nv, xs_ref[pl.ds(r * FFN_SUB, FFN_SUB), :], 0.0).astype(BF16)
            gu = _dot(x, wgu_s[...]) + bgu_ref[0]
            gt = jnp.minimum(gu[:, :D_FF], SWIGLU_LIMIT)
            up = jnp.clip(gu[:, D_FF:], -SWIGLU_LIMIT, SWIGLU_LIMIT)
            act = (up + 1.0) * gt * _sigmoid(SWIGLU_ALPHA * gt)
            o_ref[pl.ds(r * FFN_SUB, FFN_SUB), :] = _dot(act.astype(BF16), wdn_s[...]) + bdn_ref[0]


def _ffn(bexp, nval, last, xs, wgu, bgu, wdn, bdn):
    rows, d = xs.shape
    nblk = rows // FFN_ROWS
    blk = lambda i, bexp, nval, last: (jnp.minimum(i, last[0]), 0)
    exp3 = lambda i, bexp, nval, last: (bexp[i], 0, 0)
    return pl.pallas_call(
        _ffn_kernel,
        out_shape=jax.ShapeDtypeStruct((rows, d), F32),
        grid_spec=pltpu.PrefetchScalarGridSpec(
            num_scalar_prefetch=3, grid=(nblk,),
            in_specs=[pl.BlockSpec((FFN_ROWS, d), blk),
                      pl.BlockSpec((1, d, 2 * D_FF), exp3), pl.BlockSpec((1, 1, 2 * D_FF), exp3),
                      pl.BlockSpec((1, D_FF, d), exp3), pl.BlockSpec((1, 1, d), exp3)],
            out_specs=pl.BlockSpec((FFN_ROWS, d), blk),
            scratch_shapes=[pltpu.VMEM((d, 2 * D_FF), BF16), pltpu.VMEM((D_FF, d), BF16)]),
        compiler_params=_cparams("arbitrary"),
        name="moe_ffn",
    )(bexp, nval, last, xs, wgu, bgu, wdn, bdn)


def _combine_kernel(len_ref, loc_ref, dst_ref, os_hbm, pos_ref, gate_ref, h1_ref, pp_ref, ps_ref,
                    gple_ref, wpg_ref, wpp_ref, gfin_ref, yp_ref, ys_ref, oloc_ref, sem_ref, *, n_tiles, n_first):
    i = pl.program_id(0)
    slot = lax.rem(i, 2)

    def copy(s, loc, dst, groups):
        return pltpu.make_async_copy(os_hbm.at[pl.ds(dst, groups)], oloc_ref.at[s, pl.ds(loc, groups)], sem_ref.at[s])

    def start_tile(tile, s):
        _for_each_run(tile, len_ref, loc_ref, dst_ref, lambda loc, dst, rows: copy(s, loc, dst, rows).start())

    def wait_tile(tile, s):
        _wait_tile_rows(tile, len_ref, loc_ref, lambda rows: copy(s, 0, 0, rows).wait())

    @pl.when(i == 0)
    def _():
        oloc_ref[...] = jnp.zeros(oloc_ref.shape, F32)
        start_tile(0, 0)

    @pl.when(i + 1 < n_tiles)
    def _():
        start_tile(i + 1, 1 - slot)

    wait_tile(i, slot)

    pos, gate = pos_ref[...], gate_ref[...]
    lio = lax.broadcasted_iota(I32, (DISPATCH_TILE, LOCAL_ROWS), 1)
    w = jnp.zeros((DISPATCH_TILE, LOCAL_ROWS), F32)
    for k in range(TOP_K):
        w = jnp.where(lio == pos[:, k:k + 1], gate[:, k:k + 1], w)
    h2 = h1_ref[...] + _dot(w.astype(BF16), oloc_ref[slot].reshape(LOCAL_ROWS, -1).astype(BF16))

    is_p = i < n_first
    xn = _rms(h2, gple_ref[...]).astype(BF16)
    ple = _dot(_pick(is_p, pp_ref, ps_ref).astype(BF16), wpp_ref[...])
    h3 = h2 + _sigmoid(_dot(xn, wpg_ref[...])) * ple
    y = _rms(h3, gfin_ref[...])

    @pl.when(is_p)
    def _():
        yp_ref[...] = y

    @pl.when(jnp.logical_not(is_p))
    def _():
        ys_ref[...] = y


def _combine(run_len, run_loc, run_dst, os_, pos, gate, h1, pp, ps, gple, wpg, wpp, gfin):
    n, d = h1.shape
    td = DISPATCH_TILE
    n_p, n_s = pp.shape[0], ps.shape[0]
    assert n_p % td == 0 and n_s % td == 0
    n_tiles, n_first = n // td, n_p // td
    pw = pp.shape[1]
    first = lambda i, *_: (jnp.minimum(i, n_first - 1), 0)
    second = lambda i, *_: (jnp.maximum(i - n_first, 0), 0)
    row = lambda i, *_: (i, 0)
    full2 = lambda i, *_: (0, 0)
    return pl.pallas_call(
        functools.partial(_combine_kernel, n_tiles=n_tiles, n_first=n_first),
        out_shape=[jax.ShapeDtypeStruct((n_p, d), F32), jax.ShapeDtypeStruct((n_s, d), F32)],
        grid_spec=pltpu.PrefetchScalarGridSpec(
            num_scalar_prefetch=3, grid=(n_tiles,),
            in_specs=[pl.BlockSpec(memory_space=pl.ANY),
                      pl.BlockSpec((td, TOP_K), row), pl.BlockSpec((td, TOP_K), row), pl.BlockSpec((td, d), row),
                      pl.BlockSpec((td, pw), first), pl.BlockSpec((td, pw), second),
                      pl.BlockSpec(gple.shape, full2), pl.BlockSpec(wpg.shape, full2),
                      pl.BlockSpec(wpp.shape, full2), pl.BlockSpec(gfin.shape, full2)],
            out_specs=[pl.BlockSpec((td, d), first), pl.BlockSpec((td, d), second)],
            scratch_shapes=[pltpu.VMEM((2, LOCAL_GROUPS, ROW_ALIGN, d), F32), pltpu.SemaphoreType.DMA((2,))]),
        compiler_params=_cparams("arbitrary"),
        name="moe_combine_ple",
    )(run_len, run_loc, run_dst, os_, pos, gate, h1, pp, ps, gple, wpg, wpp, gfin)


def _routing_tables(cnt, total_rows):
    n_tiles = cnt.shape[0]
    run_len = (cnt + (ROW_ALIGN - 1)) // ROW_ALIGN * ROW_ALIGN
    run_loc = jnp.cumsum(run_len, axis=1) - run_len
    per_expert = jnp.sum(run_len, axis=0)
    region = (per_expert + (FFN_ROWS - 1)) // FFN_ROWS * FFN_ROWS
    region_end = jnp.cumsum(region)
    region_start = region_end - region
    run_dst = region_start[None, :] + jnp.cumsum(run_len, axis=0) - run_len
    nblk = total_rows // FFN_ROWS
    blk_row = jnp.arange(nblk, dtype=I32) * FFN_ROWS
    used = region_end[-1] // FFN_ROWS
    last = jnp.maximum(used - 1, 0)
    bexp = jnp.minimum(jnp.sum(blk_row[:, None] >= region_end[None, :], axis=1), N_EXPERTS - 1).astype(I32)
    nval = jnp.clip(per_expert[bexp] - (blk_row - region_start[bexp]), 0, FFN_ROWS)
    nval = jnp.where(blk_row < region_end[-1], nval, 0).astype(I32)
    bexp = jnp.where(blk_row < region_end[-1], bexp, bexp[last])
    flat = lambda a: (a // ROW_ALIGN).reshape(n_tiles * N_EXPERTS).astype(I32)
    return flat(run_len), flat(run_loc), flat(run_dst), bexp, nval, last.reshape(1).astype(I32)


def _chunked_t(lf, chunk):
    b, length, h = lf.shape
    return lf.reshape(b, length // chunk, chunk, h).transpose(0, 1, 3, 2)


def kernel(x_prompt, x_sample, p_prompt, p_sample, cache_k, cache_v, cache_logf, state_conv, g_mix, w_in, b_f, w_conv, w_pa, w_pb, w_o, g_ffn, w_router, b_router, w_gu, b_gu, w_dn, b_dn, g_ple, w_ple_gate, w_ple_proj, g_final):
    depth = g_mix.shape[0]
    assert depth == 1
    nb, seq, d = x_prompt.shape
    db, dseq, _ = x_sample.shape
    past = cache_k.shape[2]
    n_p, n_s = nb * seq, db * dseq

    w = w_in[0].astype(BF16)
    o_f = 3 * FOX_DIM
    o_c = o_f + HEADS
    o_g = o_c + 3 * CONV_DIM
    wf = jnp.pad(w[:, o_f:o_c], ((0, 0), (0, LANES - HEADS)))
    bf = jnp.pad(b_f[0], (0, LANES - HEADS)).reshape(1, LANES)
    mixer_w = (g_mix[0].reshape(1, d), w[:, :o_f], wf, bf, w[:, o_c:o_g], w_conv[0], w[:, o_g:], w_pb[0].astype(BF16))

    (q_p, k_p, v_p, ko_p, vo_p, logf_p, lfp_p, sa_p, mb_p), tail_p = _in_proj(
        x_prompt.reshape(n_p, d), jnp.zeros((nb, CONV_WIDTH - 1, CONV_DIM), F32), mixer_w, seq_len=seq)
    o_p = _attn_prompt(q_p, k_p, v_p, *_cumsum_rows(lfp_p, seq=seq), nb=nb, seq=seq)

    (q_s, k_s, v_s, ko_s, vo_s, logf_s, _, sa_s, mb_s), tail_s = _in_proj(
        x_sample.reshape(n_s, d), state_conv[0], mixer_w, seq_len=dseq)
    chunk = min(past, 2 * TOKEN_TILE)
    lf_all = jnp.concatenate([cache_logf[0].astype(F32), logf_s.reshape(db, dseq, HEADS),
                              jnp.zeros((db, chunk - dseq, HEADS), F32)], axis=1)
    c_s = _cumsum_chunks(_chunked_t(lf_all, chunk))
    o_s = _attn_sample(q_s, cache_k[0].transpose(0, 2, 3, 1), cache_v[0].transpose(0, 2, 3, 1), k_s, v_s, c_s, nq=dseq)

    h1, xn2, pos_t, gate_t, cnt = _post_attn(
        x_prompt.reshape(n_p, d), x_sample.reshape(n_s, d), o_p, o_s, sa_p, sa_s, mb_p, mb_s,
        w_pa[0].astype(BF16), w_o[0].astype(BF16), g_ffn[0].reshape(1, d),
        w_router[0].T.astype(BF16), b_router[0].reshape(N_EXPERTS, 1))

    n = n_p + n_s
    n_tiles = n // DISPATCH_TILE
    max_rows = n * TOP_K + n_tiles * N_EXPERTS * (ROW_ALIGN - 1) + N_EXPERTS * (FFN_ROWS - 1)
    total_rows = -(-max_rows // FFN_ROWS) * FFN_ROWS
    run_len, run_loc, run_dst, bexp, nval, last = _routing_tables(cnt[:, :, 0], total_rows)
    xs = _dispatch(run_len, run_loc, run_dst, xn2, pos_t, total_rows=total_rows)
    os_ = _ffn(bexp, nval, last, xs.reshape(total_rows, d), w_gu[0], b_gu[0].reshape(N_EXPERTS, 1, 2 * D_FF),
               w_dn[0], b_dn[0].reshape(N_EXPERTS, 1, d))

    y_p, y_s = _combine(run_len, run_loc, run_dst, os_.reshape(total_rows // ROW_ALIGN, ROW_ALIGN, d), pos_t.T, gate_t.T, h1,
                        p_prompt[0].reshape(n_p, -1), p_sample[0].reshape(n_s, -1),
                        g_ple[0].reshape(1, d), w_ple_gate[0].astype(BF16), w_ple_proj[0].astype(BF16),
                        g_final.reshape(1, d))

    return (y_p.reshape(nb, seq, d), y_s.reshape(db, dseq, d),
            ko_p.transpose(0, 3, 1, 2)[None], vo_p.transpose(0, 3, 1, 2)[None],
            logf_p.reshape(1, nb, seq, HEADS), tail_p[None],
            ko_s.reshape(1, db, dseq, HEADS, HEAD_DIM), vo_s.reshape(1, db, dseq, HEADS, HEAD_DIM),
            logf_s.reshape(1, db, dseq, HEADS), tail_s[None])
```

```python
import functools

import jax
import jax.numpy as jnp
from jax import lax
from jax.experimental import pallas as pl
from jax.experimental.pallas import tpu as pltpu

F32, BF16, I32 = jnp.float32, jnp.bfloat16, jnp.int32

HEADS = 8
HEAD_DIM = 64
FOX_DIM = HEADS * HEAD_DIM
CONV_DIM = 512
CONV_WIDTH = 3
N_EXPERTS = 32
TOP_K = 4
D_FF = 1024
SWIGLU_ALPHA = 1.702
SWIGLU_LIMIT = 7.0
RMS_EPS = 1e-6
LOG2E = 1.4426950408889634

LANES = 128
SUBLANES = 8
TOKEN_TILE = 512
KEY_CHUNK = 512
DISPATCH_TILE = 256
ROW_ALIGN = SUBLANES
LOCAL_ROWS = DISPATCH_TILE * TOP_K + N_EXPERTS * ROW_ALIGN
FFN_ROWS = 1024
FFN_SUB = 512
LOCAL_GROUPS = LOCAL_ROWS // ROW_ALIGN
RUN_CHUNKS = (32, 16, 8, 4, 2, 1)
VMEM_LIMIT_BYTES = 56 * 1024 * 1024


def _cparams(*sem):
    return pltpu.CompilerParams(dimension_semantics=sem, vmem_limit_bytes=VMEM_LIMIT_BYTES)


def _rms(x, g):
    return x * lax.rsqrt(jnp.mean(x * x, axis=-1, keepdims=True) + RMS_EPS) * g


def _sigmoid(x):
    return 1.0 / (1.0 + jnp.exp(-x))


def _log_sigmoid(x):
    return jnp.minimum(x, 0.0) - jnp.log1p(jnp.exp(-jnp.abs(x)))


def _dot(a, b):
    return jnp.dot(a, b, preferred_element_type=F32)


def _dot_nt(a, b):
    return lax.dot_general(a, b, (((1,), (1,)), ((), ())), preferred_element_type=F32)


def _mixer_inputs(x_ref, g_ref, wqkv_ref, wf_ref, bf_ref, wc_ref, wgl_ref,
                  q_ref, k_ref, v_ref, ko_ref, vo_ref, logf_ref, lfp_ref, sa_ref, *, keys_minor):
    xn = _rms(x_ref[...], g_ref[...]).astype(BF16)
    qkv = _dot(xn, wqkv_ref[...])
    q_ref[...] = (qkv[:, :FOX_DIM] * (HEAD_DIM ** -0.5 * LOG2E)).astype(BF16)
    k = qkv[:, FOX_DIM:2 * FOX_DIM]
    v = qkv[:, 2 * FOX_DIM:]
    k_ref[...] = k.astype(BF16)
    v_ref[...] = v.astype(BF16)
    if keys_minor:
        ko_ref[0] = k.T.reshape(HEADS, HEAD_DIM, k.shape[0])
        vo_ref[0] = v.T.reshape(HEADS, HEAD_DIM, v.shape[0])
    else:
        for h in range(HEADS):
            ko_ref[:, h, :] = k[:, h * HEAD_DIM:(h + 1) * HEAD_DIM]
            vo_ref[:, h, :] = v[:, h * HEAD_DIM:(h + 1) * HEAD_DIM]
    logf = _log_sigmoid(_dot(xn, wf_ref[...]) + bf_ref[...])
    logf_ref[...] = logf[:, :HEADS]
    lane = lax.broadcasted_iota(I32, logf.shape, 1)
    lfp_ref[...] = jnp.where(lane < HEADS, logf, 0.0)
    c3 = _dot(xn, wc_ref[...])
    gate_b = c3[:, :CONV_DIM]
    z = c3[:, CONV_DIM:2 * CONV_DIM] * c3[:, 2 * CONV_DIM:]
    gl = _dot(xn, wgl_ref[...])
    d = gl.shape[1] // 2
    sa_ref[...] = _sigmoid(gl[:, :d]).astype(BF16)
    return gate_b, z, _sigmoid(gl[:, d:])


def _conv_out(gate_b, z, z1, z2, sig_b, wconv_ref, wpb_ref, mb_ref):
    zc = wconv_ref[0:1, :] * z2 + wconv_ref[1:2, :] * z1 + wconv_ref[2:3, :] * z
    yb = _dot((gate_b * zc).astype(BF16), wpb_ref[...])
    mb_ref[...] = (sig_b * yb).astype(BF16)


N_MIXER_W = 8
N_MIXER_OUT = 9


def _mixer_front(x_ref, w_refs, out_refs, keys_minor):
    g_ref, wqkv_ref, wf_ref, bf_ref, wc_ref, wconv_ref, wgl_ref, wpb_ref = w_refs
    gate_b, z, sig_b = _mixer_inputs(x_ref, g_ref, wqkv_ref, wf_ref, bf_ref, wc_ref, wgl_ref, *out_refs[:-1],
                                     keys_minor=keys_minor)
    return gate_b, z, sig_b, wconv_ref, wpb_ref, out_refs[-1]


def _in_proj_seq_kernel(x_ref, prev_ref, *refs, tm):
    w_refs, out_refs = refs[:N_MIXER_W], refs[N_MIXER_W:N_MIXER_W + N_MIXER_OUT]
    tail_ref, zbuf_ref = refs[N_MIXER_W + N_MIXER_OUT:]
    gate_b, z, sig_b, wconv_ref, wpb_ref, mb_ref = _mixer_front(x_ref, w_refs, out_refs, True)
    zbuf_ref[pl.ds(SUBLANES, tm), :] = z

    @pl.when(pl.program_id(1) == 0)
    def _():
        zbuf_ref[pl.ds(SUBLANES - 2, 2), :] = prev_ref[0]

    z1 = zbuf_ref[pl.ds(SUBLANES - 1, tm), :]
    z2 = zbuf_ref[pl.ds(SUBLANES - 2, tm), :]
    _conv_out(gate_b, z, z1, z2, sig_b, wconv_ref, wpb_ref, mb_ref)
    tail = zbuf_ref[pl.ds(tm + SUBLANES - 2, 2), :]
    zbuf_ref[pl.ds(SUBLANES - 2, 2), :] = tail
    tail_ref[0] = tail


def _in_proj_multi_kernel(x_ref, ov1_ref, ov2_ref, *refs, tm, seq_len):
    w_refs, out_refs = refs[:N_MIXER_W], refs[N_MIXER_W:N_MIXER_W + N_MIXER_OUT]
    z_ref, zbuf_ref = refs[N_MIXER_W + N_MIXER_OUT:]
    gate_b, z, sig_b, wconv_ref, wpb_ref, mb_ref = _mixer_front(x_ref, w_refs, out_refs, False)
    z_ref[...] = z
    zbuf_ref[pl.ds(0, SUBLANES), :] = jnp.zeros((SUBLANES, CONV_DIM), F32)
    zbuf_ref[pl.ds(SUBLANES, tm), :] = z
    t = lax.broadcasted_iota(I32, (tm, CONV_DIM), 0) & (seq_len - 1)
    z1 = jnp.where(t == 0, ov1_ref[...], zbuf_ref[pl.ds(SUBLANES - 1, tm), :])
    z2 = jnp.where(t < 2, ov2_ref[...], zbuf_ref[pl.ds(SUBLANES - 2, tm), :])
    _conv_out(gate_b, z, z1, z2, sig_b, wconv_ref, wpb_ref, mb_ref)


def _full(shape):
    n = len(shape)
    return pl.BlockSpec(shape, lambda *_: (0,) * n)


def _in_proj(x, conv_prev, weights, *, seq_len):
    g, wqkv, wf, bf, wc, wconv, wgl, wpb = weights
    n, d = x.shape
    w_specs = [_full(w.shape) for w in (g, wqkv, wf, bf, wc, wconv, wgl, wpb)]
    shapes = (((FOX_DIM,), BF16), ((FOX_DIM,), BF16), ((FOX_DIM,), BF16), ((HEADS, HEAD_DIM), F32), ((HEADS, HEAD_DIM), F32),
              ((HEADS,), F32), ((LANES,), F32), ((d,), BF16), ((d,), BF16))
    assert len(weights) == N_MIXER_W and len(shapes) == N_MIXER_OUT
    out_shape = [jax.ShapeDtypeStruct((n,) + s, t) for s, t in shapes]
    n_common = N_MIXER_OUT
    tm = TOKEN_TILE

    def out_specs(tile_index):
        return [pl.BlockSpec((tm,) + s, lambda *a, k=len(s): (tile_index(*a),) + (0,) * k) for s, _ in shapes]

    if seq_len % tm == 0:
        nb, nj = n // seq_len, seq_len // tm
        row = lambda b, j: (b * nj + j, 0)
        specs = out_specs(lambda b, j: b * nj + j)
        for i in (3, 4):
            out_shape[i] = jax.ShapeDtypeStruct((nb, HEADS, HEAD_DIM, seq_len), F32)
            specs[i] = pl.BlockSpec((1, HEADS, HEAD_DIM, tm), lambda b, j: (b, 0, 0, j))
        outs = pl.pallas_call(
            functools.partial(_in_proj_seq_kernel, tm=tm),
            out_shape=out_shape + [jax.ShapeDtypeStruct((nb, CONV_WIDTH - 1, CONV_DIM), F32)],
            grid=(nb, nj),
            in_specs=[pl.BlockSpec((tm, d), row), pl.BlockSpec((1, CONV_WIDTH - 1, CONV_DIM), lambda b, j: (b, 0, 0))] + w_specs,
            out_specs=specs + [pl.BlockSpec((1, CONV_WIDTH - 1, CONV_DIM), lambda b, j: (b, 0, 0))],
            scratch_shapes=[pltpu.VMEM((tm + SUBLANES, CONV_DIM), F32)],
            compiler_params=_cparams("arbitrary", "arbitrary"),
            name="in_proj_seq",
        )(x, conv_prev, g, wqkv, wf, bf, wc, wconv, wgl, wpb)
        return outs[:n_common], outs[n_common]
    assert seq_len & (seq_len - 1) == 0 and seq_len >= CONV_WIDTH - 1
    assert n % tm == 0 and tm % seq_len == 0
    first = jnp.zeros((n // seq_len, seq_len, CONV_DIM), F32)
    ov1 = first.at[:, 0].set(conv_prev[:, 1]).reshape(n, CONV_DIM)
    ov2 = first.at[:, 0].set(conv_prev[:, 0]).at[:, 1].set(conv_prev[:, 1]).reshape(n, CONV_DIM)
    row = lambda i: (i, 0)
    outs = pl.pallas_call(
        functools.partial(_in_proj_multi_kernel, tm=tm, seq_len=seq_len),
        out_shape=out_shape + [jax.ShapeDtypeStruct((n, CONV_DIM), F32)],
        grid=(n // tm,),
        in_specs=[pl.BlockSpec((tm, d), row), pl.BlockSpec((tm, CONV_DIM), row), pl.BlockSpec((tm, CONV_DIM), row)] + w_specs,
        out_specs=out_specs(lambda i: i) + [pl.BlockSpec((tm, CONV_DIM), row)],
        scratch_shapes=[pltpu.VMEM((tm + SUBLANES, CONV_DIM), F32)],
        compiler_params=_cparams("arbitrary"),
        name="in_proj_multi",
    )(x, ov1, ov2, g, wqkv, wf, bf, wc, wconv, wgl, wpb)
    tail = outs[n_common].reshape(n // seq_len, seq_len, CONV_DIM)[:, seq_len - (CONV_WIDTH - 1):]
    return outs[:n_common], tail


def _cumsum_kernel(lf_ref, c_ref, *, chunk, nchunk):
    r = lax.broadcasted_iota(I32, (chunk, chunk), 0)
    c = lax.broadcasted_iota(I32, (chunk, chunk), 1)
    upper = jnp.where(r <= c, 1.0, 0.0).astype(BF16)
    carry = jnp.zeros((HEADS, 1), F32)
    for n in range(nchunk):
        rest, terms = lf_ref[0, n], []
        for _ in range(C_TERMS):
            terms.append(rest.astype(BF16))
            rest = rest - terms[-1].astype(F32)
        sums = _dot(jnp.concatenate(terms, axis=0), upper)
        cs = carry
        for j in range(C_TERMS):
            cs = cs + sums[j * HEADS:(j + 1) * HEADS]
        c_ref[0, n] = cs
        carry = cs[:, chunk - 1:chunk]


def _cumsum_chunks(lf):
    b, nchunk, _, chunk = lf.shape
    spec = pl.BlockSpec((1, nchunk, HEADS, chunk), lambda i: (i, 0, 0, 0))
    return pl.pallas_call(
        functools.partial(_cumsum_kernel, chunk=chunk, nchunk=nchunk),
        out_shape=jax.ShapeDtypeStruct(lf.shape, F32),
        grid=(b,), in_specs=[spec], out_specs=spec,
        compiler_params=_cparams("arbitrary"),
        name="logf_cumsum",
    )(lf)


C_TERMS = 3


def _cumsum_rows_kernel(lf_ref, cs_ref, *, chunk, nchunk):
    r = lax.broadcasted_iota(I32, (chunk, chunk), 0)
    c = lax.broadcasted_iota(I32, (chunk, chunk), 1)
    lower = jnp.where(c <= r, 1.0, 0.0).astype(BF16)
    carry = jnp.zeros((1, LANES), F32)
    for n in range(nchunk):
        rest = lf_ref[pl.ds(n * chunk, chunk), :]
        cs = carry
        for _ in range(C_TERMS):
            term = rest.astype(BF16)
            cs = cs + _dot(lower, term)
            rest = rest - term.astype(F32)
        carry = cs[chunk - 1:chunk, :]
        packed = jnp.zeros((chunk, LANES), F32)
        rest = cs * LOG2E
        for j in range(C_TERMS):
            term = rest.astype(BF16).astype(F32)
            packed = packed + (term if j == 0 else pltpu.roll(term, j * HEADS, 1))
            rest = rest - term
        cs_ref[pl.ds(n * chunk, chunk), :] = packed.astype(BF16)


def _cumsum_rows(lfp, *, seq):
    n = lfp.shape[0]
    spec = pl.BlockSpec((seq, LANES), lambda b: (b, 0))
    return pl.pallas_call(
        functools.partial(_cumsum_rows_kernel, chunk=TOKEN_TILE, nchunk=seq // TOKEN_TILE),
        out_shape=jax.ShapeDtypeStruct((n, LANES), BF16),
        grid=(n // seq,), in_specs=[spec], out_specs=spec,
        compiler_params=_cparams("arbitrary"),
        name="logf_cumsum_rows",
    )(lfp)


SAFE_LOGIT = 64.0


def _attn_prompt_kernel(q_ref, k_ref, v_ref, cs_ref, o_ref, ka_ref, vt_ref, kn_ref, m_ref, acc_ref, *, t, tk, nk):
    assert tk == t
    hp, i = pl.program_id(1), pl.program_id(2)
    rr = lax.broadcasted_iota(I32, (LANES, LANES), 0)
    cc = lax.broadcasted_iota(I32, (LANES, LANES), 1)
    lane = lax.broadcasted_iota(I32, (1, LANES), 1)
    head_lanes = [lane < HEAD_DIM, lane >= HEAD_DIM]
    sel = [jnp.where(jnp.where(cc < HEAD_DIM, rr - cc, -1) == HEAD_DIM * h, 1.0, 0.0).astype(BF16) for h in range(2)]

    def place(h, first_lane, value):
        src = jnp.where(cc >= first_lane, (cc - first_lane) * HEADS + 2 * hp + h, -1)
        src = jnp.where(cc < first_lane + C_TERMS, src, -1)
        return jnp.where(rr == src, value, 0.0).astype(BF16)

    def ones_at(first_lane):
        return jnp.where(jnp.logical_and(lane >= first_lane, lane < first_lane + C_TERMS), 1.0, 0.0)

    def max_sq_norm(x, h):
        sq = jnp.sum(jnp.where(head_lanes[h], x * x, 0.0), axis=1, keepdims=True)
        return jnp.max(sq, axis=0, keepdims=True)

    key_lane, query_lane = HEAD_DIM, HEAD_DIM + C_TERMS

    @pl.when(i == 0)
    def _():
        kb = k_ref[...]
        cs = cs_ref[...]
        kf = kb.astype(F32)
        vt = v_ref[...].astype(F32).T
        row = lax.broadcasted_iota(I32, (SUBLANES, tk), 0)
        ones = jnp.where(row == 0, 1.0, 0.0)
        pad = jnp.zeros((LANES - HEAD_DIM - SUBLANES, tk), F32)
        for h in range(2):
            ka_ref[h] = (_dot(kb, sel[h]) + _dot(cs, place(h, key_lane, -1.0)) + ones_at(query_lane)).astype(BF16)
            kn_ref[h] = jnp.broadcast_to(max_sq_norm(kf, h), kn_ref.shape[1:])
            for n in range(nk):
                vh = vt[h * HEAD_DIM:(h + 1) * HEAD_DIM, n * tk:(n + 1) * tk]
                vt_ref[h, n] = jnp.concatenate([vh, ones, pad], axis=0).astype(BF16)

    q = q_ref[...]
    cq = cs_ref[pl.ds(pl.multiple_of(i * t, t), t), :]
    qa = [(_dot(q, sel[h]) + _dot(cq, place(h, query_lane, 1.0)) + ones_at(key_lane)).astype(BF16) for h in range(2)]
    qf = q.astype(F32)
    bound_sq = jnp.maximum(max_sq_norm(qf, 0) * kn_ref[0, 0:1, 0:1], max_sq_norm(qf, 1) * kn_ref[1, 0:1, 0:1])
    in_range = bound_sq[0, 0] < SAFE_LOGIT * SAFE_LOGIT

    def scores(j):
        start = pl.multiple_of(j * tk, tk)
        return tuple(_dot_nt(ka_ref[h, pl.ds(start, tk), :], qa[h]) for h in range(2))

    def visible(st, limit):
        ki = lax.broadcasted_iota(I32, (tk, t), 0)
        qi = lax.broadcasted_iota(I32, (tk, t), 1)
        return jnp.where(ki <= qi + limit, st, -jnp.inf)

    @pl.when(in_range)
    def _():
        st_pair = scores(i)
        for h in range(2):
            acc_ref[h] = _dot(vt_ref[h, i], jnp.exp2(visible(st_pair[h], 0)).astype(BF16))

        def add_chunks(chunks):
            st_pairs = [scores(j) for j in chunks]
            for h in range(2):
                total = acc_ref[h]
                for j, st_pair in zip(chunks, st_pairs):
                    total = total + _dot(vt_ref[h, j], jnp.exp2(st_pair[h]).astype(BF16))
                acc_ref[h] = total

        def body(n, carry):
            add_chunks([2 * n, 2 * n + 1])
            return carry

        lax.fori_loop(0, i // 2, body, 0)

        @pl.when(i % 2 == 1)
        def _():
            add_chunks([i - 1])

    @pl.when(jnp.logical_not(in_range))
    def _():
        m_ref[0] = jnp.full(m_ref.shape[1:], -jnp.inf, F32)
        acc_ref[...] = jnp.zeros(acc_ref.shape, F32)

        def limited(j, limit):
            st_pair = scores(j)
            return st_pair if limit is None else [visible(st, limit) for st in st_pair]

        def max_pass(j, limit):
            st_pair = limited(j, limit)
            for h in range(2):
                m_ref[j + 1, h] = jnp.maximum(m_ref[j, h], jnp.max(st_pair[h], axis=0, keepdims=True))

        def weight_pass(j, limit):
            st_pair = limited(j, limit)
            for h in range(2):
                m_new = m_ref[j + 1, h]
                p = jnp.exp2(st_pair[h] - m_new).astype(BF16)
                acc_ref[h] = acc_ref[h] * jnp.exp2(m_ref[j, h] - m_new) + _dot(vt_ref[h, j], p)

        max_pass(0, i * t)

        def body(j, carry):
            max_pass(j + 1, None)
            weight_pass(j, None)
            return carry

        lax.fori_loop(0, i - 1, body, 0)

        @pl.when(i >= 1)
        def _():
            max_pass(i, 0)
            weight_pass(i - 1, None)

        weight_pass(i, 0)

    halves = []
    for h in range(2):
        acc = acc_ref[h]
        halves.append(acc[:HEAD_DIM] * (1.0 / acc[HEAD_DIM:HEAD_DIM + 1]))
    o_ref[...] = jnp.concatenate(halves, axis=0).T.astype(BF16)


def _attn_prompt(q, k, v, cs, *, nb, seq):
    t, tk = TOKEN_TILE, KEY_CHUNK
    nq, nk = seq // t, seq // tk
    return pl.pallas_call(
        functools.partial(_attn_prompt_kernel, t=t, tk=tk, nk=nk),
        out_shape=jax.ShapeDtypeStruct(q.shape, BF16),
        grid=(nb, HEADS // 2, nq),
        in_specs=[pl.BlockSpec((t, LANES), lambda b, hp, i: (b * nq + i, hp)),
                  pl.BlockSpec((seq, LANES), lambda b, hp, i: (b, hp)),
                  pl.BlockSpec((seq, LANES), lambda b, hp, i: (b, hp)),
                  pl.BlockSpec((seq, LANES), lambda b, hp, i: (b, 0))],
        out_specs=pl.BlockSpec((t, LANES), lambda b, hp, i: (b * nq + i, hp)),
        scratch_shapes=[pltpu.VMEM((2, seq, LANES), BF16), pltpu.VMEM((2, nk, LANES, tk), BF16),
                        pltpu.VMEM((2, SUBLANES, LANES), F32),
                        pltpu.VMEM((nk + 1, 2, 1, t), F32), pltpu.VMEM((2, LANES, t), F32)],
        compiler_params=_cparams("arbitrary", "arbitrary", "arbitrary"),
        name="attn_prompt",
    )(q, k, v, cs)


def _attn_sample_kernel(q_ref, kc_ref, vc_ref, kn_ref, vn_ref, c_ref, o_ref,
                        qbd_ref, m_ref, l_ref, acc_ref, kpad_ref, vpad_ref, *, nq, nchunk):
    j = pl.program_id(1)
    rows = HEADS * nq
    row_head = lax.broadcasted_iota(I32, (rows, FOX_DIM), 0) >> (nq.bit_length() - 1)
    col_head = lax.broadcasted_iota(I32, (rows, FOX_DIM), 1) >> (HEAD_DIM.bit_length() - 1)
    own = row_head == col_head

    @pl.when(j == 0)
    def _():
        qt = jnp.concatenate([q_ref[...]] * HEADS, axis=0)
        qbd_ref[...] = jnp.where(own, qt, jnp.zeros_like(qt))
        m_ref[...] = jnp.full(m_ref.shape, -jnp.inf, F32)
        l_ref[...] = jnp.zeros(l_ref.shape, F32)
        acc_ref[...] = jnp.zeros(acc_ref.shape, F32)

    def update(s, cvals, visible, weighted_values):
        width = s.shape[1]
        bias = jnp.concatenate([jnp.broadcast_to(cvals[h:h + 1, :], (nq, width)) for h in range(HEADS)], axis=0)
        s = s - LOG2E * bias
        if visible is not None:
            s = jnp.where(visible, s, -jnp.inf)
        m_prev = m_ref[...]
        m_new = jnp.maximum(m_prev, jnp.max(s, axis=-1, keepdims=True))
        a = jnp.exp2(m_prev - m_new)
        p = jnp.exp2(s - m_new)
        l_ref[...] = a * l_ref[...] + jnp.sum(p, axis=-1, keepdims=True)
        m_ref[...] = m_new
        acc_ref[...] = acc_ref[...] * a + weighted_values(p.astype(BF16))

    @pl.when(j < nchunk)
    def _():
        chunk = kc_ref.shape[3]
        kt = kc_ref[0].reshape(FOX_DIM, chunk).astype(BF16)
        vt = vc_ref[0].reshape(FOX_DIM, chunk).astype(BF16)
        update(_dot(qbd_ref[...], kt), c_ref[0, j], None, lambda p: _dot_nt(p, vt))

    @pl.when(j == nchunk)
    def _():
        kpad_ref[...] = jnp.zeros(kpad_ref.shape, BF16)
        vpad_ref[...] = jnp.zeros(vpad_ref.shape, BF16)
        kpad_ref[pl.ds(0, nq), :] = kn_ref[...]
        vpad_ref[pl.ds(0, nq), :] = vn_ref[...]
        ki = lax.broadcasted_iota(I32, (rows, LANES), 1)
        qi = lax.broadcasted_iota(I32, (rows, LANES), 0) & (nq - 1)
        update(_dot_nt(qbd_ref[...], kpad_ref[...]), c_ref[0, nchunk][:, :LANES], ki <= qi,
               lambda p: _dot(p, vpad_ref[...]))
        out = jnp.where(own, acc_ref[...] * (1.0 / l_ref[...]), 0.0)
        o = out[0:nq]
        for h in range(1, HEADS):
            o = o + out[h * nq:(h + 1) * nq]
        o_ref[...] = o.astype(BF16)


def _attn_sample(q, k_cache, v_cache, k_new, v_new, c, *, nq):
    nb, past = k_cache.shape[0], k_cache.shape[3]
    chunk = c.shape[-1]
    nchunk = past // chunk
    assert nq & (nq - 1) == 0 and nq <= LANES and past % chunk == 0 and c.shape[1] == nchunk + 1
    rows = HEADS * nq
    cache_spec = pl.BlockSpec((1, HEADS, HEAD_DIM, chunk), lambda b, j: (b, 0, 0, jnp.minimum(j, nchunk - 1)))
    new_spec = pl.BlockSpec((nq, FOX_DIM), lambda b, j: (b, 0))
    return pl.pallas_call(
        functools.partial(_attn_sample_kernel, nq=nq, nchunk=nchunk),
        out_shape=jax.ShapeDtypeStruct(q.shape, BF16),
        grid=(nb, nchunk + 1),
        in_specs=[new_spec, cache_spec, cache_spec, new_spec, new_spec,
                  pl.BlockSpec((1, nchunk + 1, HEADS, chunk), lambda b, j: (b, 0, 0, 0))],
        out_specs=new_spec,
        scratch_shapes=[pltpu.VMEM((rows, FOX_DIM), BF16), pltpu.VMEM((rows, 1), F32), pltpu.VMEM((rows, 1), F32),
                        pltpu.VMEM((rows, FOX_DIM), F32), pltpu.VMEM((LANES, FOX_DIM), BF16),
                        pltpu.VMEM((LANES, FOX_DIM), BF16)],
        compiler_params=_cparams("arbitrary", "arbitrary"),
        name="attn_sample",
    )(q, k_cache, v_cache, k_new, v_new, c)


def _pick(is_first, a_ref, b_ref):
    return jnp.where(is_first, a_ref[...], b_ref[...])


def _post_attn_kernel(xp_ref, xs_ref, op_ref, os_ref, sap_ref, sas_ref, mbp_ref, mbs_ref,
                      wpa_ref, wo_ref, g_ref, wrt_ref, br_ref,
                      h1_ref, xn_ref, pos_ref, gate_ref, cnt_ref, *, n_first, tm):
    is_p = pl.program_id(0) < n_first
    ya = _dot(_pick(is_p, op_ref, os_ref), wpa_ref[...])
    merged = _pick(is_p, sap_ref, sas_ref).astype(F32) * ya + _pick(is_p, mbp_ref, mbs_ref).astype(F32)
    h1 = _pick(is_p, xp_ref, xs_ref) + _dot(merged.astype(BF16), wo_ref[...])
    h1_ref[...] = h1
    xn = _rms(h1, g_ref[...]).astype(BF16)
    xn_ref[...] = xn

    lt = _dot_nt(wrt_ref[...], xn) + br_ref[...]
    eio = lax.broadcasted_iota(I32, (N_EXPERTS, tm), 0).astype(F32)
    vals, hots = [], []
    for _ in range(TOP_K):
        m = jnp.max(lt, axis=0, keepdims=True)
        idx = jnp.min(jnp.where(lt == m, eio, float(N_EXPERTS)), axis=0, keepdims=True)
        hot = eio == idx
        vals.append(m)
        hots.append(hot)
        lt = jnp.where(hot, -jnp.inf, lt)
    ex = [jnp.exp(v - vals[0]) for v in vals]
    den = ex[0] + ex[1] + ex[2] + ex[3]
    gate_ref[...] = jnp.concatenate([e / den for e in ex], axis=0)

    chosen = jnp.zeros((N_EXPERTS, tm), F32)
    for hot in hots:
        chosen = jnp.where(hot, 1.0, chosen)
    td = DISPATCH_TILE
    r = lax.broadcasted_iota(I32, (td, td), 0)
    c = lax.broadcasted_iota(I32, (td, td), 1)
    before = jnp.where(r < c, 1.0, 0.0).astype(BF16)
    er = lax.broadcasted_iota(I32, (N_EXPERTS, N_EXPERTS), 0)
    ec = lax.broadcasted_iota(I32, (N_EXPERTS, N_EXPERTS), 1)
    lower = jnp.where(ec < er, 1.0, 0.0).astype(BF16)
    for sub in range(tm // td):
        sl = slice(sub * td, (sub + 1) * td)
        ch = chosen[:, sl]
        rank = _dot(ch.astype(BF16), before)
        cnt = rank[:, td - 1:td] + ch[:, td - 1:td]
        units = jnp.floor((cnt + (ROW_ALIGN - 1)) * (1.0 / ROW_ALIGN))
        start = ROW_ALIGN * _dot(lower, jnp.broadcast_to(units, (N_EXPERTS, td)).astype(BF16))
        base = start + rank
        pos = [jnp.sum(jnp.where(hot[:, sl], base, 0.0), axis=0, keepdims=True) for hot in hots]
        pos_ref[:, sl] = jnp.concatenate(pos, axis=0).astype(I32)
        cnt_ref[sub] = jnp.broadcast_to(cnt, (N_EXPERTS, LANES)).astype(I32)


def _post_attn(xp, xs, op, os_, sap, sas, mbp, mbs, wpa, wo, g, wrt, br):
    n_p, d = xp.shape
    n_s = xs.shape[0]
    tm = TOKEN_TILE
    assert n_p % tm == 0 and n_s % tm == 0
    n_first, n_tiles = n_p // tm, (n_p + n_s) // tm
    n = n_p + n_s
    sub = tm // DISPATCH_TILE
    first = lambda i: (jnp.minimum(i, n_first - 1), 0)
    second = lambda i: (jnp.maximum(i - n_first, 0), 0)
    row = lambda i: (i, 0)
    col = lambda i: (0, i)

    def pair(width):
        return [pl.BlockSpec((tm, width), first), pl.BlockSpec((tm, width), second)]

    return pl.pallas_call(
        functools.partial(_post_attn_kernel, n_first=n_first, tm=tm),
        out_shape=[jax.ShapeDtypeStruct((n, d), F32), jax.ShapeDtypeStruct((n, d), BF16),
                   jax.ShapeDtypeStruct((TOP_K, n), I32), jax.ShapeDtypeStruct((TOP_K, n), F32),
                   jax.ShapeDtypeStruct((n // DISPATCH_TILE, N_EXPERTS, LANES), I32)],
        grid=(n_tiles,),
        in_specs=pair(d) + pair(FOX_DIM) + pair(d) + pair(d) + [_full(w.shape) for w in (wpa, wo, g, wrt, br)],
        out_specs=[pl.BlockSpec((tm, d), row), pl.BlockSpec((tm, d), row),
                   pl.BlockSpec((TOP_K, tm), col), pl.BlockSpec((TOP_K, tm), col),
                   pl.BlockSpec((sub, N_EXPERTS, LANES), lambda i: (i, 0, 0))],
        compiler_params=_cparams("arbitrary"),
        name="post_attn_router",
    )(xp, xs, op, os_, sap, sas, mbp, mbs, wpa, wo, g, wrt, br)


def _for_each_run(tile, len_ref, loc_ref, dst_ref, fn):
    def body(e, carry):
        idx = tile * N_EXPERTS + e
        n, loc, dst = len_ref[idx], loc_ref[idx], dst_ref[idx]

        for c in RUN_CHUNKS:
            off = n & ~(2 * c - 1)

            @pl.when((n & c) != 0)
            def _():
                fn(loc + off, dst + off, c)
        return carry

    for e in range(N_EXPERTS):
        body(e, 0)


TILE_CHUNKS = tuple(1 << b for b in reversed(range(LOCAL_GROUPS.bit_length())))


def _wait_tile_rows(tile, len_ref, loc_ref, wait_rows):
    last = tile * N_EXPERTS + N_EXPERTS - 1
    total = loc_ref[last] + len_ref[last]
    for c in TILE_CHUNKS:
        @pl.when((total & c) != 0)
        def _():
            wait_rows(c)


def _dispatch_kernel(len_ref, loc_ref, dst_ref, xn_ref, pos_ref, xs_hbm, xloc_ref, sem_ref, *, n_tiles):
    i = pl.program_id(0)
    slot = lax.rem(i, 2)

    def copy(s, loc, dst, groups):
        return pltpu.make_async_copy(xloc_ref.at[s, pl.ds(loc, groups)], xs_hbm.at[pl.ds(dst, groups)], sem_ref.at[s])

    def start_tile(tile, s):
        _for_each_run(tile, len_ref, loc_ref, dst_ref, lambda loc, dst, rows: copy(s, loc, dst, rows).start())

    def wait_tile(tile, s):
        _wait_tile_rows(tile, len_ref, loc_ref, lambda rows: copy(s, 0, 0, rows).wait())

    pos = pos_ref[...]
    aio = lax.broadcasted_iota(I32, (LOCAL_ROWS, DISPATCH_TILE), 0)
    sel = jnp.zeros((LOCAL_ROWS, DISPATCH_TILE), F32)
    for k in range(TOP_K):
        sel = jnp.where(aio == pos[k:k + 1, :], 1.0, sel)
    rows_sorted = _dot(sel.astype(BF16), xn_ref[...])

    @pl.when(i >= 2)
    def _():
        wait_tile(i - 2, slot)

    xloc_ref[slot] = rows_sorted.reshape(xloc_ref.shape[1:])
    start_tile(i, slot)

    @pl.when(i == n_tiles - 1)
    def _():
        if n_tiles >= 2:
            wait_tile(i - 1, 1 - slot)
        wait_tile(i, slot)


def _dispatch(run_len, run_loc, run_dst, xn, pos, *, total_rows):
    n, d = xn.shape
    td = DISPATCH_TILE
    n_tiles = n // td
    return pl.pallas_call(
        functools.partial(_dispatch_kernel, n_tiles=n_tiles),
        out_shape=jax.ShapeDtypeStruct((total_rows // ROW_ALIGN, ROW_ALIGN, d), F32),
        grid_spec=pltpu.PrefetchScalarGridSpec(
            num_scalar_prefetch=3, grid=(n_tiles,),
            in_specs=[pl.BlockSpec((td, d), lambda i, *_: (i, 0)), pl.BlockSpec((TOP_K, td), lambda i, *_: (0, i))],
            out_specs=pl.BlockSpec(memory_space=pl.ANY),
            scratch_shapes=[pltpu.VMEM((2, LOCAL_GROUPS, ROW_ALIGN, d), F32), pltpu.SemaphoreType.DMA((2,))]),
        compiler_params=_cparams("arbitrary"),
        name="moe_dispatch",
    )(run_len, run_loc, run_dst, xn, pos)


def _ffn_kernel(bexp_ref, nval_ref, last_ref, xs_ref, wgu_ref, bgu_ref, wdn_ref, bdn_ref, o_ref, wgu_s, wdn_s):
    i = pl.program_id(0)
    nv = nval_ref[i]

    @pl.when(jnp.logical_or(i == 0, bexp_ref[i] != bexp_ref[jnp.maximum(i - 1, 0)]))
    def _():
        wgu_s[...] = wgu_ref[0].astype(BF16)
        wdn_s[...] = wdn_ref[0].astype(BF16)

    for r in range(FFN_ROWS // FFN_SUB):
        @pl.when(nv > r * FFN_SUB)
        def _():
            rows = lax.broadcasted_iota(I32, (FFN_SUB, xs_ref.shape[1]), 0) + r * FFN_SUB
            x = jnp.where(rows < nv, xs_ref[pl.ds(r * FFN_SUB, FFN_SUB), :], 0.0).astype(BF16)
            gu = _dot(x, wgu_s[...]) + bgu_ref[0]
            gt = jnp.minimum(gu[:, :D_FF], SWIGLU_LIMIT)
            up = jnp.clip(gu[:, D_FF:], -SWIGLU_LIMIT, SWIGLU_LIMIT)
            act = (up + 1.0) * gt * _sigmoid(SWIGLU_ALPHA * gt)
            o_ref[pl.ds(r * FFN_SUB, FFN_SUB), :] = _dot(act.astype(BF16), wdn_s[...]) + bdn_ref[0]


def _ffn(bexp, nval, last, xs, wgu, bgu, wdn, bdn):
    rows, d = xs.shape
    nblk = rows // FFN_ROWS
    blk = lambda i, bexp, nval, last: (jnp.minimum(i, last[0]), 0)
    exp3 = lambda i, bexp, nval, last: (bexp[i], 0, 0)
    return pl.pallas_call(
        _ffn_kernel,
        out_shape=jax.ShapeDtypeStruct((rows, d), F32),
        grid_spec=pltpu.PrefetchScalarGridSpec(
            num_scalar_prefetch=3, grid=(nblk,),
            in_specs=[pl.BlockSpec((FFN_ROWS, d), blk),
                      pl.BlockSpec((1, d, 2 * D_FF), exp3), pl.BlockSpec((1, 1, 2 * D_FF), exp3),
                      pl.BlockSpec((1, D_FF, d), exp3), pl.BlockSpec((1, 1, d), exp3)],
            out_specs=pl.BlockSpec((FFN_ROWS, d), blk),
            scratch_shapes=[pltpu.VMEM((d, 2 * D_FF), BF16), pltpu.VMEM((D_FF, d), BF16)]),
        compiler_params=_cparams("arbitrary"),
        name="moe_ffn",
    )(bexp, nval, last, xs, wgu, bgu, wdn, bdn)


def _combine_kernel(len_ref, loc_ref, dst_ref, os_hbm, pos_ref, gate_ref, h1_ref, pp_ref, ps_ref,
                    gple_ref, wpg_ref, wpp_ref, gfin_ref, yp_ref, ys_ref, oloc_ref, sem_ref, *, n_tiles, n_first):
    i = pl.program_id(0)
    slot = lax.rem(i, 2)

    def copy(s, loc, dst, groups):
        return pltpu.make_async_copy(os_hbm.at[pl.ds(dst, groups)], oloc_ref.at[s, pl.ds(loc, groups)], sem_ref.at[s])

    def start_tile(tile, s):
        _for_each_run(tile, len_ref, loc_ref, dst_ref, lambda loc, dst, rows: copy(s, loc, dst, rows).start())

    def wait_tile(tile, s):
        _wait_tile_rows(tile, len_ref, loc_ref, lambda rows: copy(s, 0, 0, rows).wait())

    @pl.when(i == 0)
    def _():
        oloc_ref[...] = jnp.zeros(oloc_ref.shape, F32)
        start_tile(0, 0)

    @pl.when(i + 1 < n_tiles)
    def _():
        start_tile(i + 1, 1 - slot)

    wait_tile(i, slot)

    pos, gate = pos_ref[...], gate_ref[...]
    lio = lax.broadcasted_iota(I32, (DISPATCH_TILE, LOCAL_ROWS), 1)
    w = jnp.zeros((DISPATCH_TILE, LOCAL_ROWS), F32)
    for k in range(TOP_K):
        w = jnp.where(lio == pos[:, k:k + 1], gate[:, k:k + 1], w)
    h2 = h1_ref[...] + _dot(w.astype(BF16), oloc_ref[slot].reshape(LOCAL_ROWS, -1).astype(BF16))

    is_p = i < n_first
    xn = _rms(h2, gple_ref[...]).astype(BF16)
    ple = _dot(_pick(is_p, pp_ref, ps_ref).astype(BF16), wpp_ref[...])
    h3 = h2 + _sigmoid(_dot(xn, wpg_ref[...])) * ple
    y = _rms(h3, gfin_ref[...])

    @pl.when(is_p)
    def _():
        yp_ref[...] = y

    @pl.when(jnp.logical_not(is_p))
    def _():
        ys_ref[...] = y


def _combine(run_len, run_loc, run_dst, os_, pos, gate, h1, pp, ps, gple, wpg, wpp, gfin):
    n, d = h1.shape
    td = DISPATCH_TILE
    n_p, n_s = pp.shape[0], ps.shape[0]
    assert n_p % td == 0 and n_s % td == 0
    n_tiles, n_first = n // td, n_p // td
    pw = pp.shape[1]
    first = lambda i, *_: (jnp.minimum(i, n_first - 1), 0)
    second = lambda i, *_: (jnp.maximum(i - n_first, 0), 0)
    row = lambda i, *_: (i, 0)
    full2 = lambda i, *_: (0, 0)
    return pl.pallas_call(
        functools.partial(_combine_kernel, n_tiles=n_tiles, n_first=n_first),
        out_shape=[jax.ShapeDtypeStruct((n_p, d), F32), jax.ShapeDtypeStruct((n_s, d), F32)],
        grid_spec=pltpu.PrefetchScalarGridSpec(
            num_scalar_prefetch=3, grid=(n_tiles,),
            in_specs=[pl.BlockSpec(memory_space=pl.ANY),
                      pl.BlockSpec((td, TOP_K), row), pl.BlockSpec((td, TOP_K), row), pl.BlockSpec((td, d), row),
                      pl.BlockSpec((td, pw), first), pl.BlockSpec((td, pw), second),
                      pl.BlockSpec(gple.shape, full2), pl.BlockSpec(wpg.shape, full2),
                      pl.BlockSpec(wpp.shape, full2), pl.BlockSpec(gfin.shape, full2)],
            out_specs=[pl.BlockSpec((td, d), first), pl.BlockSpec((td, d), second)],
            scratch_shapes=[pltpu.VMEM((2, LOCAL_GROUPS, ROW_ALIGN, d), F32), pltpu.SemaphoreType.DMA((2,))]),
        compiler_params=_cparams("arbitrary"),
        name="moe_combine_ple",
    )(run_len, run_loc, run_dst, os_, pos, gate, h1, pp, ps, gple, wpg, wpp, gfin)


def _routing_tables(cnt, total_rows):
    n_tiles = cnt.shape[0]
    run_len = (cnt + (ROW_ALIGN - 1)) // ROW_ALIGN * ROW_ALIGN
    run_loc = jnp.cumsum(run_len, axis=1) - run_len
    per_expert = jnp.sum(run_len, axis=0)
    region = (per_expert + (FFN_ROWS - 1)) // FFN_ROWS * FFN_ROWS
    region_end = jnp.cumsum(region)
    region_start = region_end - region
    run_dst = region_start[None, :] + jnp.cumsum(run_len, axis=0) - run_len
    nblk = total_rows // FFN_ROWS
    blk_row = jnp.arange(nblk, dtype=I32) * FFN_ROWS
    used = region_end[-1] // FFN_ROWS
    last = jnp.maximum(used - 1, 0)
    bexp = jnp.minimum(jnp.sum(blk_row[:, None] >= region_end[None, :], axis=1), N_EXPERTS - 1).astype(I32)
    nval = jnp.clip(per_expert[bexp] - (blk_row - region_start[bexp]), 0, FFN_ROWS)
    nval = jnp.where(blk_row < region_end[-1], nval, 0).astype(I32)
    bexp = jnp.where(blk_row < region_end[-1], bexp, bexp[last])
    flat = lambda a: (a // ROW_ALIGN).reshape(n_tiles * N_EXPERTS).astype(I32)
    return flat(run_len), flat(run_loc), flat(run_dst), bexp, nval, last.reshape(1).astype(I32)


def _chunked_t(lf, chunk):
    b, length, h = lf.shape
    return lf.reshape(b, length // chunk, chunk, h).transpose(0, 1, 3, 2)


def kernel(x_prompt, x_sample, p_prompt, p_sample, cache_k, cache_v, cache_logf, state_conv, g_mix, w_in, b_f, w_conv, w_pa, w_pb, w_o, g_ffn, w_router, b_router, w_gu, b_gu, w_dn, b_dn, g_ple, w_ple_gate, w_ple_proj, g_final):
    depth = g_mix.shape[0]
    assert depth == 1
    nb, seq, d = x_prompt.shape
    db, dseq, _ = x_sample.shape
    past = cache_k.shape[2]
    n_p, n_s = nb * seq, db * dseq

    w = w_in[0].astype(BF16)
    o_f = 3 * FOX_DIM
    o_c = o_f + HEADS
    o_g = o_c + 3 * CONV_DIM
    wf = jnp.pad(w[:, o_f:o_c], ((0, 0), (0, LANES - HEADS)))
    bf = jnp.pad(b_f[0], (0, LANES - HEADS)).reshape(1, LANES)
    mixer_w = (g_mix[0].reshape(1, d), w[:, :o_f], wf, bf, w[:, o_c:o_g], w_conv[0], w[:, o_g:], w_pb[0].astype(BF16))

    (q_p, k_p, v_p, ko_p, vo_p, logf_p, lfp_p, sa_p, mb_p), tail_p = _in_proj(
        x_prompt.reshape(n_p, d), jnp.zeros((nb, CONV_WIDTH - 1, CONV_DIM), F32), mixer_w, seq_len=seq)
    o_p = _attn_prompt(q_p, k_p, v_p, _cumsum_rows(lfp_p, seq=seq), nb=nb, seq=seq)

    (q_s, k_s, v_s, ko_s, vo_s, logf_s, _, sa_s, mb_s), tail_s = _in_proj(
        x_sample.reshape(n_s, d), state_conv[0], mixer_w, seq_len=dseq)
    chunk = min(past, 2 * TOKEN_TILE)
    lf_all = jnp.concatenate([cache_logf[0].astype(F32), logf_s.reshape(db, dseq, HEADS),
                              jnp.zeros((db, chunk - dseq, HEADS), F32)], axis=1)
    c_s = _cumsum_chunks(_chunked_t(lf_all, chunk))
    o_s = _attn_sample(q_s, cache_k[0].transpose(0, 2, 3, 1), cache_v[0].transpose(0, 2, 3, 1), k_s, v_s, c_s, nq=dseq)

    h1, xn2, pos_t, gate_t, cnt = _post_attn(
        x_prompt.reshape(n_p, d), x_sample.reshape(n_s, d), o_p, o_s, sa_p, sa_s, mb_p, mb_s,
        w_pa[0].astype(BF16), w_o[0].astype(BF16), g_ffn[0].reshape(1, d),
        w_router[0].T.astype(BF16), b_router[0].reshape(N_EXPERTS, 1))

    n = n_p + n_s
    n_tiles = n // DISPATCH_TILE
    max_rows = n * TOP_K + n_tiles * N_EXPERTS * (ROW_ALIGN - 1) + N_EXPERTS * (FFN_ROWS - 1)
    total_rows = -(-max_rows // FFN_ROWS) * FFN_ROWS
    run_len, run_loc, run_dst, bexp, nval, last = _routing_tables(cnt[:, :, 0], total_rows)
    xs = _dispatch(run_len, run_loc, run_dst, xn2, pos_t, total_rows=total_rows)
    os_ = _ffn(bexp, nval, last, xs.reshape(total_rows, d), w_gu[0], b_gu[0].reshape(N_EXPERTS, 1, 2 * D_FF),
               w_dn[0], b_dn[0].reshape(N_EXPERTS, 1, d))

    y_p, y_s = _combine(run_len, run_loc, run_dst, os_.reshape(total_rows // ROW_ALIGN, ROW_ALIGN, d), pos_t.T, gate_t.T, h1,
                        p_prompt[0].reshape(n_p, -1), p_sample[0].reshape(n_s, -1),
                        g_ple[0].reshape(1, d), w_ple_gate[0].astype(BF16), w_ple_proj[0].astype(BF16),
                        g_final.reshape(1, d))

    return (y_p.reshape(nb, seq, d), y_s.reshape(db, dseq, d),
            ko_p.transpose(0, 3, 1, 2)[None], vo_p.transpose(0, 3, 1, 2)[None],
            logf_p.reshape(1, nb, seq, HEADS), tail_p[None],
            ko_s.reshape(1, db, dseq, HEADS, HEAD_DIM), vo_s.reshape(1, db, dseq, HEADS, HEAD_DIM),
            logf_s.reshape(1, db, dseq, HEADS), tail_s[None])
```

```python
import functools

import jax
import jax.numpy as jnp
from jax import lax
from jax.experimental import pallas as pl
from jax.experimental.pallas import tpu as pltpu

F32, BF16, I32 = jnp.float32, jnp.bfloat16, jnp.int32

HEADS = 8
HEAD_DIM = 64
FOX_DIM = HEADS * HEAD_DIM
CONV_DIM = 512
CONV_WIDTH = 3
N_EXPERTS = 32
TOP_K = 4
D_FF = 1024
SWIGLU_ALPHA = 1.702
SWIGLU_LIMIT = 7.0
RMS_EPS = 1e-6
LOG2E = 1.4426950408889634

LANES = 128
SUBLANES = 8
TOKEN_TILE = 512
KEY_CHUNK = 1024
DISPATCH_TILE = 256
ROW_ALIGN = SUBLANES
LOCAL_ROWS = DISPATCH_TILE * TOP_K + N_EXPERTS * ROW_ALIGN
FFN_ROWS = 1024
FFN_SUB = 512
LOCAL_GROUPS = LOCAL_ROWS // ROW_ALIGN
RUN_CHUNKS = (32, 16, 8, 4, 2, 1)
VMEM_LIMIT_BYTES = 56 * 1024 * 1024


def _cparams(*sem):
    return pltpu.CompilerParams(dimension_semantics=sem, vmem_limit_bytes=VMEM_LIMIT_BYTES)


def _rms(x, g):
    return x * lax.rsqrt(jnp.mean(x * x, axis=-1, keepdims=True) + RMS_EPS) * g


def _sigmoid(x):
    return 1.0 / (1.0 + jnp.exp(-x))


def _log_sigmoid(x):
    return jnp.minimum(x, 0.0) - jnp.log1p(jnp.exp(-jnp.abs(x)))


def _dot(a, b):
    return jnp.dot(a, b, preferred_element_type=F32)


def _dot_nt(a, b):
    return lax.dot_general(a, b, (((1,), (1,)), ((), ())), preferred_element_type=F32)


def _mixer_inputs(x_ref, g_ref, wqkv_ref, wf_ref, bf_ref, wc_ref, wgl_ref,
                  q_ref, k_ref, v_ref, ko_ref, vo_ref, logf_ref, lfp_ref, sa_ref, *, keys_minor):
    xn = _rms(x_ref[...], g_ref[...]).astype(BF16)
    qkv = _dot(xn, wqkv_ref[...])
    q_ref[...] = (qkv[:, :FOX_DIM] * (HEAD_DIM ** -0.5 * LOG2E)).astype(BF16)
    k = qkv[:, FOX_DIM:2 * FOX_DIM]
    v = qkv[:, 2 * FOX_DIM:]
    k_ref[...] = k.astype(BF16)
    v_ref[...] = v.astype(BF16)
    if keys_minor:
        ko_ref[0] = k.T.reshape(HEADS, HEAD_DIM, k.shape[0])
        vo_ref[0] = v.T.reshape(HEADS, HEAD_DIM, v.shape[0])
    else:
        for h in range(HEADS):
            ko_ref[:, h, :] = k[:, h * HEAD_DIM:(h + 1) * HEAD_DIM]
            vo_ref[:, h, :] = v[:, h * HEAD_DIM:(h + 1) * HEAD_DIM]
    logf = _log_sigmoid(_dot(xn, wf_ref[...]) + bf_ref[...])
    logf_ref[...] = logf[:, :HEADS]
    lane = lax.broadcasted_iota(I32, logf.shape, 1)
    lfp_ref[...] = jnp.where(lane < HEADS, logf, 0.0)
    c3 = _dot(xn, wc_ref[...])
    gate_b = c3[:, :CONV_DIM]
    z = c3[:, CONV_DIM:2 * CONV_DIM] * c3[:, 2 * CONV_DIM:]
    gl = _dot(xn, wgl_ref[...])
    d = gl.shape[1] // 2
    sa_ref[...] = _sigmoid(gl[:, :d]).astype(BF16)
    return gate_b, z, _sigmoid(gl[:, d:])


def _conv_out(gate_b, z, z1, z2, sig_b, wconv_ref, wpb_ref, mb_ref):
    zc = wconv_ref[0:1, :] * z2 + wconv_ref[1:2, :] * z1 + wconv_ref[2:3, :] * z
    yb = _dot((gate_b * zc).astype(BF16), wpb_ref[...])
    mb_ref[...] = (sig_b * yb).astype(BF16)


N_MIXER_W = 8
N_MIXER_OUT = 9


def _mixer_front(x_ref, w_refs, out_refs, keys_minor):
    g_ref, wqkv_ref, wf_ref, bf_ref, wc_ref, wconv_ref, wgl_ref, wpb_ref = w_refs
    gate_b, z, sig_b = _mixer_inputs(x_ref, g_ref, wqkv_ref, wf_ref, bf_ref, wc_ref, wgl_ref, *out_refs[:-1],
                                     keys_minor=keys_minor)
    return gate_b, z, sig_b, wconv_ref, wpb_ref, out_refs[-1]


def _in_proj_seq_kernel(x_ref, prev_ref, *refs, tm):
    w_refs, out_refs = refs[:N_MIXER_W], refs[N_MIXER_W:N_MIXER_W + N_MIXER_OUT]
    tail_ref, zbuf_ref = refs[N_MIXER_W + N_MIXER_OUT:]
    gate_b, z, sig_b, wconv_ref, wpb_ref, mb_ref = _mixer_front(x_ref, w_refs, out_refs, True)
    zbuf_ref[pl.ds(SUBLANES, tm), :] = z

    @pl.when(pl.program_id(1) == 0)
    def _():
        zbuf_ref[pl.ds(SUBLANES - 2, 2), :] = prev_ref[0]

    z1 = zbuf_ref[pl.ds(SUBLANES - 1, tm), :]
    z2 = zbuf_ref[pl.ds(SUBLANES - 2, tm), :]
    _conv_out(gate_b, z, z1, z2, sig_b, wconv_ref, wpb_ref, mb_ref)
    tail = zbuf_ref[pl.ds(tm + SUBLANES - 2, 2), :]
    zbuf_ref[pl.ds(SUBLANES - 2, 2), :] = tail
    tail_ref[0] = tail


def _in_proj_multi_kernel(x_ref, ov1_ref, ov2_ref, *refs, tm, seq_len):
    w_refs, out_refs = refs[:N_MIXER_W], refs[N_MIXER_W:N_MIXER_W + N_MIXER_OUT]
    z_ref, zbuf_ref = refs[N_MIXER_W + N_MIXER_OUT:]
    gate_b, z, sig_b, wconv_ref, wpb_ref, mb_ref = _mixer_front(x_ref, w_refs, out_refs, False)
    z_ref[...] = z
    zbuf_ref[pl.ds(0, SUBLANES), :] = jnp.zeros((SUBLANES, CONV_DIM), F32)
    zbuf_ref[pl.ds(SUBLANES, tm), :] = z
    t = lax.broadcasted_iota(I32, (tm, CONV_DIM), 0) & (seq_len - 1)
    z1 = jnp.where(t == 0, ov1_ref[...], zbuf_ref[pl.ds(SUBLANES - 1, tm), :])
    z2 = jnp.where(t < 2, ov2_ref[...], zbuf_ref[pl.ds(SUBLANES - 2, tm), :])
    _conv_out(gate_b, z, z1, z2, sig_b, wconv_ref, wpb_ref, mb_ref)


def _full(shape):
    n = len(shape)
    return pl.BlockSpec(shape, lambda *_: (0,) * n)


def _in_proj(x, conv_prev, weights, *, seq_len):
    g, wqkv, wf, bf, wc, wconv, wgl, wpb = weights
    n, d = x.shape
    w_specs = [_full(w.shape) for w in (g, wqkv, wf, bf, wc, wconv, wgl, wpb)]
    shapes = (((FOX_DIM,), BF16), ((FOX_DIM,), BF16), ((FOX_DIM,), BF16), ((HEADS, HEAD_DIM), F32), ((HEADS, HEAD_DIM), F32),
              ((HEADS,), F32), ((LANES,), F32), ((d,), BF16), ((d,), BF16))
    assert len(weights) == N_MIXER_W and len(shapes) == N_MIXER_OUT
    out_shape = [jax.ShapeDtypeStruct((n,) + s, t) for s, t in shapes]
    n_common = N_MIXER_OUT
    tm = TOKEN_TILE

    def out_specs(tile_index):
        return [pl.BlockSpec((tm,) + s, lambda *a, k=len(s): (tile_index(*a),) + (0,) * k) for s, _ in shapes]

    if seq_len % tm == 0:
        nb, nj = n // seq_len, seq_len // tm
        row = lambda b, j: (b * nj + j, 0)
        specs = out_specs(lambda b, j: b * nj + j)
        for i in (3, 4):
            out_shape[i] = jax.ShapeDtypeStruct((nb, HEADS, HEAD_DIM, seq_len), F32)
            specs[i] = pl.BlockSpec((1, HEADS, HEAD_DIM, tm), lambda b, j: (b, 0, 0, j))
        outs = pl.pallas_call(
            functools.partial(_in_proj_seq_kernel, tm=tm),
            out_shape=out_shape + [jax.ShapeDtypeStruct((nb, CONV_WIDTH - 1, CONV_DIM), F32)],
            grid=(nb, nj),
            in_specs=[pl.BlockSpec((tm, d), row), pl.BlockSpec((1, CONV_WIDTH - 1, CONV_DIM), lambda b, j: (b, 0, 0))] + w_specs,
            out_specs=specs + [pl.BlockSpec((1, CONV_WIDTH - 1, CONV_DIM), lambda b, j: (b, 0, 0))],
            scratch_shapes=[pltpu.VMEM((tm + SUBLANES, CONV_DIM), F32)],
            compiler_params=_cparams("arbitrary", "arbitrary"),
            name="in_proj_seq",
        )(x, conv_prev, g, wqkv, wf, bf, wc, wconv, wgl, wpb)
        return outs[:n_common], outs[n_common]
    assert seq_len & (seq_len - 1) == 0 and seq_len >= CONV_WIDTH - 1
    assert n % tm == 0 and tm % seq_len == 0
    first = jnp.zeros((n // seq_len, seq_len, CONV_DIM), F32)
    ov1 = first.at[:, 0].set(conv_prev[:, 1]).reshape(n, CONV_DIM)
    ov2 = first.at[:, 0].set(conv_prev[:, 0]).at[:, 1].set(conv_prev[:, 1]).reshape(n, CONV_DIM)
    row = lambda i: (i, 0)
    outs = pl.pallas_call(
        functools.partial(_in_proj_multi_kernel, tm=tm, seq_len=seq_len),
        out_shape=out_shape + [jax.ShapeDtypeStruct((n, CONV_DIM), F32)],
        grid=(n // tm,),
        in_specs=[pl.BlockSpec((tm, d), row), pl.BlockSpec((tm, CONV_DIM), row), pl.BlockSpec((tm, CONV_DIM), row)] + w_specs,
        out_specs=out_specs(lambda i: i) + [pl.BlockSpec((tm, CONV_DIM), row)],
        scratch_shapes=[pltpu.VMEM((tm + SUBLANES, CONV_DIM), F32)],
        compiler_params=_cparams("arbitrary"),
        name="in_proj_multi",
    )(x, ov1, ov2, g, wqkv, wf, bf, wc, wconv, wgl, wpb)
    tail = outs[n_common].reshape(n // seq_len, seq_len, CONV_DIM)[:, seq_len - (CONV_WIDTH - 1):]
    return outs[:n_common], tail


def _cumsum_kernel(lf_ref, c_ref, *, chunk, nchunk):
    r = lax.broadcasted_iota(I32, (chunk, chunk), 0)
    c = lax.broadcasted_iota(I32, (chunk, chunk), 1)
    upper = jnp.where(r <= c, 1.0, 0.0).astype(BF16)
    carry = jnp.zeros((HEADS, 1), F32)
    for n in range(nchunk):
        rest, terms = lf_ref[0, n], []
        for _ in range(C_TERMS):
            terms.append(rest.astype(BF16))
            rest = rest - terms[-1].astype(F32)
        sums = _dot(jnp.concatenate(terms, axis=0), upper)
        cs = carry
        for j in range(C_TERMS):
            cs = cs + sums[j * HEADS:(j + 1) * HEADS]
        c_ref[0, n] = cs
        carry = cs[:, chunk - 1:chunk]


def _cumsum_chunks(lf):
    b, nchunk, _, chunk = lf.shape
    spec = pl.BlockSpec((1, nchunk, HEADS, chunk), lambda i: (i, 0, 0, 0))
    return pl.pallas_call(
        functools.partial(_cumsum_kernel, chunk=chunk, nchunk=nchunk),
        out_shape=jax.ShapeDtypeStruct(lf.shape, F32),
        grid=(b,), in_specs=[spec], out_specs=spec,
        compiler_params=_cparams("arbitrary"),
        name="logf_cumsum",
    )(lf)


C_TERMS = 3


def _cumsum_rows_kernel(lf_ref, cs_ref, *, chunk, nchunk):
    r = lax.broadcasted_iota(I32, (chunk, chunk), 0)
    c = lax.broadcasted_iota(I32, (chunk, chunk), 1)
    lower = jnp.where(c <= r, 1.0, 0.0).astype(BF16)
    carry = jnp.zeros((1, LANES), F32)
    for n in range(nchunk):
        rest = lf_ref[pl.ds(n * chunk, chunk), :]
        cs = carry
        for _ in range(C_TERMS):
            term = rest.astype(BF16)
            cs = cs + _dot(lower, term)
            rest = rest - term.astype(F32)
        carry = cs[chunk - 1:chunk, :]
        packed = jnp.zeros((chunk, LANES), F32)
        rest = cs * LOG2E
        for j in range(C_TERMS):
            term = rest.astype(BF16).astype(F32)
            packed = packed + (term if j == 0 else pltpu.roll(term, j * HEADS, 1))
            rest = rest - term
        cs_ref[pl.ds(n * chunk, chunk), :] = packed.astype(BF16)


def _cumsum_rows(lfp, *, seq):
    n = lfp.shape[0]
    spec = pl.BlockSpec((seq, LANES), lambda b: (b, 0))
    return pl.pallas_call(
        functools.partial(_cumsum_rows_kernel, chunk=TOKEN_TILE, nchunk=seq // TOKEN_TILE),
        out_shape=jax.ShapeDtypeStruct((n, LANES), BF16),
        grid=(n // seq,), in_specs=[spec], out_specs=spec,
        compiler_params=_cparams("arbitrary"),
        name="logf_cumsum_rows",
    )(lfp)


SAFE_LOGIT = 64.0


def _attn_prompt_kernel(q_ref, k_ref, v_ref, cs_ref, o_ref, ka_ref, vt_ref, kn_ref, m_ref, acc_ref, *, t, tk, nk):
    assert tk == t
    hp, i = pl.program_id(1), pl.program_id(2)
    rr = lax.broadcasted_iota(I32, (LANES, LANES), 0)
    cc = lax.broadcasted_iota(I32, (LANES, LANES), 1)
    lane = lax.broadcasted_iota(I32, (1, LANES), 1)
    head_lanes = [lane < HEAD_DIM, lane >= HEAD_DIM]
    sel = [jnp.where(jnp.where(cc < HEAD_DIM, rr - cc, -1) == HEAD_DIM * h, 1.0, 0.0).astype(BF16) for h in range(2)]

    def place(h, first_lane, value):
        src = jnp.where(cc >= first_lane, (cc - first_lane) * HEADS + 2 * hp + h, -1)
        src = jnp.where(cc < first_lane + C_TERMS, src, -1)
        return jnp.where(rr == src, value, 0.0).astype(BF16)

    def ones_at(first_lane):
        return jnp.where(jnp.logical_and(lane >= first_lane, lane < first_lane + C_TERMS), 1.0, 0.0)

    def max_sq_norm(x, h):
        sq = jnp.sum(jnp.where(head_lanes[h], x * x, 0.0), axis=1, keepdims=True)
        return jnp.max(sq, axis=0, keepdims=True)

    key_lane, query_lane = HEAD_DIM, HEAD_DIM + C_TERMS

    @pl.when(i == 0)
    def _():
        kb = k_ref[...]
        cs = cs_ref[...]
        kf = kb.astype(F32)
        vt = v_ref[...].astype(F32).T
        row = lax.broadcasted_iota(I32, (SUBLANES, tk), 0)
        ones = jnp.where(row == 0, 1.0, 0.0)
        pad = jnp.zeros((LANES - HEAD_DIM - SUBLANES, tk), F32)
        for h in range(2):
            ka_ref[h] = (_dot(kb, sel[h]) + _dot(cs, place(h, key_lane, -1.0)) + ones_at(query_lane)).astype(BF16)
            kn_ref[h] = jnp.broadcast_to(max_sq_norm(kf, h), kn_ref.shape[1:])
            for n in range(nk):
                vh = vt[h * HEAD_DIM:(h + 1) * HEAD_DIM, n * tk:(n + 1) * tk]
                vt_ref[h, n] = jnp.concatenate([vh, ones, pad], axis=0).astype(BF16)

    q = q_ref[...]
    cq = cs_ref[pl.ds(pl.multiple_of(i * t, t), t), :]
    qa = [(_dot(q, sel[h]) + _dot(cq, place(h, query_lane, 1.0)) + ones_at(key_lane)).astype(BF16) for h in range(2)]
    qf = q.astype(F32)
    bound_sq = jnp.maximum(max_sq_norm(qf, 0) * kn_ref[0, 0:1, 0:1], max_sq_norm(qf, 1) * kn_ref[1, 0:1, 0:1])
    in_range = bound_sq[0, 0] < SAFE_LOGIT * SAFE_LOGIT

    def scores(j):
        start = pl.multiple_of(j * tk, tk)
        return tuple(_dot_nt(ka_ref[h, pl.ds(start, tk), :], qa[h]) for h in range(2))

    def visible(st, limit):
        ki = lax.broadcasted_iota(I32, (tk, t), 0)
        qi = lax.broadcasted_iota(I32, (tk, t), 1)
        return jnp.where(ki <= qi + limit, st, -jnp.inf)

    @pl.when(in_range)
    def _():
        st_pair = scores(i)
        for h in range(2):
            acc_ref[h] = _dot(vt_ref[h, i], jnp.exp2(visible(st_pair[h], 0)).astype(BF16))

        def add_chunks(chunks):
            st_pairs = [scores(j) for j in chunks]
            for h in range(2):
                total = acc_ref[h]
                for j, st_pair in zip(chunks, st_pairs):
                    total = total + _dot(vt_ref[h, j], jnp.exp2(st_pair[h]).astype(BF16))
                acc_ref[h] = total

        def body(n, carry):
            add_chunks([2 * n, 2 * n + 1])
            return carry

        lax.fori_loop(0, i // 2, body, 0)

        @pl.when(i % 2 == 1)
        def _():
            add_chunks([i - 1])

    @pl.when(jnp.logical_not(in_range))
    def _():
        m_ref[0] = jnp.full(m_ref.shape[1:], -jnp.inf, F32)
        acc_ref[...] = jnp.zeros(acc_ref.shape, F32)

        def limited(j, limit):
            st_pair = scores(j)
            return st_pair if limit is None else [visible(st, limit) for st in st_pair]

        def max_pass(j, limit):
            st_pair = limited(j, limit)
            for h in range(2):
                m_ref[j + 1, h] = jnp.maximum(m_ref[j, h], jnp.max(st_pair[h], axis=0, keepdims=True))

        def weight_pass(j, limit):
            st_pair = limited(j, limit)
            for h in range(2):
                m_new = m_ref[j + 1, h]
                p = jnp.exp2(st_pair[h] - m_new).astype(BF16)
                acc_ref[h] = acc_ref[h] * jnp.exp2(m_ref[j, h] - m_new) + _dot(vt_ref[h, j], p)

        max_pass(0, i * t)

        def body(j, carry):
            max_pass(j + 1, None)
            weight_pass(j, None)
            return carry

        lax.fori_loop(0, i - 1, body, 0)

        @pl.when(i >= 1)
        def _():
            max_pass(i, 0)
            weight_pass(i - 1, None)

        weight_pass(i, 0)

    halves = []
    for h in range(2):
        acc = acc_ref[h]
        halves.append(acc[:HEAD_DIM] * (1.0 / acc[HEAD_DIM:HEAD_DIM + 1]))
    o_ref[...] = jnp.concatenate(halves, axis=0).T.astype(BF16)


def _attn_prompt(q, k, v, cs, *, nb, seq):
    t = tk = min(KEY_CHUNK, seq)
    nq, nk = seq // t, seq // tk
    return pl.pallas_call(
        functools.partial(_attn_prompt_kernel, t=t, tk=tk, nk=nk),
        out_shape=jax.ShapeDtypeStruct(q.shape, BF16),
        grid=(nb, HEADS // 2, nq),
        in_specs=[pl.BlockSpec((t, LANES), lambda b, hp, i: (b * nq + i, hp)),
                  pl.BlockSpec((seq, LANES), lambda b, hp, i: (b, hp)),
                  pl.BlockSpec((seq, LANES), lambda b, hp, i: (b, hp)),
                  pl.BlockSpec((seq, LANES), lambda b, hp, i: (b, 0))],
        out_specs=pl.BlockSpec((t, LANES), lambda b, hp, i: (b * nq + i, hp)),
        scratch_shapes=[pltpu.VMEM((2, seq, LANES), BF16), pltpu.VMEM((2, nk, LANES, tk), BF16),
                        pltpu.VMEM((2, SUBLANES, LANES), F32),
                        pltpu.VMEM((nk + 1, 2, 1, t), F32), pltpu.VMEM((2, LANES, t), F32)],
        compiler_params=_cparams("arbitrary", "arbitrary", "arbitrary"),
        name="attn_prompt",
    )(q, k, v, cs)


def _attn_sample_kernel(q_ref, kc_ref, vc_ref, kn_ref, vn_ref, c_ref, o_ref,
                        qbd_ref, m_ref, l_ref, acc_ref, kpad_ref, vpad_ref, *, nq, nchunk):
    j = pl.program_id(1)
    rows = HEADS * nq
    row_head = lax.broadcasted_iota(I32, (rows, FOX_DIM), 0) >> (nq.bit_length() - 1)
    col_head = lax.broadcasted_iota(I32, (rows, FOX_DIM), 1) >> (HEAD_DIM.bit_length() - 1)
    own = row_head == col_head

    @pl.when(j == 0)
    def _():
        qt = jnp.concatenate([q_ref[...]] * HEADS, axis=0)
        qbd_ref[...] = jnp.where(own, qt, jnp.zeros_like(qt))
        m_ref[...] = jnp.full(m_ref.shape, -jnp.inf, F32)
        l_ref[...] = jnp.zeros(l_ref.shape, F32)
        acc_ref[...] = jnp.zeros(acc_ref.shape, F32)

    def update(s, cvals, visible, weighted_values):
        width = s.shape[1]
        bias = jnp.concatenate([jnp.broadcast_to(cvals[h:h + 1, :], (nq, width)) for h in range(HEADS)], axis=0)
        s = s - LOG2E * bias
        if visible is not None:
            s = jnp.where(visible, s, -jnp.inf)
        m_prev = m_ref[...]
        m_new = jnp.maximum(m_prev, jnp.max(s, axis=-1, keepdims=True))
        a = jnp.exp2(m_prev - m_new)
        p = jnp.exp2(s - m_new)
        l_ref[...] = a * l_ref[...] + jnp.sum(p, axis=-1, keepdims=True)
        m_ref[...] = m_new
        acc_ref[...] = acc_ref[...] * a + weighted_values(p.astype(BF16))

    @pl.when(j < nchunk)
    def _():
        chunk = kc_ref.shape[3]
        kt = kc_ref[0].reshape(FOX_DIM, chunk).astype(BF16)
        vt = vc_ref[0].reshape(FOX_DIM, chunk).astype(BF16)
        update(_dot(qbd_ref[...], kt), c_ref[0, j], None, lambda p: _dot_nt(p, vt))

    @pl.when(j == nchunk)
    def _():
        kpad_ref[...] = jnp.zeros(kpad_ref.shape, BF16)
        vpad_ref[...] = jnp.zeros(vpad_ref.shape, BF16)
        kpad_ref[pl.ds(0, nq), :] = kn_ref[...]
        vpad_ref[pl.ds(0, nq), :] = vn_ref[...]
        ki = lax.broadcasted_iota(I32, (rows, LANES), 1)
        qi = lax.broadcasted_iota(I32, (rows, LANES), 0) & (nq - 1)
        update(_dot_nt(qbd_ref[...], kpad_ref[...]), c_ref[0, nchunk][:, :LANES], ki <= qi,
               lambda p: _dot(p, vpad_ref[...]))
        out = jnp.where(own, acc_ref[...] * (1.0 / l_ref[...]), 0.0)
        o = out[0:nq]
        for h in range(1, HEADS):
            o = o + out[h * nq:(h + 1) * nq]
        o_ref[...] = o.astype(BF16)


def _attn_sample(q, k_cache, v_cache, k_new, v_new, c, *, nq):
    nb, past = k_cache.shape[0], k_cache.shape[3]
    chunk = c.shape[-1]
    nchunk = past // chunk
    assert nq & (nq - 1) == 0 and nq <= LANES and past % chunk == 0 and c.shape[1] == nchunk + 1
    rows = HEADS * nq
    cache_spec = pl.BlockSpec((1, HEADS, HEAD_DIM, chunk), lambda b, j: (b, 0, 0, jnp.minimum(j, nchunk - 1)))
    new_spec = pl.BlockSpec((nq, FOX_DIM), lambda b, j: (b, 0))
    return pl.pallas_call(
        functools.partial(_attn_sample_kernel, nq=nq, nchunk=nchunk),
        out_shape=jax.ShapeDtypeStruct(q.shape, BF16),
        grid=(nb, nchunk + 1),
        in_specs=[new_spec, cache_spec, cache_spec, new_spec, new_spec,
                  pl.BlockSpec((1, nchunk + 1, HEADS, chunk), lambda b, j: (b, 0, 0, 0))],
        out_specs=new_spec,
        scratch_shapes=[pltpu.VMEM((rows, FOX_DIM), BF16), pltpu.VMEM((rows, 1), F32), pltpu.VMEM((rows, 1), F32),
                        pltpu.VMEM((rows, FOX_DIM), F32), pltpu.VMEM((LANES, FOX_DIM), BF16),
                        pltpu.VMEM((LANES, FOX_DIM), BF16)],
        compiler_params=_cparams("arbitrary", "arbitrary"),
        name="attn_sample",
    )(q, k_cache, v_cache, k_new, v_new, c)


def _pick(is_first, a_ref, b_ref):
    return jnp.where(is_first, a_ref[...], b_ref[...])


def _post_attn_kernel(xp_ref, xs_ref, op_ref, os_ref, sap_ref, sas_ref, mbp_ref, mbs_ref,
                      wpa_ref, wo_ref, g_ref, wrt_ref, br_ref,
                      h1_ref, xn_ref, pos_ref, gate_ref, cnt_ref, *, n_first, tm):
    is_p = pl.program_id(0) < n_first
    ya = _dot(_pick(is_p, op_ref, os_ref), wpa_ref[...])
    merged = _pick(is_p, sap_ref, sas_ref).astype(F32) * ya + _pick(is_p, mbp_ref, mbs_ref).astype(F32)
    h1 = _pick(is_p, xp_ref, xs_ref) + _dot(merged.astype(BF16), wo_ref[...])
    h1_ref[...] = h1
    xn = _rms(h1, g_ref[...]).astype(BF16)
    xn_ref[...] = xn

    lt = _dot_nt(wrt_ref[...], xn) + br_ref[...]
    eio = lax.broadcasted_iota(I32, (N_EXPERTS, tm), 0).astype(F32)
    vals, hots = [], []
    for _ in range(TOP_K):
        m = jnp.max(lt, axis=0, keepdims=True)
        idx = jnp.min(jnp.where(lt == m, eio, float(N_EXPERTS)), axis=0, keepdims=True)
        hot = eio == idx
        vals.append(m)
        hots.append(hot)
        lt = jnp.where(hot, -jnp.inf, lt)
    ex = [jnp.exp(v - vals[0]) for v in vals]
    den = ex[0] + ex[1] + ex[2] + ex[3]
    gate_ref[...] = jnp.concatenate([e / den for e in ex], axis=0)

    chosen = jnp.zeros((N_EXPERTS, tm), F32)
    for hot in hots:
        chosen = jnp.where(hot, 1.0, chosen)
    td = DISPATCH_TILE
    r = lax.broadcasted_iota(I32, (td, td), 0)
    c = lax.broadcasted_iota(I32, (td, td), 1)
    before = jnp.where(r < c, 1.0, 0.0).astype(BF16)
    er = lax.broadcasted_iota(I32, (N_EXPERTS, N_EXPERTS), 0)
    ec = lax.broadcasted_iota(I32, (N_EXPERTS, N_EXPERTS), 1)
    lower = jnp.where(ec < er, 1.0, 0.0).astype(BF16)
    for sub in range(tm // td):
        sl = slice(sub * td, (sub + 1) * td)
        ch = chosen[:, sl]
        rank = _dot(ch.astype(BF16), before)
        cnt = rank[:, td - 1:td] + ch[:, td - 1:td]
        units = jnp.floor((cnt + (ROW_ALIGN - 1)) * (1.0 / ROW_ALIGN))
        start = ROW_ALIGN * _dot(lower, jnp.broadcast_to(units, (N_EXPERTS, td)).astype(BF16))
        base = start + rank
        pos = [jnp.sum(jnp.where(hot[:, sl], base, 0.0), axis=0, keepdims=True) for hot in hots]
        pos_ref[:, sl] = jnp.concatenate(pos, axis=0).astype(I32)
        cnt_ref[sub] = jnp.broadcast_to(cnt, (N_EXPERTS, LANES)).astype(I32)


def _post_attn(xp, xs, op, os_, sap, sas, mbp, mbs, wpa, wo, g, wrt, br):
    n_p, d = xp.shape
    n_s = xs.shape[0]
    tm = TOKEN_TILE
    assert n_p % tm == 0 and n_s % tm == 0
    n_first, n_tiles = n_p // tm, (n_p + n_s) // tm
    n = n_p + n_s
    sub = tm // DISPATCH_TILE
    first = lambda i: (jnp.minimum(i, n_first - 1), 0)
    second = lambda i: (jnp.maximum(i - n_first, 0), 0)
    row = lambda i: (i, 0)
    col = lambda i: (0, i)

    def pair(width):
        return [pl.BlockSpec((tm, width), first), pl.BlockSpec((tm, width), second)]

    return pl.pallas_call(
        functools.partial(_post_attn_kernel, n_first=n_first, tm=tm),
        out_shape=[jax.ShapeDtypeStruct((n, d), F32), jax.ShapeDtypeStruct((n, d), BF16),
                   jax.ShapeDtypeStruct((TOP_K, n), I32), jax.ShapeDtypeStruct((TOP_K, n), F32),
                   jax.ShapeDtypeStruct((n // DISPATCH_TILE, N_EXPERTS, LANES), I32)],
        grid=(n_tiles,),
        in_specs=pair(d) + pair(FOX_DIM) + pair(d) + pair(d) + [_full(w.shape) for w in (wpa, wo, g, wrt, br)],
        out_specs=[pl.BlockSpec((tm, d), row), pl.BlockSpec((tm, d), row),
                   pl.BlockSpec((TOP_K, tm), col), pl.BlockSpec((TOP_K, tm), col),
                   pl.BlockSpec((sub, N_EXPERTS, LANES), lambda i: (i, 0, 0))],
        compiler_params=_cparams("arbitrary"),
        name="post_attn_router",
    )(xp, xs, op, os_, sap, sas, mbp, mbs, wpa, wo, g, wrt, br)


def _for_each_run(tile, len_ref, loc_ref, dst_ref, fn):
    def body(e, carry):
        idx = tile * N_EXPERTS + e
        n, loc, dst = len_ref[idx], loc_ref[idx], dst_ref[idx]

        for c in RUN_CHUNKS:
            off = n & ~(2 * c - 1)

            @pl.when((n & c) != 0)
            def _():
                fn(loc + off, dst + off, c)
        return carry

    for e in range(N_EXPERTS):
        body(e, 0)


TILE_CHUNKS = tuple(1 << b for b in reversed(range(LOCAL_GROUPS.bit_length())))


def _wait_tile_rows(tile, len_ref, loc_ref, wait_rows):
    last = tile * N_EXPERTS + N_EXPERTS - 1
    total = loc_ref[last] + len_ref[last]
    for c in TILE_CHUNKS:
        @pl.when((total & c) != 0)
        def _():
            wait_rows(c)


def _dispatch_kernel(len_ref, loc_ref, dst_ref, xn_ref, pos_ref, xs_hbm, xloc_ref, sem_ref, *, n_tiles):
    i = pl.program_id(0)
    slot = lax.rem(i, 2)

    def copy(s, loc, dst, groups):
        return pltpu.make_async_copy(xloc_ref.at[s, pl.ds(loc, groups)], xs_hbm.at[pl.ds(dst, groups)], sem_ref.at[s])

    def start_tile(tile, s):
        _for_each_run(tile, len_ref, loc_ref, dst_ref, lambda loc, dst, rows: copy(s, loc, dst, rows).start())

    def wait_tile(tile, s):
        _wait_tile_rows(tile, len_ref, loc_ref, lambda rows: copy(s, 0, 0, rows).wait())

    pos = pos_ref[...]
    aio = lax.broadcasted_iota(I32, (LOCAL_ROWS, DISPATCH_TILE), 0)
    sel = jnp.zeros((LOCAL_ROWS, DISPATCH_TILE), F32)
    for k in range(TOP_K):
        sel = jnp.where(aio == pos[k:k + 1, :], 1.0, sel)
    rows_sorted = _dot(sel.astype(BF16), xn_ref[...])

    @pl.when(i >= 2)
    def _():
        wait_tile(i - 2, slot)

    xloc_ref[slot] = rows_sorted.reshape(xloc_ref.shape[1:])
    start_tile(i, slot)

    @pl.when(i == n_tiles - 1)
    def _():
        if n_tiles >= 2:
            wait_tile(i - 1, 1 - slot)
        wait_tile(i, slot)


def _dispatch(run_len, run_loc, run_dst, xn, pos, *, total_rows):
    n, d = xn.shape
    td = DISPATCH_TILE
    n_tiles = n // td
    return pl.pallas_call(
        functools.partial(_dispatch_kernel, n_tiles=n_tiles),
        out_shape=jax.ShapeDtypeStruct((total_rows // ROW_ALIGN, ROW_ALIGN, d), F32),
        grid_spec=pltpu.PrefetchScalarGridSpec(
            num_scalar_prefetch=3, grid=(n_tiles,),
            in_specs=[pl.BlockSpec((td, d), lambda i, *_: (i, 0)), pl.BlockSpec((TOP_K, td), lambda i, *_: (0, i))],
            out_specs=pl.BlockSpec(memory_space=pl.ANY),
            scratch_shapes=[pltpu.VMEM((2, LOCAL_GROUPS, ROW_ALIGN, d), F32), pltpu.SemaphoreType.DMA((2,))]),
        compiler_params=_cparams("arbitrary"),
        name="moe_dispatch",
    )(run_len, run_loc, run_dst, xn, pos)


def _ffn_kernel(bexp_ref, nval_ref, last_ref, xs_ref, wgu_ref, bgu_ref, wdn_ref, bdn_ref, o_ref, wgu_s, wdn_s):
    i = pl.program_id(0)
    nv = nval_ref[i]

    @pl.when(jnp.logical_or(i == 0, bexp_ref[i] != bexp_ref[jnp.maximum(i - 1, 0)]))
    def _():
        wgu_s[...] = wgu_ref[0].astype(BF16)
        wdn_s[...] = wdn_ref[0].astype(BF16)

    for r in range(FFN_ROWS // FFN_SUB):
        @pl.when(nv > r * FFN_SUB)
        def _():
            rows = lax.broadcasted_iota(I32, (FFN_SUB, xs_ref.shape[1]), 0) + r * FFN_SUB
            x = jnp.where(rows < nv, xs_ref[pl.ds(r * FFN_SUB, FFN_SUB), :], 0.0).astype(BF16)
            gu = _dot(x, wgu_s[...]) + bgu_ref[0]
            gt = jnp.minimum(gu[:, :D_FF], SWIGLU_LIMIT)
            up = jnp.clip(gu[:, D_FF:], -SWIGLU_LIMIT, SWIGLU_LIMIT)
            act = (up + 1.0) * gt * _sigmoid(SWIGLU_ALPHA * gt)
            o_ref[pl.ds(r * FFN_SUB, FFN_SUB), :] = _dot(act.astype(BF16), wdn_s[...]) + bdn_ref[0]


def _ffn(bexp, nval, last, xs, wgu, bgu, wdn, bdn):
    rows, d = xs.shape
    nblk = rows // FFN_ROWS
    blk = lambda i, bexp, nval, last: (jnp.minimum(i, last[0]), 0)
    exp3 = lambda i, bexp, nval, last: (bexp[i], 0, 0)
    return pl.pallas_call(
        _ffn_kernel,
        out_shape=jax.ShapeDtypeStruct((rows, d), F32),
        grid_spec=pltpu.PrefetchScalarGridSpec(
            num_scalar_prefetch=3, grid=(nblk,),
            in_specs=[pl.BlockSpec((FFN_ROWS, d), blk),
                      pl.BlockSpec((1, d, 2 * D_FF), exp3), pl.BlockSpec((1, 1, 2 * D_FF), exp3),
                      pl.BlockSpec((1, D_FF, d), exp3), pl.BlockSpec((1, 1, d), exp3)],
            out_specs=pl.BlockSpec((FFN_ROWS, d), blk),
            scratch_shapes=[pltpu.VMEM((d, 2 * D_FF), BF16), pltpu.VMEM((D_FF, d), BF16)]),
        compiler_params=_cparams("arbitrary"),
        name="moe_ffn",
    )(bexp, nval, last, xs, wgu, bgu, wdn, bdn)


def _combine_kernel(len_ref, loc_ref, dst_ref, os_hbm, pos_ref, gate_ref, h1_ref, pp_ref, ps_ref,
                    gple_ref, wpg_ref, wpp_ref, gfin_ref, yp_ref, ys_ref, oloc_ref, sem_ref, *, n_tiles, n_first):
    i = pl.program_id(0)
    slot = lax.rem(i, 2)

    def copy(s, loc, dst, groups):
        return pltpu.make_async_copy(os_hbm.at[pl.ds(dst, groups)], oloc_ref.at[s, pl.ds(loc, groups)], sem_ref.at[s])

    def start_tile(tile, s):
        _for_each_run(tile, len_ref, loc_ref, dst_ref, lambda loc, dst, rows: copy(s, loc, dst, rows).start())

    def wait_tile(tile, s):
        _wait_tile_rows(tile, len_ref, loc_ref, lambda rows: copy(s, 0, 0, rows).wait())

    @pl.when(i == 0)
    def _():
        oloc_ref[...] = jnp.zeros(oloc_ref.shape, F32)
        start_tile(0, 0)

    @pl.when(i + 1 < n_tiles)
    def _():
        start_tile(i + 1, 1 - slot)

    wait_tile(i, slot)

    pos, gate = pos_ref[...], gate_ref[...]
    lio = lax.broadcasted_iota(I32, (DISPATCH_TILE, LOCAL_ROWS), 1)
    w = jnp.zeros((DISPATCH_TILE, LOCAL_ROWS), F32)
    for k in range(TOP_K):
        w = jnp.where(lio == pos[:, k:k + 1], gate[:, k:k + 1], w)
    h2 = h1_ref[...] + _dot(w.astype(BF16), oloc_ref[slot].reshape(LOCAL_ROWS, -1).astype(BF16))

    is_p = i < n_first
    xn = _rms(h2, gple_ref[...]).astype(BF16)
    ple = _dot(_pick(is_p, pp_ref, ps_ref).astype(BF16), wpp_ref[...])
    h3 = h2 + _sigmoid(_dot(xn, wpg_ref[...])) * ple
    y = _rms(h3, gfin_ref[...])

    @pl.when(is_p)
    def _():
        yp_ref[...] = y

    @pl.when(jnp.logical_not(is_p))
    def _():
        ys_ref[...] = y


def _combine(run_len, run_loc, run_dst, os_, pos, gate, h1, pp, ps, gple, wpg, wpp, gfin):
    n, d = h1.shape
    td = DISPATCH_TILE
    n_p, n_s = pp.shape[0], ps.shape[0]
    assert n_p % td == 0 and n_s % td == 0
    n_tiles, n_first = n // td, n_p // td
    pw = pp.shape[1]
    first = lambda i, *_: (jnp.minimum(i, n_first - 1), 0)
    second = lambda i, *_: (jnp.maximum(i - n_first, 0), 0)
    row = lambda i, *_: (i, 0)
    full2 = lambda i, *_: (0, 0)
    return pl.pallas_call(
        functools.partial(_combine_kernel, n_tiles=n_tiles, n_first=n_first),
        out_shape=[jax.ShapeDtypeStruct((n_p, d), F32), jax.ShapeDtypeStruct((n_s, d), F32)],
        grid_spec=pltpu.PrefetchScalarGridSpec(
            num_scalar_prefetch=3, grid=(n_tiles,),
            in_specs=[pl.BlockSpec(memory_space=pl.ANY),
                      pl.BlockSpec((td, TOP_K), row), pl.BlockSpec((td, TOP_K), row), pl.BlockSpec((td, d), row),
                      pl.BlockSpec((td, pw), first), pl.BlockSpec((td, pw), second),
                      pl.BlockSpec(gple.shape, full2), pl.BlockSpec(wpg.shape, full2),
                      pl.BlockSpec(wpp.shape, full2), pl.BlockSpec(gfin.shape, full2)],
            out_specs=[pl.BlockSpec((td, d), first), pl.BlockSpec((td, d), second)],
            scratch_shapes=[pltpu.VMEM((2, LOCAL_GROUPS, ROW_ALIGN, d), F32), pltpu.SemaphoreType.DMA((2,))]),
        compiler_params=_cparams("arbitrary"),
        name="moe_combine_ple",
    )(run_len, run_loc, run_dst, os_, pos, gate, h1, pp, ps, gple, wpg, wpp, gfin)


def _routing_tables(cnt, total_rows):
    n_tiles = cnt.shape[0]
    run_len = (cnt + (ROW_ALIGN - 1)) // ROW_ALIGN * ROW_ALIGN
    run_loc = jnp.cumsum(run_len, axis=1) - run_len
    per_expert = jnp.sum(run_len, axis=0)
    region = (per_expert + (FFN_ROWS - 1)) // FFN_ROWS * FFN_ROWS
    region_end = jnp.cumsum(region)
    region_start = region_end - region
    run_dst = region_start[None, :] + jnp.cumsum(run_len, axis=0) - run_len
    nblk = total_rows // FFN_ROWS
    blk_row = jnp.arange(nblk, dtype=I32) * FFN_ROWS
    used = region_end[-1] // FFN_ROWS
    last = jnp.maximum(used - 1, 0)
    bexp = jnp.minimum(jnp.sum(blk_row[:, None] >= region_end[None, :], axis=1), N_EXPERTS - 1).astype(I32)
    nval = jnp.clip(per_expert[bexp] - (blk_row - region_start[bexp]), 0, FFN_ROWS)
    nval = jnp.where(blk_row < region_end[-1], nval, 0).astype(I32)
    bexp = jnp.where(blk_row < region_end[-1], bexp, bexp[last])
    flat = lambda a: (a // ROW_ALIGN).reshape(n_tiles * N_EXPERTS).astype(I32)
    return flat(run_len), flat(run_loc), flat(run_dst), bexp, nval, last.reshape(1).astype(I32)


def _chunked_t(lf, chunk):
    b, length, h = lf.shape
    return lf.reshape(b, length // chunk, chunk, h).transpose(0, 1, 3, 2)


def kernel(x_prompt, x_sample, p_prompt, p_sample, cache_k, cache_v, cache_logf, state_conv, g_mix, w_in, b_f, w_conv, w_pa, w_pb, w_o, g_ffn, w_router, b_router, w_gu, b_gu, w_dn, b_dn, g_ple, w_ple_gate, w_ple_proj, g_final):
    depth = g_mix.shape[0]
    assert depth == 1
    nb, seq, d = x_prompt.shape
    db, dseq, _ = x_sample.shape
    past = cache_k.shape[2]
    n_p, n_s = nb * seq, db * dseq

    w = w_in[0].astype(BF16)
    o_f = 3 * FOX_DIM
    o_c = o_f + HEADS
    o_g = o_c + 3 * CONV_DIM
    wf = jnp.pad(w[:, o_f:o_c], ((0, 0), (0, LANES - HEADS)))
    bf = jnp.pad(b_f[0], (0, LANES - HEADS)).reshape(1, LANES)
    mixer_w = (g_mix[0].reshape(1, d), w[:, :o_f], wf, bf, w[:, o_c:o_g], w_conv[0], w[:, o_g:], w_pb[0].astype(BF16))

    (q_p, k_p, v_p, ko_p, vo_p, logf_p, lfp_p, sa_p, mb_p), tail_p = _in_proj(
        x_prompt.reshape(n_p, d), jnp.zeros((nb, CONV_WIDTH - 1, CONV_DIM), F32), mixer_w, seq_len=seq)
    o_p = _attn_prompt(q_p, k_p, v_p, _cumsum_rows(lfp_p, seq=seq), nb=nb, seq=seq)

    (q_s, k_s, v_s, ko_s, vo_s, logf_s, _, sa_s, mb_s), tail_s = _in_proj(
        x_sample.reshape(n_s, d), state_conv[0], mixer_w, seq_len=dseq)
    chunk = min(past, 2 * TOKEN_TILE)
    lf_all = jnp.concatenate([cache_logf[0].astype(F32), logf_s.reshape(db, dseq, HEADS),
                              jnp.zeros((db, chunk - dseq, HEADS), F32)], axis=1)
    c_s = _cumsum_chunks(_chunked_t(lf_all, chunk))
    o_s = _attn_sample(q_s, cache_k[0].transpose(0, 2, 3, 1), cache_v[0].transpose(0, 2, 3, 1), k_s, v_s, c_s, nq=dseq)

    h1, xn2, pos_t, gate_t, cnt = _post_attn(
        x_prompt.reshape(n_p, d), x_sample.reshape(n_s, d), o_p, o_s, sa_p, sa_s, mb_p, mb_s,
        w_pa[0].astype(BF16), w_o[0].astype(BF16), g_ffn[0].reshape(1, d),
        w_router[0].T.astype(BF16), b_router[0].reshape(N_EXPERTS, 1))

    n = n_p + n_s
    n_tiles = n // DISPATCH_TILE
    max_rows = n * TOP_K + n_tiles * N_EXPERTS * (ROW_ALIGN - 1) + N_EXPERTS * (FFN_ROWS - 1)
    total_rows = -(-max_rows // FFN_ROWS) * FFN_ROWS
    run_len, run_loc, run_dst, bexp, nval, last = _routing_tables(cnt[:, :, 0], total_rows)
    xs = _dispatch(run_len, run_loc, run_dst, xn2, pos_t, total_rows=total_rows)
    os_ = _ffn(bexp, nval, last, xs.reshape(total_rows, d), w_gu[0], b_gu[0].reshape(N_EXPERTS, 1, 2 * D_FF),
               w_dn[0], b_dn[0].reshape(N_EXPERTS, 1, d))

    y_p, y_s = _combine(run_len, run_loc, run_dst, os_.reshape(total_rows // ROW_ALIGN, ROW_ALIGN, d), pos_t.T, gate_t.T, h1,
                        p_prompt[0].reshape(n_p, -1), p_sample[0].reshape(n_s, -1),
                        g_ple[0].reshape(1, d), w_ple_gate[0].astype(BF16), w_ple_proj[0].astype(BF16),
                        g_final.reshape(1, d))

    return (y_p.reshape(nb, seq, d), y_s.reshape(db, dseq, d),
            ko_p.transpose(0, 3, 1, 2)[None], vo_p.transpose(0, 3, 1, 2)[None],
            logf_p.reshape(1, nb, seq, HEADS), tail_p[None],
            ko_s.reshape(1, db, dseq, HEADS, HEAD_DIM), vo_s.reshape(1, db, dseq, HEADS, HEAD_DIM),
            logf_s.reshape(1, db, dseq, HEADS), tail_s[None])
```

```python
import functools

import jax
import jax.numpy as jnp
from jax import lax
from jax.experimental import pallas as pl
from jax.experimental.pallas import tpu as pltpu

F32, BF16, I32 = jnp.float32, jnp.bfloat16, jnp.int32

HEADS = 8
HEAD_DIM = 64
FOX_DIM = HEADS * HEAD_DIM
CONV_DIM = 512
CONV_WIDTH = 3
N_EXPERTS = 32
TOP_K = 4
D_FF = 1024
SWIGLU_ALPHA = 1.702
SWIGLU_LIMIT = 7.0
RMS_EPS = 1e-6
LOG2E = 1.4426950408889634

LANES = 128
SUBLANES = 8
TOKEN_TILE = 512
KEY_CHUNK = 1024
DISPATCH_TILE = 256
ROW_ALIGN = SUBLANES
LOCAL_ROWS = DISPATCH_TILE * TOP_K + N_EXPERTS * ROW_ALIGN
FFN_ROWS = 1024
FFN_SUB = 512
LOCAL_GROUPS = LOCAL_ROWS // ROW_ALIGN
RUN_CHUNKS = (32, 16, 8, 4, 2, 1)
VMEM_LIMIT_BYTES = 56 * 1024 * 1024


def _cparams(*sem):
    return pltpu.CompilerParams(dimension_semantics=sem, vmem_limit_bytes=VMEM_LIMIT_BYTES)


def _rms(x, g):
    return x * lax.rsqrt(jnp.mean(x * x, axis=-1, keepdims=True) + RMS_EPS) * g


def _sigmoid(x):
    return 1.0 / (1.0 + jnp.exp(-x))


def _log_sigmoid(x):
    return jnp.minimum(x, 0.0) - jnp.log1p(jnp.exp(-jnp.abs(x)))


def _dot(a, b):
    return jnp.dot(a, b, preferred_element_type=F32)


def _dot_nt(a, b):
    return lax.dot_general(a, b, (((1,), (1,)), ((), ())), preferred_element_type=F32)


def _mixer_inputs(x_ref, g_ref, wqkv_ref, wf_ref, bf_ref, wc_ref, wgl_ref,
                  q_ref, k_ref, v_ref, ko_ref, vo_ref, logf_ref, lfp_ref, sa_ref, *, keys_minor):
    xn = _rms(x_ref[...], g_ref[...]).astype(BF16)
    qkv = _dot(xn, wqkv_ref[...])
    q_ref[...] = (qkv[:, :FOX_DIM] * (HEAD_DIM ** -0.5 * LOG2E)).astype(BF16)
    k = qkv[:, FOX_DIM:2 * FOX_DIM]
    v = qkv[:, 2 * FOX_DIM:]
    k_ref[...] = k.astype(BF16)
    v_ref[...] = v.astype(BF16)
    if keys_minor:
        ko_ref[0] = k.T.reshape(HEADS, HEAD_DIM, k.shape[0])
        vo_ref[0] = v.T.reshape(HEADS, HEAD_DIM, v.shape[0])
    else:
        for h in range(HEADS):
            ko_ref[:, h, :] = k[:, h * HEAD_DIM:(h + 1) * HEAD_DIM]
            vo_ref[:, h, :] = v[:, h * HEAD_DIM:(h + 1) * HEAD_DIM]
    logf = _log_sigmoid(_dot(xn, wf_ref[...]) + bf_ref[...])
    logf_ref[...] = logf[:, :HEADS]
    lane = lax.broadcasted_iota(I32, logf.shape, 1)
    lfp_ref[...] = jnp.where(lane < HEADS, logf, 0.0)
    c3 = _dot(xn, wc_ref[...])
    gate_b = c3[:, :CONV_DIM]
    z = c3[:, CONV_DIM:2 * CONV_DIM] * c3[:, 2 * CONV_DIM:]
    gl = _dot(xn, wgl_ref[...])
    d = gl.shape[1] // 2
    sa_ref[...] = _sigmoid(gl[:, :d]).astype(BF16)
    return gate_b, z, _sigmoid(gl[:, d:])


def _conv_out(gate_b, z, z1, z2, sig_b, wconv_ref, wpb_ref, mb_ref):
    zc = wconv_ref[0:1, :] * z2 + wconv_ref[1:2, :] * z1 + wconv_ref[2:3, :] * z
    yb = _dot((gate_b * zc).astype(BF16), wpb_ref[...])
    mb_ref[...] = (sig_b * yb).astype(BF16)


N_MIXER_W = 8
N_MIXER_OUT = 9


def _mixer_front(x_ref, w_refs, out_refs, keys_minor):
    g_ref, wqkv_ref, wf_ref, bf_ref, wc_ref, wconv_ref, wgl_ref, wpb_ref = w_refs
    gate_b, z, sig_b = _mixer_inputs(x_ref, g_ref, wqkv_ref, wf_ref, bf_ref, wc_ref, wgl_ref, *out_refs[:-1],
                                     keys_minor=keys_minor)
    return gate_b, z, sig_b, wconv_ref, wpb_ref, out_refs[-1]


def _in_proj_seq_kernel(x_ref, prev_ref, *refs, tm):
    w_refs, out_refs = refs[:N_MIXER_W], refs[N_MIXER_W:N_MIXER_W + N_MIXER_OUT]
    tail_ref, zbuf_ref = refs[N_MIXER_W + N_MIXER_OUT:]
    gate_b, z, sig_b, wconv_ref, wpb_ref, mb_ref = _mixer_front(x_ref, w_refs, out_refs, True)
    zbuf_ref[pl.ds(SUBLANES, tm), :] = z

    @pl.when(pl.program_id(1) == 0)
    def _():
        zbuf_ref[pl.ds(SUBLANES - 2, 2), :] = prev_ref[0]

    z1 = zbuf_ref[pl.ds(SUBLANES - 1, tm), :]
    z2 = zbuf_ref[pl.ds(SUBLANES - 2, tm), :]
    _conv_out(gate_b, z, z1, z2, sig_b, wconv_ref, wpb_ref, mb_ref)
    tail = zbuf_ref[pl.ds(tm + SUBLANES - 2, 2), :]
    zbuf_ref[pl.ds(SUBLANES - 2, 2), :] = tail
    tail_ref[0] = tail


def _in_proj_multi_kernel(x_ref, ov1_ref, ov2_ref, *refs, tm, seq_len):
    w_refs, out_refs = refs[:N_MIXER_W], refs[N_MIXER_W:N_MIXER_W + N_MIXER_OUT]
    z_ref, zbuf_ref = refs[N_MIXER_W + N_MIXER_OUT:]
    gate_b, z, sig_b, wconv_ref, wpb_ref, mb_ref = _mixer_front(x_ref, w_refs, out_refs, False)
    z_ref[...] = z
    zbuf_ref[pl.ds(0, SUBLANES), :] = jnp.zeros((SUBLANES, CONV_DIM), F32)
    zbuf_ref[pl.ds(SUBLANES, tm), :] = z
    t = lax.broadcasted_iota(I32, (tm, CONV_DIM), 0) & (seq_len - 1)
    z1 = jnp.where(t == 0, ov1_ref[...], zbuf_ref[pl.ds(SUBLANES - 1, tm), :])
    z2 = jnp.where(t < 2, ov2_ref[...], zbuf_ref[pl.ds(SUBLANES - 2, tm), :])
    _conv_out(gate_b, z, z1, z2, sig_b, wconv_ref, wpb_ref, mb_ref)


def _full(shape):
    n = len(shape)
    return pl.BlockSpec(shape, lambda *_: (0,) * n)


def _in_proj(x, conv_prev, weights, *, seq_len):
    g, wqkv, wf, bf, wc, wconv, wgl, wpb = weights
    n, d = x.shape
    w_specs = [_full(w.shape) for w in (g, wqkv, wf, bf, wc, wconv, wgl, wpb)]
    shapes = (((FOX_DIM,), BF16), ((FOX_DIM,), BF16), ((FOX_DIM,), BF16), ((HEADS, HEAD_DIM), F32), ((HEADS, HEAD_DIM), F32),
              ((HEADS,), F32), ((LANES,), F32), ((d,), BF16), ((d,), BF16))
    assert len(weights) == N_MIXER_W and len(shapes) == N_MIXER_OUT
    out_shape = [jax.ShapeDtypeStruct((n,) + s, t) for s, t in shapes]
    n_common = N_MIXER_OUT
    tm = TOKEN_TILE

    def out_specs(tile_index):
        return [pl.BlockSpec((tm,) + s, lambda *a, k=len(s): (tile_index(*a),) + (0,) * k) for s, _ in shapes]

    if seq_len % tm == 0:
        nb, nj = n // seq_len, seq_len // tm
        row = lambda b, j: (b * nj + j, 0)
        specs = out_specs(lambda b, j: b * nj + j)
        for i in (3, 4):
            out_shape[i] = jax.ShapeDtypeStruct((nb, HEADS, HEAD_DIM, seq_len), F32)
            specs[i] = pl.BlockSpec((1, HEADS, HEAD_DIM, tm), lambda b, j: (b, 0, 0, j))
        outs = pl.pallas_call(
            functools.partial(_in_proj_seq_kernel, tm=tm),
            out_shape=out_shape + [jax.ShapeDtypeStruct((nb, CONV_WIDTH - 1, CONV_DIM), F32)],
            grid=(nb, nj),
            in_specs=[pl.BlockSpec((tm, d), row), pl.BlockSpec((1, CONV_WIDTH - 1, CONV_DIM), lambda b, j: (b, 0, 0))] + w_specs,
            out_specs=specs + [pl.BlockSpec((1, CONV_WIDTH - 1, CONV_DIM), lambda b, j: (b, 0, 0))],
            scratch_shapes=[pltpu.VMEM((tm + SUBLANES, CONV_DIM), F32)],
            compiler_params=_cparams("arbitrary", "arbitrary"),
            name="in_proj_seq",
        )(x, conv_prev, g, wqkv, wf, bf, wc, wconv, wgl, wpb)
        return outs[:n_common], outs[n_common]
    assert seq_len & (seq_len - 1) == 0 and seq_len >= CONV_WIDTH - 1
    assert n % tm == 0 and tm % seq_len == 0
    first = jnp.zeros((n // seq_len, seq_len, CONV_DIM), F32)
    ov1 = first.at[:, 0].set(conv_prev[:, 1]).reshape(n, CONV_DIM)
    ov2 = first.at[:, 0].set(conv_prev[:, 0]).at[:, 1].set(conv_prev[:, 1]).reshape(n, CONV_DIM)
    row = lambda i: (i, 0)
    outs = pl.pallas_call(
        functools.partial(_in_proj_multi_kernel, tm=tm, seq_len=seq_len),
        out_shape=out_shape + [jax.ShapeDtypeStruct((n, CONV_DIM), F32)],
        grid=(n // tm,),
        in_specs=[pl.BlockSpec((tm, d), row), pl.BlockSpec((tm, CONV_DIM), row), pl.BlockSpec((tm, CONV_DIM), row)] + w_specs,
        out_specs=out_specs(lambda i: i) + [pl.BlockSpec((tm, CONV_DIM), row)],
        scratch_shapes=[pltpu.VMEM((tm + SUBLANES, CONV_DIM), F32)],
        compiler_params=_cparams("arbitrary"),
        name="in_proj_multi",
    )(x, ov1, ov2, g, wqkv, wf, bf, wc, wconv, wgl, wpb)
    tail = outs[n_common].reshape(n // seq_len, seq_len, CONV_DIM)[:, seq_len - (CONV_WIDTH - 1):]
    return outs[:n_common], tail


def _cumsum_kernel(lf_ref, c_ref, *, chunk, nchunk):
    piece = min(chunk, 2 * TOKEN_TILE)
    r = lax.broadcasted_iota(I32, (piece, piece), 0)
    c = lax.broadcasted_iota(I32, (piece, piece), 1)
    upper = jnp.where(r <= c, 1.0, 0.0).astype(BF16)
    carry = jnp.zeros((HEADS, 1), F32)
    for n in range(nchunk):
        for s in range(chunk // piece):
            rest, terms = lf_ref[0, n, :, s * piece:(s + 1) * piece], []
            for _ in range(C_TERMS):
                terms.append(rest.astype(BF16))
                rest = rest - terms[-1].astype(F32)
            sums = _dot(jnp.concatenate(terms, axis=0), upper)
            cs = carry
            for j in range(C_TERMS):
                cs = cs + sums[j * HEADS:(j + 1) * HEADS]
            c_ref[0, n, :, s * piece:(s + 1) * piece] = cs
            carry = cs[:, piece - 1:piece]


def _cumsum_chunks(lf):
    b, nchunk, _, chunk = lf.shape
    spec = pl.BlockSpec((1, nchunk, HEADS, chunk), lambda i: (i, 0, 0, 0))
    return pl.pallas_call(
        functools.partial(_cumsum_kernel, chunk=chunk, nchunk=nchunk),
        out_shape=jax.ShapeDtypeStruct(lf.shape, F32),
        grid=(b,), in_specs=[spec], out_specs=spec,
        compiler_params=_cparams("arbitrary"),
        name="logf_cumsum",
    )(lf)


C_TERMS = 3


def _cumsum_rows_kernel(lf_ref, cs_ref, *, chunk, nchunk):
    r = lax.broadcasted_iota(I32, (chunk, chunk), 0)
    c = lax.broadcasted_iota(I32, (chunk, chunk), 1)
    lower = jnp.where(c <= r, 1.0, 0.0).astype(BF16)
    carry = jnp.zeros((1, LANES), F32)
    for n in range(nchunk):
        rest = lf_ref[pl.ds(n * chunk, chunk), :]
        cs = carry
        for _ in range(C_TERMS):
            term = rest.astype(BF16)
            cs = cs + _dot(lower, term)
            rest = rest - term.astype(F32)
        carry = cs[chunk - 1:chunk, :]
        packed = jnp.zeros((chunk, LANES), F32)
        rest = cs * LOG2E
        for j in range(C_TERMS):
            term = rest.astype(BF16).astype(F32)
            packed = packed + (term if j == 0 else pltpu.roll(term, j * HEADS, 1))
            rest = rest - term
        cs_ref[pl.ds(n * chunk, chunk), :] = packed.astype(BF16)


def _cumsum_rows(lfp, *, seq):
    n = lfp.shape[0]
    spec = pl.BlockSpec((seq, LANES), lambda b: (b, 0))
    return pl.pallas_call(
        functools.partial(_cumsum_rows_kernel, chunk=TOKEN_TILE, nchunk=seq // TOKEN_TILE),
        out_shape=jax.ShapeDtypeStruct((n, LANES), BF16),
        grid=(n // seq,), in_specs=[spec], out_specs=spec,
        compiler_params=_cparams("arbitrary"),
        name="logf_cumsum_rows",
    )(lfp)


SAFE_LOGIT = 64.0


def _attn_prompt_kernel(q_ref, k_ref, v_ref, cs_ref, o_ref, ka_ref, vt_ref, kn_ref, m_ref, acc_ref, *, t, tk, nk):
    assert tk == t
    hp, i = pl.program_id(1), pl.program_id(2)
    rr = lax.broadcasted_iota(I32, (LANES, LANES), 0)
    cc = lax.broadcasted_iota(I32, (LANES, LANES), 1)
    lane = lax.broadcasted_iota(I32, (1, LANES), 1)
    head_lanes = [lane < HEAD_DIM, lane >= HEAD_DIM]
    sel = [jnp.where(jnp.where(cc < HEAD_DIM, rr - cc, -1) == HEAD_DIM * h, 1.0, 0.0).astype(BF16) for h in range(2)]

    def place(h, first_lane, value):
        src = jnp.where(cc >= first_lane, (cc - first_lane) * HEADS + 2 * hp + h, -1)
        src = jnp.where(cc < first_lane + C_TERMS, src, -1)
        return jnp.where(rr == src, value, 0.0).astype(BF16)

    def ones_at(first_lane):
        return jnp.where(jnp.logical_and(lane >= first_lane, lane < first_lane + C_TERMS), 1.0, 0.0)

    def max_sq_norm(x, h):
        sq = jnp.sum(jnp.where(head_lanes[h], x * x, 0.0), axis=1, keepdims=True)
        return jnp.max(sq, axis=0, keepdims=True)

    key_lane, query_lane = HEAD_DIM, HEAD_DIM + C_TERMS

    @pl.when(i == 0)
    def _():
        kb = k_ref[...]
        cs = cs_ref[...]
        kf = kb.astype(F32)
        vt = v_ref[...].astype(F32).T
        row = lax.broadcasted_iota(I32, (SUBLANES, tk), 0)
        ones = jnp.where(row == 0, 1.0, 0.0)
        pad = jnp.zeros((LANES - HEAD_DIM - SUBLANES, tk), F32)
        for h in range(2):
            ka_ref[h] = (_dot(kb, sel[h]) + _dot(cs, place(h, key_lane, -1.0)) + ones_at(query_lane)).astype(BF16)
            kn_ref[h] = jnp.broadcast_to(max_sq_norm(kf, h), kn_ref.shape[1:])
            for n in range(nk):
                vh = vt[h * HEAD_DIM:(h + 1) * HEAD_DIM, n * tk:(n + 1) * tk]
                vt_ref[h, n] = jnp.concatenate([vh, ones, pad], axis=0).astype(BF16)

    q = q_ref[...]
    cq = cs_ref[pl.ds(pl.multiple_of(i * t, t), t), :]
    qa = [(_dot(q, sel[h]) + _dot(cq, place(h, query_lane, 1.0)) + ones_at(key_lane)).astype(BF16) for h in range(2)]
    qf = q.astype(F32)
    bound_sq = jnp.maximum(max_sq_norm(qf, 0) * kn_ref[0, 0:1, 0:1], max_sq_norm(qf, 1) * kn_ref[1, 0:1, 0:1])
    in_range = bound_sq[0, 0] < SAFE_LOGIT * SAFE_LOGIT

    def scores(j):
        start = pl.multiple_of(j * tk, tk)
        return tuple(_dot_nt(ka_ref[h, pl.ds(start, tk), :], qa[h]) for h in range(2))

    def visible(st, limit):
        ki = lax.broadcasted_iota(I32, (tk, t), 0)
        qi = lax.broadcasted_iota(I32, (tk, t), 1)
        return jnp.where(ki <= qi + limit, st, -jnp.inf)

    @pl.when(in_range)
    def _():
        st_pair = scores(i)
        for h in range(2):
            acc_ref[h] = _dot(vt_ref[h, i], jnp.exp2(visible(st_pair[h], 0)).astype(BF16))

        def add_chunks(chunks):
            st_pairs = [scores(j) for j in chunks]
            for h in range(2):
                total = acc_ref[h]
                for j, st_pair in zip(chunks, st_pairs):
                    total = total + _dot(vt_ref[h, j], jnp.exp2(st_pair[h]).astype(BF16))
                acc_ref[h] = total

        def body(n, carry):
            add_chunks([2 * n, 2 * n + 1])
            return carry

        lax.fori_loop(0, i // 2, body, 0)

        @pl.when(i % 2 == 1)
        def _():
            add_chunks([i - 1])

    @pl.when(jnp.logical_not(in_range))
    def _():
        m_ref[0] = jnp.full(m_ref.shape[1:], -jnp.inf, F32)
        acc_ref[...] = jnp.zeros(acc_ref.shape, F32)

        def limited(j, limit):
            st_pair = scores(j)
            return st_pair if limit is None else [visible(st, limit) for st in st_pair]

        def max_pass(j, limit):
            st_pair = limited(j, limit)
            for h in range(2):
                m_ref[j + 1, h] = jnp.maximum(m_ref[j, h], jnp.max(st_pair[h], axis=0, keepdims=True))

        def weight_pass(j, limit):
            st_pair = limited(j, limit)
            for h in range(2):
                m_new = m_ref[j + 1, h]
                p = jnp.exp2(st_pair[h] - m_new).astype(BF16)
                acc_ref[h] = acc_ref[h] * jnp.exp2(m_ref[j, h] - m_new) + _dot(vt_ref[h, j], p)

        max_pass(0, i * t)

        def body(j, carry):
            max_pass(j + 1, None)
            weight_pass(j, None)
            return carry

        lax.fori_loop(0, i - 1, body, 0)

        @pl.when(i >= 1)
        def _():
            max_pass(i, 0)
            weight_pass(i - 1, None)

        weight_pass(i, 0)

    halves = []
    for h in range(2):
        acc = acc_ref[h]
        halves.append(acc[:HEAD_DIM] * (1.0 / acc[HEAD_DIM:HEAD_DIM + 1]))
    o_ref[...] = jnp.concatenate(halves, axis=0).T.astype(BF16)


def _attn_prompt(q, k, v, cs, *, nb, seq):
    t = tk = min(KEY_CHUNK, seq)
    nq, nk = seq // t, seq // tk
    return pl.pallas_call(
        functools.partial(_attn_prompt_kernel, t=t, tk=tk, nk=nk),
        out_shape=jax.ShapeDtypeStruct(q.shape, BF16),
        grid=(nb, HEADS // 2, nq),
        in_specs=[pl.BlockSpec((t, LANES), lambda b, hp, i: (b * nq + i, hp)),
                  pl.BlockSpec((seq, LANES), lambda b, hp, i: (b, hp)),
                  pl.BlockSpec((seq, LANES), lambda b, hp, i: (b, hp)),
                  pl.BlockSpec((seq, LANES), lambda b, hp, i: (b, 0))],
        out_specs=pl.BlockSpec((t, LANES), lambda b, hp, i: (b * nq + i, hp)),
        scratch_shapes=[pltpu.VMEM((2, seq, LANES), BF16), pltpu.VMEM((2, nk, LANES, tk), BF16),
                        pltpu.VMEM((2, SUBLANES, LANES), F32),
                        pltpu.VMEM((nk + 1, 2, 1, t), F32), pltpu.VMEM((2, LANES, t), F32)],
        compiler_params=_cparams("arbitrary", "arbitrary", "arbitrary"),
        name="attn_prompt",
    )(q, k, v, cs)


def _attn_sample_kernel(q_ref, kc_ref, vc_ref, kn_ref, vn_ref, c_ref, o_ref,
                        qbd_ref, m_ref, l_ref, acc_ref, kpad_ref, vpad_ref, *, nq, nchunk):
    j = pl.program_id(1)
    rows = HEADS * nq
    row_head = lax.broadcasted_iota(I32, (rows, FOX_DIM), 0) >> (nq.bit_length() - 1)
    col_head = lax.broadcasted_iota(I32, (rows, FOX_DIM), 1) >> (HEAD_DIM.bit_length() - 1)
    own = row_head == col_head

    @pl.when(j == 0)
    def _():
        qt = jnp.concatenate([q_ref[...]] * HEADS, axis=0)
        qbd_ref[...] = jnp.where(own, qt, jnp.zeros_like(qt))
        m_ref[...] = jnp.full(m_ref.shape, -jnp.inf, F32)
        l_ref[...] = jnp.zeros(l_ref.shape, F32)
        acc_ref[...] = jnp.zeros(acc_ref.shape, F32)

    def update(s, cvals, visible, weighted_values):
        width = s.shape[1]
        bias = jnp.concatenate([jnp.broadcast_to(cvals[h:h + 1, :], (nq, width)) for h in range(HEADS)], axis=0)
        s = s - LOG2E * bias
        if visible is not None:
            s = jnp.where(visible, s, -jnp.inf)
        m_prev = m_ref[...]
        m_new = jnp.maximum(m_prev, jnp.max(s, axis=-1, keepdims=True))
        a = jnp.exp2(m_prev - m_new)
        p = jnp.exp2(s - m_new)
        l_ref[...] = a * l_ref[...] + jnp.sum(p, axis=-1, keepdims=True)
        m_ref[...] = m_new
        acc_ref[...] = acc_ref[...] * a + weighted_values(p.astype(BF16))

    @pl.when(j < nchunk)
    def _():
        chunk = kc_ref.shape[3]
        kt = kc_ref[0].reshape(FOX_DIM, chunk).astype(BF16)
        vt = vc_ref[0].reshape(FOX_DIM, chunk).astype(BF16)
        update(_dot(qbd_ref[...], kt), c_ref[0, j], None, lambda p: _dot_nt(p, vt))

    @pl.when(j == nchunk)
    def _():
        kpad_ref[...] = jnp.zeros(kpad_ref.shape, BF16)
        vpad_ref[...] = jnp.zeros(vpad_ref.shape, BF16)
        kpad_ref[pl.ds(0, nq), :] = kn_ref[...]
        vpad_ref[pl.ds(0, nq), :] = vn_ref[...]
        ki = lax.broadcasted_iota(I32, (rows, LANES), 1)
        qi = lax.broadcasted_iota(I32, (rows, LANES), 0) & (nq - 1)
        update(_dot_nt(qbd_ref[...], kpad_ref[...]), c_ref[0, nchunk][:, :LANES], ki <= qi,
               lambda p: _dot(p, vpad_ref[...]))
        out = jnp.where(own, acc_ref[...] * (1.0 / l_ref[...]), 0.0)
        o = out[0:nq]
        for h in range(1, HEADS):
            o = o + out[h * nq:(h + 1) * nq]
        o_ref[...] = o.astype(BF16)


def _attn_sample(q, k_cache, v_cache, k_new, v_new, c, *, nq):
    nb, past = k_cache.shape[0], k_cache.shape[3]
    chunk = c.shape[-1]
    nchunk = past // chunk
    assert nq & (nq - 1) == 0 and nq <= LANES and past % chunk == 0 and c.shape[1] == nchunk + 1
    rows = HEADS * nq
    cache_spec = pl.BlockSpec((1, HEADS, HEAD_DIM, chunk), lambda b, j: (b, 0, 0, jnp.minimum(j, nchunk - 1)))
    new_spec = pl.BlockSpec((nq, FOX_DIM), lambda b, j: (b, 0))
    return pl.pallas_call(
        functools.partial(_attn_sample_kernel, nq=nq, nchunk=nchunk),
        out_shape=jax.ShapeDtypeStruct(q.shape, BF16),
        grid=(nb, nchunk + 1),
        in_specs=[new_spec, cache_spec, cache_spec, new_spec, new_spec,
                  pl.BlockSpec((1, nchunk + 1, HEADS, chunk), lambda b, j: (b, 0, 0, 0))],
        out_specs=new_spec,
        scratch_shapes=[pltpu.VMEM((rows, FOX_DIM), BF16), pltpu.VMEM((rows, 1), F32), pltpu.VMEM((rows, 1), F32),
                        pltpu.VMEM((rows, FOX_DIM), F32), pltpu.VMEM((LANES, FOX_DIM), BF16),
                        pltpu.VMEM((LANES, FOX_DIM), BF16)],
        compiler_params=_cparams("arbitrary", "arbitrary"),
        name="attn_sample",
    )(q, k_cache, v_cache, k_new, v_new, c)


def _pick(is_first, a_ref, b_ref):
    return jnp.where(is_first, a_ref[...], b_ref[...])


def _post_attn_kernel(xp_ref, xs_ref, op_ref, os_ref, sap_ref, sas_ref, mbp_ref, mbs_ref,
                      wpa_ref, wo_ref, g_ref, wrt_ref, br_ref,
                      h1_ref, xn_ref, pos_ref, gate_ref, cnt_ref, *, n_first, tm):
    is_p = pl.program_id(0) < n_first
    ya = _dot(_pick(is_p, op_ref, os_ref), wpa_ref[...])
    merged = _pick(is_p, sap_ref, sas_ref).astype(F32) * ya + _pick(is_p, mbp_ref, mbs_ref).astype(F32)
    h1 = _pick(is_p, xp_ref, xs_ref) + _dot(merged.astype(BF16), wo_ref[...])
    h1_ref[...] = h1
    xn = _rms(h1, g_ref[...]).astype(BF16)
    xn_ref[...] = xn

    lt = _dot_nt(wrt_ref[...], xn) + br_ref[...]
    eio = lax.broadcasted_iota(I32, (N_EXPERTS, tm), 0).astype(F32)
    vals, hots = [], []
    for _ in range(TOP_K):
        m = jnp.max(lt, axis=0, keepdims=True)
        idx = jnp.min(jnp.where(lt == m, eio, float(N_EXPERTS)), axis=0, keepdims=True)
        hot = eio == idx
        vals.append(m)
        hots.append(hot)
        lt = jnp.where(hot, -jnp.inf, lt)
    ex = [jnp.exp(v - vals[0]) for v in vals]
    den = ex[0] + ex[1] + ex[2] + ex[3]
    gate_ref[...] = jnp.concatenate([e / den for e in ex], axis=0)

    chosen = jnp.zeros((N_EXPERTS, tm), F32)
    for hot in hots:
        chosen = jnp.where(hot, 1.0, chosen)
    td = DISPATCH_TILE
    r = lax.broadcasted_iota(I32, (td, td), 0)
    c = lax.broadcasted_iota(I32, (td, td), 1)
    before = jnp.where(r < c, 1.0, 0.0).astype(BF16)
    er = lax.broadcasted_iota(I32, (N_EXPERTS, N_EXPERTS), 0)
    ec = lax.broadcasted_iota(I32, (N_EXPERTS, N_EXPERTS), 1)
    lower = jnp.where(ec < er, 1.0, 0.0).astype(BF16)
    for sub in range(tm // td):
        sl = slice(sub * td, (sub + 1) * td)
        ch = chosen[:, sl]
        rank = _dot(ch.astype(BF16), before)
        cnt = rank[:, td - 1:td] + ch[:, td - 1:td]
        units = jnp.floor((cnt + (ROW_ALIGN - 1)) * (1.0 / ROW_ALIGN))
        start = ROW_ALIGN * _dot(lower, jnp.broadcast_to(units, (N_EXPERTS, td)).astype(BF16))
        base = start + rank
        pos = [jnp.sum(jnp.where(hot[:, sl], base, 0.0), axis=0, keepdims=True) for hot in hots]
        pos_ref[:, sl] = jnp.concatenate(pos, axis=0).astype(I32)
        cnt_ref[sub] = jnp.broadcast_to(cnt, (N_EXPERTS, LANES)).astype(I32)


def _post_attn(xp, xs, op, os_, sap, sas, mbp, mbs, wpa, wo, g, wrt, br):
    n_p, d = xp.shape
    n_s = xs.shape[0]
    tm = TOKEN_TILE
    assert n_p % tm == 0 and n_s % tm == 0
    n_first, n_tiles = n_p // tm, (n_p + n_s) // tm
    n = n_p + n_s
    sub = tm // DISPATCH_TILE
    first = lambda i: (jnp.minimum(i, n_first - 1), 0)
    second = lambda i: (jnp.maximum(i - n_first, 0), 0)
    row = lambda i: (i, 0)
    col = lambda i: (0, i)

    def pair(width):
        return [pl.BlockSpec((tm, width), first), pl.BlockSpec((tm, width), second)]

    return pl.pallas_call(
        functools.partial(_post_attn_kernel, n_first=n_first, tm=tm),
        out_shape=[jax.ShapeDtypeStruct((n, d), F32), jax.ShapeDtypeStruct((n, d), BF16),
                   jax.ShapeDtypeStruct((TOP_K, n), I32), jax.ShapeDtypeStruct((TOP_K, n), F32),
                   jax.ShapeDtypeStruct((n // DISPATCH_TILE, N_EXPERTS, LANES), I32)],
        grid=(n_tiles,),
        in_specs=pair(d) + pair(FOX_DIM) + pair(d) + pair(d) + [_full(w.shape) for w in (wpa, wo, g, wrt, br)],
        out_specs=[pl.BlockSpec((tm, d), row), pl.BlockSpec((tm, d), row),
                   pl.BlockSpec((TOP_K, tm), col), pl.BlockSpec((TOP_K, tm), col),
                   pl.BlockSpec((sub, N_EXPERTS, LANES), lambda i: (i, 0, 0))],
        compiler_params=_cparams("arbitrary"),
        name="post_attn_router",
    )(xp, xs, op, os_, sap, sas, mbp, mbs, wpa, wo, g, wrt, br)


def _for_each_run(tile, len_ref, loc_ref, dst_ref, fn):
    def body(e, carry):
        idx = tile * N_EXPERTS + e
        n, loc, dst = len_ref[idx], loc_ref[idx], dst_ref[idx]

        for c in RUN_CHUNKS:
            off = n & ~(2 * c - 1)

            @pl.when((n & c) != 0)
            def _():
                fn(loc + off, dst + off, c)
        return carry

    for e in range(N_EXPERTS):
        body(e, 0)


TILE_CHUNKS = tuple(1 << b for b in reversed(range(LOCAL_GROUPS.bit_length())))


def _wait_tile_rows(tile, len_ref, loc_ref, wait_rows):
    last = tile * N_EXPERTS + N_EXPERTS - 1
    total = loc_ref[last] + len_ref[last]
    for c in TILE_CHUNKS:
        @pl.when((total & c) != 0)
        def _():
            wait_rows(c)


def _dispatch_kernel(len_ref, loc_ref, dst_ref, xn_ref, pos_ref, xs_hbm, xloc_ref, sem_ref, *, n_tiles):
    i = pl.program_id(0)
    slot = lax.rem(i, 2)

    def copy(s, loc, dst, groups):
        return pltpu.make_async_copy(xloc_ref.at[s, pl.ds(loc, groups)], xs_hbm.at[pl.ds(dst, groups)], sem_ref.at[s])

    def start_tile(tile, s):
        _for_each_run(tile, len_ref, loc_ref, dst_ref, lambda loc, dst, rows: copy(s, loc, dst, rows).start())

    def wait_tile(tile, s):
        _wait_tile_rows(tile, len_ref, loc_ref, lambda rows: copy(s, 0, 0, rows).wait())

    pos = pos_ref[...]
    aio = lax.broadcasted_iota(I32, (LOCAL_ROWS, DISPATCH_TILE), 0)
    sel = jnp.zeros((LOCAL_ROWS, DISPATCH_TILE), F32)
    for k in range(TOP_K):
        sel = jnp.where(aio == pos[k:k + 1, :], 1.0, sel)
    rows_sorted = _dot(sel.astype(BF16), xn_ref[...])

    @pl.when(i >= 2)
    def _():
        wait_tile(i - 2, slot)

    xloc_ref[slot] = rows_sorted.reshape(xloc_ref.shape[1:])
    start_tile(i, slot)

    @pl.when(i == n_tiles - 1)
    def _():
        if n_tiles >= 2:
            wait_tile(i - 1, 1 - slot)
        wait_tile(i, slot)


def _dispatch(run_len, run_loc, run_dst, xn, pos, *, total_rows):
    n, d = xn.shape
    td = DISPATCH_TILE
    n_tiles = n // td
    return pl.pallas_call(
        functools.partial(_dispatch_kernel, n_tiles=n_tiles),
        out_shape=jax.ShapeDtypeStruct((total_rows // ROW_ALIGN, ROW_ALIGN, d), F32),
        grid_spec=pltpu.PrefetchScalarGridSpec(
            num_scalar_prefetch=3, grid=(n_tiles,),
            in_specs=[pl.BlockSpec((td, d), lambda i, *_: (i, 0)), pl.BlockSpec((TOP_K, td), lambda i, *_: (0, i))],
            out_specs=pl.BlockSpec(memory_space=pl.ANY),
            scratch_shapes=[pltpu.VMEM((2, LOCAL_GROUPS, ROW_ALIGN, d), F32), pltpu.SemaphoreType.DMA((2,))]),
        compiler_params=_cparams("arbitrary"),
        name="moe_dispatch",
    )(run_len, run_loc, run_dst, xn, pos)


def _ffn_kernel(bexp_ref, nval_ref, last_ref, xs_ref, wgu_ref, bgu_ref, wdn_ref, bdn_ref, o_ref, wgu_s, wdn_s):
    i = pl.program_id(0)
    nv = nval_ref[i]

    @pl.when(jnp.logical_or(i == 0, bexp_ref[i] != bexp_ref[jnp.maximum(i - 1, 0)]))
    def _():
        wgu_s[...] = wgu_ref[0].astype(BF16)
        wdn_s[...] = wdn_ref[0].astype(BF16)

    for r in range(FFN_ROWS // FFN_SUB):
        @pl.when(nv > r * FFN_SUB)
        def _():
            rows = lax.broadcasted_iota(I32, (FFN_SUB, xs_ref.shape[1]), 0) + r * FFN_SUB
            x = jnp.where(rows < nv, xs_ref[pl.ds(r * FFN_SUB, FFN_SUB), :], 0.0).astype(BF16)
            gu = _dot(x, wgu_s[...]) + bgu_ref[0]
            gt = jnp.minimum(gu[:, :D_FF], SWIGLU_LIMIT)
            up = jnp.clip(gu[:, D_FF:], -SWIGLU_LIMIT, SWIGLU_LIMIT)
            act = (up + 1.0) * gt * _sigmoid(SWIGLU_ALPHA * gt)
            o_ref[pl.ds(r * FFN_SUB, FFN_SUB), :] = _dot(act.astype(BF16), wdn_s[...]) + bdn_ref[0]


def _ffn(bexp, nval, last, xs, wgu, bgu, wdn, bdn):
    rows, d = xs.shape
    nblk = rows // FFN_ROWS
    blk = lambda i, bexp, nval, last: (jnp.minimum(i, last[0]), 0)
    exp3 = lambda i, bexp, nval, last: (bexp[i], 0, 0)
    return pl.pallas_call(
        _ffn_kernel,
        out_shape=jax.ShapeDtypeStruct((rows, d), F32),
        grid_spec=pltpu.PrefetchScalarGridSpec(
            num_scalar_prefetch=3, grid=(nblk,),
            in_specs=[pl.BlockSpec((FFN_ROWS, d), blk),
                      pl.BlockSpec((1, d, 2 * D_FF), exp3), pl.BlockSpec((1, 1, 2 * D_FF), exp3),
                      pl.BlockSpec((1, D_FF, d), exp3), pl.BlockSpec((1, 1, d), exp3)],
            out_specs=pl.BlockSpec((FFN_ROWS, d), blk),
            scratch_shapes=[pltpu.VMEM((d, 2 * D_FF), BF16), pltpu.VMEM((D_FF, d), BF16)]),
        compiler_params=_cparams("arbitrary"),
        name="moe_ffn",
    )(bexp, nval, last, xs, wgu, bgu, wdn, bdn)


def _combine_kernel(len_ref, loc_ref, dst_ref, os_hbm, pos_ref, gate_ref, h1_ref, pp_ref, ps_ref,
                    gple_ref, wpg_ref, wpp_ref, gfin_ref, yp_ref, ys_ref, oloc_ref, sem_ref, *, n_tiles, n_first):
    i = pl.program_id(0)
    slot = lax.rem(i, 2)

    def copy(s, loc, dst, groups):
        return pltpu.make_async_copy(os_hbm.at[pl.ds(dst, groups)], oloc_ref.at[s, pl.ds(loc, groups)], sem_ref.at[s])

    def start_tile(tile, s):
        _for_each_run(tile, len_ref, loc_ref, dst_ref, lambda loc, dst, rows: copy(s, loc, dst, rows).start())

    def wait_tile(tile, s):
        _wait_tile_rows(tile, len_ref, loc_ref, lambda rows: copy(s, 0, 0, rows).wait())

    @pl.when(i == 0)
    def _():
        oloc_ref[...] = jnp.zeros(oloc_ref.shape, F32)
        start_tile(0, 0)

    @pl.when(i + 1 < n_tiles)
    def _():
        start_tile(i + 1, 1 - slot)

    wait_tile(i, slot)

    pos, gate = pos_ref[...], gate_ref[...]
    lio = lax.broadcasted_iota(I32, (DISPATCH_TILE, LOCAL_ROWS), 1)
    w = jnp.zeros((DISPATCH_TILE, LOCAL_ROWS), F32)
    for k in range(TOP_K):
        w = jnp.where(lio == pos[:, k:k + 1], gate[:, k:k + 1], w)
    h2 = h1_ref[...] + _dot(w.astype(BF16), oloc_ref[slot].reshape(LOCAL_ROWS, -1).astype(BF16))

    is_p = i < n_first
    xn = _rms(h2, gple_ref[...]).astype(BF16)
    ple = _dot(_pick(is_p, pp_ref, ps_ref).astype(BF16), wpp_ref[...])
    h3 = h2 + _sigmoid(_dot(xn, wpg_ref[...])) * ple
    y = _rms(h3, gfin_ref[...])

    @pl.when(is_p)
    def _():
        yp_ref[...] = y

    @pl.when(jnp.logical_not(is_p))
    def _():
        ys_ref[...] = y


def _combine(run_len, run_loc, run_dst, os_, pos, gate, h1, pp, ps, gple, wpg, wpp, gfin):
    n, d = h1.shape
    td = DISPATCH_TILE
    n_p, n_s = pp.shape[0], ps.shape[0]
    assert n_p % td == 0 and n_s % td == 0
    n_tiles, n_first = n // td, n_p // td
    pw = pp.shape[1]
    first = lambda i, *_: (jnp.minimum(i, n_first - 1), 0)
    second = lambda i, *_: (jnp.maximum(i - n_first, 0), 0)
    row = lambda i, *_: (i, 0)
    full2 = lambda i, *_: (0, 0)
    return pl.pallas_call(
        functools.partial(_combine_kernel, n_tiles=n_tiles, n_first=n_first),
        out_shape=[jax.ShapeDtypeStruct((n_p, d), F32), jax.ShapeDtypeStruct((n_s, d), F32)],
        grid_spec=pltpu.PrefetchScalarGridSpec(
            num_scalar_prefetch=3, grid=(n_tiles,),
            in_specs=[pl.BlockSpec(memory_space=pl.ANY),
                      pl.BlockSpec((td, TOP_K), row), pl.BlockSpec((td, TOP_K), row), pl.BlockSpec((td, d), row),
                      pl.BlockSpec((td, pw), first), pl.BlockSpec((td, pw), second),
                      pl.BlockSpec(gple.shape, full2), pl.BlockSpec(wpg.shape, full2),
                      pl.BlockSpec(wpp.shape, full2), pl.BlockSpec(gfin.shape, full2)],
            out_specs=[pl.BlockSpec((td, d), first), pl.BlockSpec((td, d), second)],
            scratch_shapes=[pltpu.VMEM((2, LOCAL_GROUPS, ROW_ALIGN, d), F32), pltpu.SemaphoreType.DMA((2,))]),
        compiler_params=_cparams("arbitrary"),
        name="moe_combine_ple",
    )(run_len, run_loc, run_dst, os_, pos, gate, h1, pp, ps, gple, wpg, wpp, gfin)


def _routing_tables(cnt, total_rows):
    n_tiles = cnt.shape[0]
    run_len = (cnt + (ROW_ALIGN - 1)) // ROW_ALIGN * ROW_ALIGN
    run_loc = jnp.cumsum(run_len, axis=1) - run_len
    per_expert = jnp.sum(run_len, axis=0)
    region = (per_expert + (FFN_ROWS - 1)) // FFN_ROWS * FFN_ROWS
    region_end = jnp.cumsum(region)
    region_start = region_end - region
    run_dst = region_start[None, :] + jnp.cumsum(run_len, axis=0) - run_len
    nblk = total_rows // FFN_ROWS
    blk_row = jnp.arange(nblk, dtype=I32) * FFN_ROWS
    used = region_end[-1] // FFN_ROWS
    last = jnp.maximum(used - 1, 0)
    bexp = jnp.minimum(jnp.sum(blk_row[:, None] >= region_end[None, :], axis=1), N_EXPERTS - 1).astype(I32)
    nval = jnp.clip(per_expert[bexp] - (blk_row - region_start[bexp]), 0, FFN_ROWS)
    nval = jnp.where(blk_row < region_end[-1], nval, 0).astype(I32)
    bexp = jnp.where(blk_row < region_end[-1], bexp, bexp[last])
    flat = lambda a: (a // ROW_ALIGN).reshape(n_tiles * N_EXPERTS).astype(I32)
    return flat(run_len), flat(run_loc), flat(run_dst), bexp, nval, last.reshape(1).astype(I32)


def _chunked_t(lf, chunk):
    b, length, h = lf.shape
    return lf.reshape(b, length // chunk, chunk, h).transpose(0, 1, 3, 2)


def kernel(x_prompt, x_sample, p_prompt, p_sample, cache_k, cache_v, cache_logf, state_conv, g_mix, w_in, b_f, w_conv, w_pa, w_pb, w_o, g_ffn, w_router, b_router, w_gu, b_gu, w_dn, b_dn, g_ple, w_ple_gate, w_ple_proj, g_final):
    depth = g_mix.shape[0]
    assert depth == 1
    nb, seq, d = x_prompt.shape
    db, dseq, _ = x_sample.shape
    past = cache_k.shape[2]
    n_p, n_s = nb * seq, db * dseq

    w = w_in[0].astype(BF16)
    o_f = 3 * FOX_DIM
    o_c = o_f + HEADS
    o_g = o_c + 3 * CONV_DIM
    wf = jnp.pad(w[:, o_f:o_c], ((0, 0), (0, LANES - HEADS)))
    bf = jnp.pad(b_f[0], (0, LANES - HEADS)).reshape(1, LANES)
    mixer_w = (g_mix[0].reshape(1, d), w[:, :o_f], wf, bf, w[:, o_c:o_g], w_conv[0], w[:, o_g:], w_pb[0].astype(BF16))

    (q_p, k_p, v_p, ko_p, vo_p, logf_p, lfp_p, sa_p, mb_p), tail_p = _in_proj(
        x_prompt.reshape(n_p, d), jnp.zeros((nb, CONV_WIDTH - 1, CONV_DIM), F32), mixer_w, seq_len=seq)
    o_p = _attn_prompt(q_p, k_p, v_p, _cumsum_rows(lfp_p, seq=seq), nb=nb, seq=seq)

    (q_s, k_s, v_s, ko_s, vo_s, logf_s, _, sa_s, mb_s), tail_s = _in_proj(
        x_sample.reshape(n_s, d), state_conv[0], mixer_w, seq_len=dseq)
    chunk = min(past, 4 * TOKEN_TILE)
    lf_all = jnp.concatenate([cache_logf[0].astype(F32), logf_s.reshape(db, dseq, HEADS),
                              jnp.zeros((db, chunk - dseq, HEADS), F32)], axis=1)
    c_s = _cumsum_chunks(_chunked_t(lf_all, chunk))
    o_s = _attn_sample(q_s, cache_k[0].transpose(0, 2, 3, 1), cache_v[0].transpose(0, 2, 3, 1), k_s, v_s, c_s, nq=dseq)

    h1, xn2, pos_t, gate_t, cnt = _post_attn(
        x_prompt.reshape(n_p, d), x_sample.reshape(n_s, d), o_p, o_s, sa_p, sa_s, mb_p, mb_s,
        w_pa[0].astype(BF16), w_o[0].astype(BF16), g_ffn[0].reshape(1, d),
        w_router[0].T.astype(BF16), b_router[0].reshape(N_EXPERTS, 1))

    n = n_p + n_s
    n_tiles = n // DISPATCH_TILE
    max_rows = n * TOP_K + n_tiles * N_EXPERTS * (ROW_ALIGN - 1) + N_EXPERTS * (FFN_ROWS - 1)
    total_rows = -(-max_rows // FFN_ROWS) * FFN_ROWS
    run_len, run_loc, run_dst, bexp, nval, last = _routing_tables(cnt[:, :, 0], total_rows)
    xs = _dispatch(run_len, run_loc, run_dst, xn2, pos_t, total_rows=total_rows)
    os_ = _ffn(bexp, nval, last, xs.reshape(total_rows, d), w_gu[0], b_gu[0].reshape(N_EXPERTS, 1, 2 * D_FF),
               w_dn[0], b_dn[0].reshape(N_EXPERTS, 1, d))

    y_p, y_s = _combine(run_len, run_loc, run_dst, os_.reshape(total_rows // ROW_ALIGN, ROW_ALIGN, d), pos_t.T, gate_t.T, h1,
                        p_prompt[0].reshape(n_p, -1), p_sample[0].reshape(n_s, -1),
                        g_ple[0].reshape(1, d), w_ple_gate[0].astype(BF16), w_ple_proj[0].astype(BF16),
                        g_final.reshape(1, d))

    return (y_p.reshape(nb, seq, d), y_s.reshape(db, dseq, d),
            ko_p.transpose(0, 3, 1, 2)[None], vo_p.transpose(0, 3, 1, 2)[None],
            logf_p.reshape(1, nb, seq, HEADS), tail_p[None],
            ko_s.reshape(1, db, dseq, HEADS, HEAD_DIM), vo_s.reshape(1, db, dseq, HEADS, HEAD_DIM),
            logf_s.reshape(1, db, dseq, HEADS), tail_s[None])
```

```python
import functools

import jax
import jax.numpy as jnp
from jax import lax
from jax.experimental import pallas as pl
from jax.experimental.pallas import tpu as pltpu

F32, BF16, I32 = jnp.float32, jnp.bfloat16, jnp.int32

HEADS = 8
HEAD_DIM = 64
FOX_DIM = HEADS * HEAD_DIM
CONV_DIM = 512
CONV_WIDTH = 3
N_EXPERTS = 32
TOP_K = 4
D_FF = 1024
SWIGLU_ALPHA = 1.702
SWIGLU_LIMIT = 7.0
RMS_EPS = 1e-6
LOG2E = 1.4426950408889634

LANES = 128
SUBLANES = 8
TOKEN_TILE = 512
KEY_CHUNK = 1024
DISPATCH_TILE = 256
ROW_ALIGN = SUBLANES
LOCAL_ROWS = DISPATCH_TILE * TOP_K + N_EXPERTS * ROW_ALIGN
FFN_ROWS = 1024
FFN_SUB = 512
LOCAL_GROUPS = LOCAL_ROWS // ROW_ALIGN
RUN_CHUNKS = (32, 16, 8, 4, 2, 1)
VMEM_LIMIT_BYTES = 56 * 1024 * 1024


def _cparams(*sem):
    return pltpu.CompilerParams(dimension_semantics=sem, vmem_limit_bytes=VMEM_LIMIT_BYTES)


def _rms(x, g):
    return x * lax.rsqrt(jnp.mean(x * x, axis=-1, keepdims=True) + RMS_EPS) * g


def _sigmoid(x):
    return 1.0 / (1.0 + jnp.exp(-x))


def _log_sigmoid(x):
    return jnp.minimum(x, 0.0) - jnp.log1p(jnp.exp(-jnp.abs(x)))


def _dot(a, b):
    return jnp.dot(a, b, preferred_element_type=F32)


def _dot_nt(a, b):
    return lax.dot_general(a, b, (((1,), (1,)), ((), ())), preferred_element_type=F32)


def _mixer_inputs(x_ref, g_ref, wqkv_ref, wf_ref, bf_ref, wc_ref, wgl_ref,
                  q_ref, k_ref, v_ref, ko_ref, vo_ref, logf_ref, lfp_ref, sa_ref, *, keys_minor):
    xn = _rms(x_ref[...], g_ref[...]).astype(BF16)
    qkv = _dot(xn, wqkv_ref[...])
    q_ref[...] = (qkv[:, :FOX_DIM] * (HEAD_DIM ** -0.5 * LOG2E)).astype(BF16)
    k = qkv[:, FOX_DIM:2 * FOX_DIM]
    v = qkv[:, 2 * FOX_DIM:]
    k_ref[...] = k.astype(BF16)
    v_ref[...] = v.astype(BF16)
    if keys_minor:
        ko_ref[0] = k.T.reshape(HEADS, HEAD_DIM, k.shape[0])
        vo_ref[0] = v.T.reshape(HEADS, HEAD_DIM, v.shape[0])
    else:
        for h in range(HEADS):
            ko_ref[:, h, :] = k[:, h * HEAD_DIM:(h + 1) * HEAD_DIM]
            vo_ref[:, h, :] = v[:, h * HEAD_DIM:(h + 1) * HEAD_DIM]
    logf = _log_sigmoid(_dot(xn, wf_ref[...]) + bf_ref[...])
    logf_ref[...] = logf[:, :HEADS]
    lane = lax.broadcasted_iota(I32, logf.shape, 1)
    lfp_ref[...] = jnp.where(lane < HEADS, logf, 0.0)
    c3 = _dot(xn, wc_ref[...])
    gate_b = c3[:, :CONV_DIM]
    z = c3[:, CONV_DIM:2 * CONV_DIM] * c3[:, 2 * CONV_DIM:]
    gl = _dot(xn, wgl_ref[...])
    d = gl.shape[1] // 2
    sa_ref[...] = _sigmoid(gl[:, :d]).astype(BF16)
    return gate_b, z, _sigmoid(gl[:, d:])


def _conv_out(gate_b, z, z1, z2, sig_b, wconv_ref, wpb_ref, mb_ref):
    zc = wconv_ref[0:1, :] * z2 + wconv_ref[1:2, :] * z1 + wconv_ref[2:3, :] * z
    yb = _dot((gate_b * zc).astype(BF16), wpb_ref[...])
    mb_ref[...] = (sig_b * yb).astype(BF16)


N_MIXER_W = 8
N_MIXER_OUT = 9


def _mixer_front(x_ref, w_refs, out_refs, keys_minor):
    g_ref, wqkv_ref, wf_ref, bf_ref, wc_ref, wconv_ref, wgl_ref, wpb_ref = w_refs
    gate_b, z, sig_b = _mixer_inputs(x_ref, g_ref, wqkv_ref, wf_ref, bf_ref, wc_ref, wgl_ref, *out_refs[:-1],
                                     keys_minor=keys_minor)
    return gate_b, z, sig_b, wconv_ref, wpb_ref, out_refs[-1]


def _in_proj_seq_kernel(x_ref, prev_ref, *refs, tm):
    w_refs, out_refs = refs[:N_MIXER_W], refs[N_MIXER_W:N_MIXER_W + N_MIXER_OUT]
    tail_ref, zbuf_ref = refs[N_MIXER_W + N_MIXER_OUT:]
    gate_b, z, sig_b, wconv_ref, wpb_ref, mb_ref = _mixer_front(x_ref, w_refs, out_refs, True)
    zbuf_ref[pl.ds(SUBLANES, tm), :] = z

    @pl.when(pl.program_id(1) == 0)
    def _():
        zbuf_ref[pl.ds(SUBLANES - 2, 2), :] = prev_ref[0]

    z1 = zbuf_ref[pl.ds(SUBLANES - 1, tm), :]
    z2 = zbuf_ref[pl.ds(SUBLANES - 2, tm), :]
    _conv_out(gate_b, z, z1, z2, sig_b, wconv_ref, wpb_ref, mb_ref)
    tail = zbuf_ref[pl.ds(tm + SUBLANES - 2, 2), :]
    zbuf_ref[pl.ds(SUBLANES - 2, 2), :] = tail
    tail_ref[0] = tail


def _in_proj_multi_kernel(x_ref, ov1_ref, ov2_ref, *refs, tm, seq_len):
    w_refs, out_refs = refs[:N_MIXER_W], refs[N_MIXER_W:N_MIXER_W + N_MIXER_OUT]
    z_ref, zbuf_ref = refs[N_MIXER_W + N_MIXER_OUT:]
    gate_b, z, sig_b, wconv_ref, wpb_ref, mb_ref = _mixer_front(x_ref, w_refs, out_refs, False)
    z_ref[...] = z
    zbuf_ref[pl.ds(0, SUBLANES), :] = jnp.zeros((SUBLANES, CONV_DIM), F32)
    zbuf_ref[pl.ds(SUBLANES, tm), :] = z
    t = lax.broadcasted_iota(I32, (tm, CONV_DIM), 0) & (seq_len - 1)
    z1 = jnp.where(t == 0, ov1_ref[...], zbuf_ref[pl.ds(SUBLANES - 1, tm), :])
    z2 = jnp.where(t < 2, ov2_ref[...], zbuf_ref[pl.ds(SUBLANES - 2, tm), :])
    _conv_out(gate_b, z, z1, z2, sig_b, wconv_ref, wpb_ref, mb_ref)


def _full(shape):
    n = len(shape)
    return pl.BlockSpec(shape, lambda *_: (0,) * n)


def _in_proj(x, conv_prev, weights, *, seq_len):
    g, wqkv, wf, bf, wc, wconv, wgl, wpb = weights
    n, d = x.shape
    w_specs = [_full(w.shape) for w in (g, wqkv, wf, bf, wc, wconv, wgl, wpb)]
    shapes = (((FOX_DIM,), BF16), ((FOX_DIM,), BF16), ((FOX_DIM,), BF16), ((HEADS, HEAD_DIM), F32), ((HEADS, HEAD_DIM), F32),
              ((HEADS,), F32), ((LANES,), F32), ((d,), BF16), ((d,), BF16))
    assert len(weights) == N_MIXER_W and len(shapes) == N_MIXER_OUT
    out_shape = [jax.ShapeDtypeStruct((n,) + s, t) for s, t in shapes]
    n_common = N_MIXER_OUT
    tm = TOKEN_TILE

    def out_specs(tile_index):
        return [pl.BlockSpec((tm,) + s, lambda *a, k=len(s): (tile_index(*a),) + (0,) * k) for s, _ in shapes]

    if seq_len % tm == 0:
        nb, nj = n // seq_len, seq_len // tm
        row = lambda b, j: (b * nj + j, 0)
        specs = out_specs(lambda b, j: b * nj + j)
        for i in (3, 4):
            out_shape[i] = jax.ShapeDtypeStruct((nb, HEADS, HEAD_DIM, seq_len), F32)
            specs[i] = pl.BlockSpec((1, HEADS, HEAD_DIM, tm), lambda b, j: (b, 0, 0, j))
        outs = pl.pallas_call(
            functools.partial(_in_proj_seq_kernel, tm=tm),
            out_shape=out_shape + [jax.ShapeDtypeStruct((nb, CONV_WIDTH - 1, CONV_DIM), F32)],
            grid=(nb, nj),
            in_specs=[pl.BlockSpec((tm, d), row), pl.BlockSpec((1, CONV_WIDTH - 1, CONV_DIM), lambda b, j: (b, 0, 0))] + w_specs,
            out_specs=specs + [pl.BlockSpec((1, CONV_WIDTH - 1, CONV_DIM), lambda b, j: (b, 0, 0))],
            scratch_shapes=[pltpu.VMEM((tm + SUBLANES, CONV_DIM), F32)],
            compiler_params=_cparams("arbitrary", "arbitrary"),
            name="in_proj_seq",
        )(x, conv_prev, g, wqkv, wf, bf, wc, wconv, wgl, wpb)
        return outs[:n_common], outs[n_common]
    assert seq_len & (seq_len - 1) == 0 and seq_len >= CONV_WIDTH - 1
    assert n % tm == 0 and tm % seq_len == 0
    first = jnp.zeros((n // seq_len, seq_len, CONV_DIM), F32)
    ov1 = first.at[:, 0].set(conv_prev[:, 1]).reshape(n, CONV_DIM)
    ov2 = first.at[:, 0].set(conv_prev[:, 0]).at[:, 1].set(conv_prev[:, 1]).reshape(n, CONV_DIM)
    row = lambda i: (i, 0)
    outs = pl.pallas_call(
        functools.partial(_in_proj_multi_kernel, tm=tm, seq_len=seq_len),
        out_shape=out_shape + [jax.ShapeDtypeStruct((n, CONV_DIM), F32)],
        grid=(n // tm,),
        in_specs=[pl.BlockSpec((tm, d), row), pl.BlockSpec((tm, CONV_DIM), row), pl.BlockSpec((tm, CONV_DIM), row)] + w_specs,
        out_specs=out_specs(lambda i: i) + [pl.BlockSpec((tm, CONV_DIM), row)],
        scratch_shapes=[pltpu.VMEM((tm + SUBLANES, CONV_DIM), F32)],
        compiler_params=_cparams("arbitrary"),
        name="in_proj_multi",
    )(x, ov1, ov2, g, wqkv, wf, bf, wc, wconv, wgl, wpb)
    tail = outs[n_common].reshape(n // seq_len, seq_len, CONV_DIM)[:, seq_len - (CONV_WIDTH - 1):]
    return outs[:n_common], tail


def _cumsum_kernel(lf_ref, c_ref, *, chunk, nchunk):
    piece = min(chunk, 2 * TOKEN_TILE)
    r = lax.broadcasted_iota(I32, (piece, piece), 0)
    c = lax.broadcasted_iota(I32, (piece, piece), 1)
    upper = jnp.where(r <= c, 1.0, 0.0).astype(BF16)
    carry = jnp.zeros((HEADS, 1), F32)
    for n in range(nchunk):
        for s in range(chunk // piece):
            rest, terms = lf_ref[0, n, :, s * piece:(s + 1) * piece], []
            for _ in range(C_TERMS):
                terms.append(rest.astype(BF16))
                rest = rest - terms[-1].astype(F32)
            sums = _dot(jnp.concatenate(terms, axis=0), upper)
            cs = carry
            for j in range(C_TERMS):
                cs = cs + sums[j * HEADS:(j + 1) * HEADS]
            c_ref[0, n, :, s * piece:(s + 1) * piece] = cs
            carry = cs[:, piece - 1:piece]


def _cumsum_chunks(lf):
    b, nchunk, _, chunk = lf.shape
    spec = pl.BlockSpec((1, nchunk, HEADS, chunk), lambda i: (i, 0, 0, 0))
    return pl.pallas_call(
        functools.partial(_cumsum_kernel, chunk=chunk, nchunk=nchunk),
        out_shape=jax.ShapeDtypeStruct(lf.shape, F32),
        grid=(b,), in_specs=[spec], out_specs=spec,
        compiler_params=_cparams("arbitrary"),
        name="logf_cumsum",
    )(lf)


C_TERMS = 3


def _cumsum_rows_kernel(lf_ref, cs_ref, *, chunk, nchunk):
    r = lax.broadcasted_iota(I32, (chunk, chunk), 0)
    c = lax.broadcasted_iota(I32, (chunk, chunk), 1)
    lower = jnp.where(c <= r, 1.0, 0.0).astype(BF16)
    carry = jnp.zeros((1, LANES), F32)
    for n in range(nchunk):
        rest = lf_ref[pl.ds(n * chunk, chunk), :]
        cs = carry
        for _ in range(C_TERMS):
            term = rest.astype(BF16)
            cs = cs + _dot(lower, term)
            rest = rest - term.astype(F32)
        carry = cs[chunk - 1:chunk, :]
        packed = jnp.zeros((chunk, LANES), F32)
        rest = cs * LOG2E
        for j in range(C_TERMS):
            term = rest.astype(BF16).astype(F32)
            packed = packed + (term if j == 0 else pltpu.roll(term, j * HEADS, 1))
            rest = rest - term
        cs_ref[pl.ds(n * chunk, chunk), :] = packed.astype(BF16)


def _cumsum_rows(lfp, *, seq):
    n = lfp.shape[0]
    spec = pl.BlockSpec((seq, LANES), lambda b: (b, 0))
    return pl.pallas_call(
        functools.partial(_cumsum_rows_kernel, chunk=TOKEN_TILE, nchunk=seq // TOKEN_TILE),
        out_shape=jax.ShapeDtypeStruct((n, LANES), BF16),
        grid=(n // seq,), in_specs=[spec], out_specs=spec,
        compiler_params=_cparams("arbitrary"),
        name="logf_cumsum_rows",
    )(lfp)


SAFE_LOGIT = 64.0


def _attn_prompt_kernel(q_ref, k_ref, v_ref, cs_ref, o_ref, ka_ref, vt_ref, kn_ref, m_ref, acc_ref, *, t, tk, nk):
    assert tk == t
    hp, i = pl.program_id(1), pl.program_id(2)
    rr = lax.broadcasted_iota(I32, (LANES, LANES), 0)
    cc = lax.broadcasted_iota(I32, (LANES, LANES), 1)
    lane = lax.broadcasted_iota(I32, (1, LANES), 1)
    head_lanes = [lane < HEAD_DIM, lane >= HEAD_DIM]
    sel = [jnp.where(jnp.where(cc < HEAD_DIM, rr - cc, -1) == HEAD_DIM * h, 1.0, 0.0).astype(BF16) for h in range(2)]

    def place(h, first_lane, value):
        src = jnp.where(cc >= first_lane, (cc - first_lane) * HEADS + 2 * hp + h, -1)
        src = jnp.where(cc < first_lane + C_TERMS, src, -1)
        return jnp.where(rr == src, value, 0.0).astype(BF16)

    def ones_at(first_lane):
        return jnp.where(jnp.logical_and(lane >= first_lane, lane < first_lane + C_TERMS), 1.0, 0.0)

    def max_sq_norm(x, h):
        sq = jnp.sum(jnp.where(head_lanes[h], x * x, 0.0), axis=1, keepdims=True)
        return jnp.max(sq, axis=0, keepdims=True)

    key_lane, query_lane = HEAD_DIM, HEAD_DIM + C_TERMS

    @pl.when(i == 0)
    def _():
        kb = k_ref[...]
        cs = cs_ref[...]
        kf = kb.astype(F32)
        vt = v_ref[...].astype(F32).T
        row = lax.broadcasted_iota(I32, (SUBLANES, tk), 0)
        ones = jnp.where(row == 0, 1.0, 0.0)
        pad = jnp.zeros((LANES - HEAD_DIM - SUBLANES, tk), F32)
        for h in range(2):
            ka_ref[h] = (_dot(kb, sel[h]) + _dot(cs, place(h, key_lane, -1.0)) + ones_at(query_lane)).astype(BF16)
            kn_ref[h] = jnp.broadcast_to(max_sq_norm(kf, h), kn_ref.shape[1:])
            for n in range(nk):
                vh = vt[h * HEAD_DIM:(h + 1) * HEAD_DIM, n * tk:(n + 1) * tk]
                vt_ref[h, n] = jnp.concatenate([vh, ones, pad], axis=0).astype(BF16)

    q = q_ref[...]
    cq = cs_ref[pl.ds(pl.multiple_of(i * t, t), t), :]
    qa = [(_dot(q, sel[h]) + _dot(cq, place(h, query_lane, 1.0)) + ones_at(key_lane)).astype(BF16) for h in range(2)]
    qf = q.astype(F32)
    bound_sq = jnp.maximum(max_sq_norm(qf, 0) * kn_ref[0, 0:1, 0:1], max_sq_norm(qf, 1) * kn_ref[1, 0:1, 0:1])
    in_range = bound_sq[0, 0] < SAFE_LOGIT * SAFE_LOGIT

    def scores(j):
        start = pl.multiple_of(j * tk, tk)
        return tuple(_dot_nt(ka_ref[h, pl.ds(start, tk), :], qa[h]) for h in range(2))

    def visible(st, limit):
        ki = lax.broadcasted_iota(I32, (tk, t), 0)
        qi = lax.broadcasted_iota(I32, (tk, t), 1)
        return jnp.where(ki <= qi + limit, st, -jnp.inf)

    @pl.when(in_range)
    def _():
        st_pair = scores(i)
        for h in range(2):
            acc_ref[h] = _dot(vt_ref[h, i], jnp.exp2(visible(st_pair[h], 0)).astype(BF16))

        def add_chunks(chunks):
            st_pairs = [scores(j) for j in chunks]
            for h in range(2):
                total = acc_ref[h]
                for j, st_pair in zip(chunks, st_pairs):
                    total = total + _dot(vt_ref[h, j], jnp.exp2(st_pair[h]).astype(BF16))
                acc_ref[h] = total

        def body(n, carry):
            add_chunks([2 * n, 2 * n + 1])
            return carry

        lax.fori_loop(0, i // 2, body, 0)

        @pl.when(i % 2 == 1)
        def _():
            add_chunks([i - 1])

    @pl.when(jnp.logical_not(in_range))
    def _():
        m_ref[0] = jnp.full(m_ref.shape[1:], -jnp.inf, F32)
        acc_ref[...] = jnp.zeros(acc_ref.shape, F32)

        def limited(j, limit):
            st_pair = scores(j)
            return st_pair if limit is None else [visible(st, limit) for st in st_pair]

        def max_pass(j, limit):
            st_pair = limited(j, limit)
            for h in range(2):
                m_ref[j + 1, h] = jnp.maximum(m_ref[j, h], jnp.max(st_pair[h], axis=0, keepdims=True))

        def weight_pass(j, limit):
            st_pair = limited(j, limit)
            for h in range(2):
                m_new = m_ref[j + 1, h]
                p = jnp.exp2(st_pair[h] - m_new).astype(BF16)
                acc_ref[h] = acc_ref[h] * jnp.exp2(m_ref[j, h] - m_new) + _dot(vt_ref[h, j], p)

        max_pass(0, i * t)

        def body(j, carry):
            max_pass(j + 1, None)
            weight_pass(j, None)
            return carry

        lax.fori_loop(0, i - 1, body, 0)

        @pl.when(i >= 1)
        def _():
            max_pass(i, 0)
            weight_pass(i - 1, None)

        weight_pass(i, 0)

    halves = []
    for h in range(2):
        acc = acc_ref[h]
        halves.append(acc[:HEAD_DIM] * (1.0 / acc[HEAD_DIM:HEAD_DIM + 1]))
    o_ref[...] = jnp.concatenate(halves, axis=0).T.astype(BF16)


def _attn_prompt(q, k, v, cs, *, nb, seq):
    t = tk = min(KEY_CHUNK, seq)
    nq, nk = seq // t, seq // tk
    return pl.pallas_call(
        functools.partial(_attn_prompt_kernel, t=t, tk=tk, nk=nk),
        out_shape=jax.ShapeDtypeStruct(q.shape, BF16),
        grid=(nb, HEADS // 2, nq),
        in_specs=[pl.BlockSpec((t, LANES), lambda b, hp, i: (b * nq + i, hp)),
                  pl.BlockSpec((seq, LANES), lambda b, hp, i: (b, hp)),
                  pl.BlockSpec((seq, LANES), lambda b, hp, i: (b, hp)),
                  pl.BlockSpec((seq, LANES), lambda b, hp, i: (b, 0))],
        out_specs=pl.BlockSpec((t, LANES), lambda b, hp, i: (b * nq + i, hp)),
        scratch_shapes=[pltpu.VMEM((2, seq, LANES), BF16), pltpu.VMEM((2, nk, LANES, tk), BF16),
                        pltpu.VMEM((2, SUBLANES, LANES), F32),
                        pltpu.VMEM((nk + 1, 2, 1, t), F32), pltpu.VMEM((2, LANES, t), F32)],
        compiler_params=_cparams("arbitrary", "arbitrary", "arbitrary"),
        name="attn_prompt",
    )(q, k, v, cs)


def _attn_sample_kernel(q_ref, kc_ref, vc_ref, kn_ref, vn_ref, c_ref, o_ref,
                        qbd_ref, m_ref, l_ref, acc_ref, kpad_ref, vpad_ref, *, nq, nchunk):
    j = pl.program_id(1)
    rows = HEADS * nq
    row_head = lax.broadcasted_iota(I32, (rows, FOX_DIM), 0) >> (nq.bit_length() - 1)
    col_head = lax.broadcasted_iota(I32, (rows, FOX_DIM), 1) >> (HEAD_DIM.bit_length() - 1)
    own = row_head == col_head

    @pl.when(j == 0)
    def _():
        qt = jnp.concatenate([q_ref[...]] * HEADS, axis=0)
        qbd_ref[...] = jnp.where(own, qt, jnp.zeros_like(qt))
        m_ref[...] = jnp.full(m_ref.shape, -jnp.inf, F32)
        l_ref[...] = jnp.zeros(l_ref.shape, F32)
        acc_ref[...] = jnp.zeros(acc_ref.shape, F32)

    def update(s, cvals, visible, weighted_values):
        width = s.shape[1]
        bias = jnp.concatenate([jnp.broadcast_to(cvals[h:h + 1, :], (nq, width)) for h in range(HEADS)], axis=0)
        s = s - LOG2E * bias
        if visible is not None:
            s = jnp.where(visible, s, -jnp.inf)
        m_prev = m_ref[...]
        m_new = jnp.maximum(m_prev, jnp.max(s, axis=-1, keepdims=True))
        a = jnp.exp2(m_prev - m_new)
        p = jnp.exp2(s - m_new)
        l_ref[...] = a * l_ref[...] + jnp.sum(p, axis=-1, keepdims=True)
        m_ref[...] = m_new
        acc_ref[...] = acc_ref[...] * a + weighted_values(p.astype(BF16))

    @pl.when(j < nchunk)
    def _():
        chunk = kc_ref.shape[3]
        kt = kc_ref[0].reshape(FOX_DIM, chunk).astype(BF16)
        vt = vc_ref[0].reshape(FOX_DIM, chunk).astype(BF16)
        update(_dot(qbd_ref[...], kt), c_ref[0, j], None, lambda p: _dot_nt(p, vt))

    @pl.when(j == nchunk)
    def _():
        kpad_ref[...] = jnp.zeros(kpad_ref.shape, BF16)
        vpad_ref[...] = jnp.zeros(vpad_ref.shape, BF16)
        kpad_ref[pl.ds(0, nq), :] = kn_ref[...]
        vpad_ref[pl.ds(0, nq), :] = vn_ref[...]
        ki = lax.broadcasted_iota(I32, (rows, LANES), 1)
        qi = lax.broadcasted_iota(I32, (rows, LANES), 0) & (nq - 1)
        update(_dot_nt(qbd_ref[...], kpad_ref[...]), c_ref[0, nchunk][:, :LANES], ki <= qi,
               lambda p: _dot(p, vpad_ref[...]))
        out = jnp.where(own, acc_ref[...] * (1.0 / l_ref[...]), 0.0)
        o = out[0:nq]
        for h in range(1, HEADS):
            o = o + out[h * nq:(h + 1) * nq]
        o_ref[...] = o.astype(BF16)


def _attn_sample(q, k_cache, v_cache, k_new, v_new, c, *, nq):
    nb, past = k_cache.shape[0], k_cache.shape[3]
    chunk = c.shape[-1]
    nchunk = past // chunk
    assert nq & (nq - 1) == 0 and nq <= LANES and past % chunk == 0 and c.shape[1] == nchunk + 1
    rows = HEADS * nq
    cache_spec = pl.BlockSpec((1, HEADS, HEAD_DIM, chunk), lambda b, j: (b, 0, 0, jnp.minimum(j, nchunk - 1)))
    new_spec = pl.BlockSpec((nq, FOX_DIM), lambda b, j: (b, 0))
    return pl.pallas_call(
        functools.partial(_attn_sample_kernel, nq=nq, nchunk=nchunk),
        out_shape=jax.ShapeDtypeStruct(q.shape, BF16),
        grid=(nb, nchunk + 1),
        in_specs=[new_spec, cache_spec, cache_spec, new_spec, new_spec,
                  pl.BlockSpec((1, nchunk + 1, HEADS, chunk), lambda b, j: (b, 0, 0, 0))],
        out_specs=new_spec,
        scratch_shapes=[pltpu.VMEM((rows, FOX_DIM), BF16), pltpu.VMEM((rows, 1), F32), pltpu.VMEM((rows, 1), F32),
                        pltpu.VMEM((rows, FOX_DIM), F32), pltpu.VMEM((LANES, FOX_DIM), BF16),
                        pltpu.VMEM((LANES, FOX_DIM), BF16)],
        compiler_params=_cparams("arbitrary", "arbitrary"),
        name="attn_sample",
    )(q, k_cache, v_cache, k_new, v_new, c)


def _pick(is_first, a_ref, b_ref):
    return jnp.where(is_first, a_ref[...], b_ref[...])


def _post_attn_kernel(xp_ref, xs_ref, op_ref, os_ref, sap_ref, sas_ref, mbp_ref, mbs_ref,
                      wpa_ref, wo_ref, g_ref, wrt_ref, br_ref,
                      h1_ref, xn_ref, pos_ref, gate_ref, cnt_ref, *, n_first, tm):
    is_p = pl.program_id(0) < n_first
    ya = _dot(_pick(is_p, op_ref, os_ref), wpa_ref[...])
    merged = _pick(is_p, sap_ref, sas_ref).astype(F32) * ya + _pick(is_p, mbp_ref, mbs_ref).astype(F32)
    h1 = _pick(is_p, xp_ref, xs_ref) + _dot(merged.astype(BF16), wo_ref[...])
    h1_ref[...] = h1
    xn = _rms(h1, g_ref[...]).astype(BF16)
    xn_ref[...] = xn

    lt = _dot_nt(wrt_ref[...], xn) + br_ref[...]
    eio = lax.broadcasted_iota(I32, (N_EXPERTS, tm), 0).astype(F32)
    vals, hots = [], []
    for _ in range(TOP_K):
        m = jnp.max(lt, axis=0, keepdims=True)
        idx = jnp.min(jnp.where(lt == m, eio, float(N_EXPERTS)), axis=0, keepdims=True)
        hot = eio == idx
        vals.append(m)
        hots.append(hot)
        lt = jnp.where(hot, -jnp.inf, lt)
    ex = [jnp.exp(v - vals[0]) for v in vals]
    den = ex[0] + ex[1] + ex[2] + ex[3]
    gate_ref[...] = jnp.concatenate([e / den for e in ex], axis=0)

    chosen = jnp.zeros((N_EXPERTS, tm), F32)
    for hot in hots:
        chosen = jnp.where(hot, 1.0, chosen)
    td = DISPATCH_TILE
    r = lax.broadcasted_iota(I32, (td, td), 0)
    c = lax.broadcasted_iota(I32, (td, td), 1)
    before = jnp.where(r < c, 1.0, 0.0).astype(BF16)
    er = lax.broadcasted_iota(I32, (N_EXPERTS, N_EXPERTS), 0)
    ec = lax.broadcasted_iota(I32, (N_EXPERTS, N_EXPERTS), 1)
    lower = jnp.where(ec < er, 1.0, 0.0).astype(BF16)
    for sub in range(tm // td):
        sl = slice(sub * td, (sub + 1) * td)
        ch = chosen[:, sl]
        rank = _dot(ch.astype(BF16), before)
        cnt = rank[:, td - 1:td] + ch[:, td - 1:td]
        units = jnp.floor((cnt + (ROW_ALIGN - 1)) * (1.0 / ROW_ALIGN))
        start = ROW_ALIGN * _dot(lower, jnp.broadcast_to(units, (N_EXPERTS, td)).astype(BF16))
        base = start + rank
        pos = [jnp.sum(jnp.where(hot[:, sl], base, 0.0), axis=0, keepdims=True) for hot in hots]
        pos_ref[:, sl] = jnp.concatenate(pos, axis=0).astype(I32)
        cnt_ref[sub] = jnp.broadcast_to(cnt, (N_EXPERTS, LANES)).astype(I32)


def _post_attn(xp, xs, op, os_, sap, sas, mbp, mbs, wpa, wo, g, wrt, br):
    n_p, d = xp.shape
    n_s = xs.shape[0]
    tm = TOKEN_TILE
    assert n_p % tm == 0 and n_s % tm == 0
    n_first, n_tiles = n_p // tm, (n_p + n_s) // tm
    n = n_p + n_s
    sub = tm // DISPATCH_TILE
    first = lambda i: (jnp.minimum(i, n_first - 1), 0)
    second = lambda i: (jnp.maximum(i - n_first, 0), 0)
    row = lambda i: (i, 0)
    col = lambda i: (0, i)

    def pair(width):
        return [pl.BlockSpec((tm, width), first), pl.BlockSpec((tm, width), second)]

    return pl.pallas_call(
        functools.partial(_post_attn_kernel, n_first=n_first, tm=tm),
        out_shape=[jax.ShapeDtypeStruct((n, d), F32), jax.ShapeDtypeStruct((n, d), BF16),
                   jax.ShapeDtypeStruct((TOP_K, n), I32), jax.ShapeDtypeStruct((TOP_K, n), F32),
                   jax.ShapeDtypeStruct((n // DISPATCH_TILE, N_EXPERTS, LANES), I32)],
        grid=(n_tiles,),
        in_specs=pair(d) + pair(FOX_DIM) + pair(d) + pair(d) + [_full(w.shape) for w in (wpa, wo, g, wrt, br)],
        out_specs=[pl.BlockSpec((tm, d), row), pl.BlockSpec((tm, d), row),
                   pl.BlockSpec((TOP_K, tm), col), pl.BlockSpec((TOP_K, tm), col),
                   pl.BlockSpec((sub, N_EXPERTS, LANES), lambda i: (i, 0, 0))],
        compiler_params=_cparams("arbitrary"),
        name="post_attn_router",
    )(xp, xs, op, os_, sap, sas, mbp, mbs, wpa, wo, g, wrt, br)


def _for_each_run(tile, len_ref, loc_ref, dst_ref, fn):
    def body(e, carry):
        idx = tile * N_EXPERTS + e
        n, loc, dst = len_ref[idx], loc_ref[idx], dst_ref[idx]

        for c in RUN_CHUNKS:
            off = n & ~(2 * c - 1)

            @pl.when((n & c) != 0)
            def _():
                fn(loc + off, dst + off, c, (e + RUN_CHUNKS.index(c)) % 2)
        return carry

    for e in range(N_EXPERTS):
        body(e, 0)


TILE_CHUNKS = tuple(1 << b for b in reversed(range(LOCAL_GROUPS.bit_length())))


def _wait_tile_rows(tile, len_ref, loc_ref, wait_rows):
    last = tile * N_EXPERTS + N_EXPERTS - 1
    total = loc_ref[last] + len_ref[last]
    for c in TILE_CHUNKS:
        @pl.when((total & c) != 0)
        def _():
            wait_rows(c)


def _dispatch_kernel(len_ref, loc_ref, dst_ref, xn_ref, pos_ref, xs_hbm, xloc_ref, sem_ref, *, n_tiles):
    i = pl.program_id(0)
    slot = lax.rem(i, 2)

    def copy(s, loc, dst, groups):
        return pltpu.make_async_copy(xloc_ref.at[s, pl.ds(loc, groups)], xs_hbm.at[pl.ds(dst, groups)], sem_ref.at[s])

    def start_tile(tile, s):
        _for_each_run(tile, len_ref, loc_ref, dst_ref, lambda loc, dst, rows, prio: copy(s, loc, dst, rows).start(priority=prio))

    def wait_tile(tile, s):
        _wait_tile_rows(tile, len_ref, loc_ref, lambda rows: copy(s, 0, 0, rows).wait())

    pos = pos_ref[...]
    aio = lax.broadcasted_iota(I32, (LOCAL_ROWS, DISPATCH_TILE), 0)
    sel = jnp.zeros((LOCAL_ROWS, DISPATCH_TILE), F32)
    for k in range(TOP_K):
        sel = jnp.where(aio == pos[k:k + 1, :], 1.0, sel)
    rows_sorted = _dot(sel.astype(BF16), xn_ref[...])

    @pl.when(i >= 2)
    def _():
        wait_tile(i - 2, slot)

    xloc_ref[slot] = rows_sorted.reshape(xloc_ref.shape[1:])
    start_tile(i, slot)

    @pl.when(i == n_tiles - 1)
    def _():
        if n_tiles >= 2:
            wait_tile(i - 1, 1 - slot)
        wait_tile(i, slot)


def _dispatch(run_len, run_loc, run_dst, xn, pos, *, total_rows):
    n, d = xn.shape
    td = DISPATCH_TILE
    n_tiles = n // td
    return pl.pallas_call(
        functools.partial(_dispatch_kernel, n_tiles=n_tiles),
        out_shape=jax.ShapeDtypeStruct((total_rows // ROW_ALIGN, ROW_ALIGN, d), F32),
        grid_spec=pltpu.PrefetchScalarGridSpec(
            num_scalar_prefetch=3, grid=(n_tiles,),
            in_specs=[pl.BlockSpec((td, d), lambda i, *_: (i, 0)), pl.BlockSpec((TOP_K, td), lambda i, *_: (0, i))],
            out_specs=pl.BlockSpec(memory_space=pl.ANY),
            scratch_shapes=[pltpu.VMEM((2, LOCAL_GROUPS, ROW_ALIGN, d), F32), pltpu.SemaphoreType.DMA((2,))]),
        compiler_params=_cparams("arbitrary"),
        name="moe_dispatch",
    )(run_len, run_loc, run_dst, xn, pos)


def _ffn_kernel(bexp_ref, nval_ref, last_ref, xs_ref, wgu_ref, bgu_ref, wdn_ref, bdn_ref, o_ref, wgu_s, wdn_s):
    i = pl.program_id(0)
    nv = nval_ref[i]

    @pl.when(jnp.logical_or(i == 0, bexp_ref[i] != bexp_ref[jnp.maximum(i - 1, 0)]))
    def _():
        wgu_s[...] = wgu_ref[0].astype(BF16)
        wdn_s[...] = wdn_ref[0].astype(BF16)

    for r in range(FFN_ROWS // FFN_SUB):
        @pl.when(nv > r * FFN_SUB)
        def _():
            rows = lax.broadcasted_iota(I32, (FFN_SUB, xs_ref.shape[1]), 0) + r * FFN_SUB
            x = jnp.where(rows < nv, xs_ref[pl.ds(r * FFN_SUB, FFN_SUB), :], 0.0).astype(BF16)
            gu = _dot(x, wgu_s[...]) + bgu_ref[0]
            gt = jnp.minimum(gu[:, :D_FF], SWIGLU_LIMIT)
            up = jnp.clip(gu[:, D_FF:], -SWIGLU_LIMIT, SWIGLU_LIMIT)
            act = (up + 1.0) * gt * _sigmoid(SWIGLU_ALPHA * gt)
            o_ref[pl.ds(r * FFN_SUB, FFN_SUB), :] = _dot(act.astype(BF16), wdn_s[...]) + bdn_ref[0]


def _ffn(bexp, nval, last, xs, wgu, bgu, wdn, bdn):
    rows, d = xs.shape
    nblk = rows // FFN_ROWS
    blk = lambda i, bexp, nval, last: (jnp.minimum(i, last[0]), 0)
    exp3 = lambda i, bexp, nval, last: (bexp[i], 0, 0)
    return pl.pallas_call(
        _ffn_kernel,
        out_shape=jax.ShapeDtypeStruct((rows, d), F32),
        grid_spec=pltpu.PrefetchScalarGridSpec(
            num_scalar_prefetch=3, grid=(nblk,),
            in_specs=[pl.BlockSpec((FFN_ROWS, d), blk),
                      pl.BlockSpec((1, d, 2 * D_FF), exp3), pl.BlockSpec((1, 1, 2 * D_FF), exp3),
                      pl.BlockSpec((1, D_FF, d), exp3), pl.BlockSpec((1, 1, d), exp3)],
            out_specs=pl.BlockSpec((FFN_ROWS, d), blk),
            scratch_shapes=[pltpu.VMEM((d, 2 * D_FF), BF16), pltpu.VMEM((D_FF, d), BF16)]),
        compiler_params=_cparams("arbitrary"),
        name="moe_ffn",
    )(bexp, nval, last, xs, wgu, bgu, wdn, bdn)


def _combine_kernel(len_ref, loc_ref, dst_ref, os_hbm, pos_ref, gate_ref, h1_ref, pp_ref, ps_ref,
                    gple_ref, wpg_ref, wpp_ref, gfin_ref, yp_ref, ys_ref, oloc_ref, sem_ref, *, n_tiles, n_first):
    i = pl.program_id(0)
    slot = lax.rem(i, 2)

    def copy(s, loc, dst, groups):
        return pltpu.make_async_copy(os_hbm.at[pl.ds(dst, groups)], oloc_ref.at[s, pl.ds(loc, groups)], sem_ref.at[s])

    def start_tile(tile, s):
        _for_each_run(tile, len_ref, loc_ref, dst_ref, lambda loc, dst, rows, prio: copy(s, loc, dst, rows).start(priority=prio))

    def wait_tile(tile, s):
        _wait_tile_rows(tile, len_ref, loc_ref, lambda rows: copy(s, 0, 0, rows).wait())

    @pl.when(i == 0)
    def _():
        oloc_ref[...] = jnp.zeros(oloc_ref.shape, F32)
        start_tile(0, 0)

    @pl.when(i + 1 < n_tiles)
    def _():
        start_tile(i + 1, 1 - slot)

    wait_tile(i, slot)

    pos, gate = pos_ref[...], gate_ref[...]
    lio = lax.broadcasted_iota(I32, (DISPATCH_TILE, LOCAL_ROWS), 1)
    w = jnp.zeros((DISPATCH_TILE, LOCAL_ROWS), F32)
    for k in range(TOP_K):
        w = jnp.where(lio == pos[:, k:k + 1], gate[:, k:k + 1], w)
    h2 = h1_ref[...] + _dot(w.astype(BF16), oloc_ref[slot].reshape(LOCAL_ROWS, -1).astype(BF16))

    is_p = i < n_first
    xn = _rms(h2, gple_ref[...]).astype(BF16)
    ple = _dot(_pick(is_p, pp_ref, ps_ref).astype(BF16), wpp_ref[...])
    h3 = h2 + _sigmoid(_dot(xn, wpg_ref[...])) * ple
    y = _rms(h3, gfin_ref[...])

    @pl.when(is_p)
    def _():
        yp_ref[...] = y

    @pl.when(jnp.logical_not(is_p))
    def _():
        ys_ref[...] = y


def _combine(run_len, run_loc, run_dst, os_, pos, gate, h1, pp, ps, gple, wpg, wpp, gfin):
    n, d = h1.shape
    td = DISPATCH_TILE
    n_p, n_s = pp.shape[0], ps.shape[0]
    assert n_p % td == 0 and n_s % td == 0
    n_tiles, n_first = n // td, n_p // td
    pw = pp.shape[1]
    first = lambda i, *_: (jnp.minimum(i, n_first - 1), 0)
    second = lambda i, *_: (jnp.maximum(i - n_first, 0), 0)
    row = lambda i, *_: (i, 0)
    full2 = lambda i, *_: (0, 0)
    return pl.pallas_call(
        functools.partial(_combine_kernel, n_tiles=n_tiles, n_first=n_first),
        out_shape=[jax.ShapeDtypeStruct((n_p, d), F32), jax.ShapeDtypeStruct((n_s, d), F32)],
        grid_spec=pltpu.PrefetchScalarGridSpec(
            num_scalar_prefetch=3, grid=(n_tiles,),
            in_specs=[pl.BlockSpec(memory_space=pl.ANY),
                      pl.BlockSpec((td, TOP_K), row), pl.BlockSpec((td, TOP_K), row), pl.BlockSpec((td, d), row),
                      pl.BlockSpec((td, pw), first), pl.BlockSpec((td, pw), second),
                      pl.BlockSpec(gple.shape, full2), pl.BlockSpec(wpg.shape, full2),
                      pl.BlockSpec(wpp.shape, full2), pl.BlockSpec(gfin.shape, full2)],
            out_specs=[pl.BlockSpec((td, d), first), pl.BlockSpec((td, d), second)],
            scratch_shapes=[pltpu.VMEM((2, LOCAL_GROUPS, ROW_ALIGN, d), F32), pltpu.SemaphoreType.DMA((2,))]),
        compiler_params=_cparams("arbitrary"),
        name="moe_combine_ple",
    )(run_len, run_loc, run_dst, os_, pos, gate, h1, pp, ps, gple, wpg, wpp, gfin)


def _routing_tables(cnt, total_rows):
    n_tiles = cnt.shape[0]
    run_len = (cnt + (ROW_ALIGN - 1)) // ROW_ALIGN * ROW_ALIGN
    run_loc = jnp.cumsum(run_len, axis=1) - run_len
    per_expert = jnp.sum(run_len, axis=0)
    region = (per_expert + (FFN_ROWS - 1)) // FFN_ROWS * FFN_ROWS
    region_end = jnp.cumsum(region)
    region_start = region_end - region
    run_dst = region_start[None, :] + jnp.cumsum(run_len, axis=0) - run_len
    nblk = total_rows // FFN_ROWS
    blk_row = jnp.arange(nblk, dtype=I32) * FFN_ROWS
    used = region_end[-1] // FFN_ROWS
    last = jnp.maximum(used - 1, 0)
    bexp = jnp.minimum(jnp.sum(blk_row[:, None] >= region_end[None, :], axis=1), N_EXPERTS - 1).astype(I32)
    nval = jnp.clip(per_expert[bexp] - (blk_row - region_start[bexp]), 0, FFN_ROWS)
    nval = jnp.where(blk_row < region_end[-1], nval, 0).astype(I32)
    bexp = jnp.where(blk_row < region_end[-1], bexp, bexp[last])
    flat = lambda a: (a // ROW_ALIGN).reshape(n_tiles * N_EXPERTS).astype(I32)
    return flat(run_len), flat(run_loc), flat(run_dst), bexp, nval, last.reshape(1).astype(I32)


def _chunked_t(lf, chunk):
    b, length, h = lf.shape
    return lf.reshape(b, length // chunk, chunk, h).transpose(0, 1, 3, 2)


def kernel(x_prompt, x_sample, p_prompt, p_sample, cache_k, cache_v, cache_logf, state_conv, g_mix, w_in, b_f, w_conv, w_pa, w_pb, w_o, g_ffn, w_router, b_router, w_gu, b_gu, w_dn, b_dn, g_ple, w_ple_gate, w_ple_proj, g_final):
    depth = g_mix.shape[0]
    assert depth == 1
    nb, seq, d = x_prompt.shape
    db, dseq, _ = x_sample.shape
    past = cache_k.shape[2]
    n_p, n_s = nb * seq, db * dseq

    w = w_in[0].astype(BF16)
    o_f = 3 * FOX_DIM
    o_c = o_f + HEADS
    o_g = o_c + 3 * CONV_DIM
    wf = jnp.pad(w[:, o_f:o_c], ((0, 0), (0, LANES - HEADS)))
    bf = jnp.pad(b_f[0], (0, LANES - HEADS)).reshape(1, LANES)
    mixer_w = (g_mix[0].reshape(1, d), w[:, :o_f], wf, bf, w[:, o_c:o_g], w_conv[0], w[:, o_g:], w_pb[0].astype(BF16))

    (q_p, k_p, v_p, ko_p, vo_p, logf_p, lfp_p, sa_p, mb_p), tail_p = _in_proj(
        x_prompt.reshape(n_p, d), jnp.zeros((nb, CONV_WIDTH - 1, CONV_DIM), F32), mixer_w, seq_len=seq)
    o_p = _attn_prompt(q_p, k_p, v_p, _cumsum_rows(lfp_p, seq=seq), nb=nb, seq=seq)

    (q_s, k_s, v_s, ko_s, vo_s, logf_s, _, sa_s, mb_s), tail_s = _in_proj(
        x_sample.reshape(n_s, d), state_conv[0], mixer_w, seq_len=dseq)
    chunk = min(past, 4 * TOKEN_TILE)
    lf_all = jnp.concatenate([cache_logf[0].astype(F32), logf_s.reshape(db, dseq, HEADS),
                              jnp.zeros((db, chunk - dseq, HEADS), F32)], axis=1)
    c_s = _cumsum_chunks(_chunked_t(lf_all, chunk))
    o_s = _attn_sample(q_s, cache_k[0].transpose(0, 2, 3, 1), cache_v[0].transpose(0, 2, 3, 1), k_s, v_s, c_s, nq=dseq)

    h1, xn2, pos_t, gate_t, cnt = _post_attn(
        x_prompt.reshape(n_p, d), x_sample.reshape(n_s, d), o_p, o_s, sa_p, sa_s, mb_p, mb_s,
        w_pa[0].astype(BF16), w_o[0].astype(BF16), g_ffn[0].reshape(1, d),
        w_router[0].T.astype(BF16), b_router[0].reshape(N_EXPERTS, 1))

    n = n_p + n_s
    n_tiles = n // DISPATCH_TILE
    max_rows = n * TOP_K + n_tiles * N_EXPERTS * (ROW_ALIGN - 1) + N_EXPERTS * (FFN_ROWS - 1)
    total_rows = -(-max_rows // FFN_ROWS) * FFN_ROWS
    run_len, run_loc, run_dst, bexp, nval, last = _routing_tables(cnt[:, :, 0], total_rows)
    xs = _dispatch(run_len, run_loc, run_dst, xn2, pos_t, total_rows=total_rows)
    os_ = _ffn(bexp, nval, last, xs.reshape(total_rows, d), w_gu[0], b_gu[0].reshape(N_EXPERTS, 1, 2 * D_FF),
               w_dn[0], b_dn[0].reshape(N_EXPERTS, 1, d))

    y_p, y_s = _combine(run_len, run_loc, run_dst, os_.reshape(total_rows // ROW_ALIGN, ROW_ALIGN, d), pos_t.T, gate_t.T, h1,
                        p_prompt[0].reshape(n_p, -1), p_sample[0].reshape(n_s, -1),
                        g_ple[0].reshape(1, d), w_ple_gate[0].astype(BF16), w_ple_proj[0].astype(BF16),
                        g_final.reshape(1, d))

    return (y_p.reshape(nb, seq, d), y_s.reshape(db, dseq, d),
            ko_p.transpose(0, 3, 1, 2)[None], vo_p.transpose(0, 3, 1, 2)[None],
            logf_p.reshape(1, nb, seq, HEADS), tail_p[None],
            ko_s.reshape(1, db, dseq, HEADS, HEAD_DIM), vo_s.reshape(1, db, dseq, HEADS, HEAD_DIM),
            logf_s.reshape(1, db, dseq, HEADS), tail_s[None])
```
